```python
import jax, jax.numpy as jnp
from jax import lax
import numpy as np

D_MODEL = 2048
BATCH = 4
SEQ = 4096
DEPTH = 1

N_Q_HEADS = 16
N_KV_HEADS = 4
HEAD_DIM = 64
Q_WIDTH = N_Q_HEADS * HEAD_DIM
KV_WIDTH = N_KV_HEADS * HEAD_DIM
WINDOW = 128
Q_BLOCK = 128
N_FOURIER_GROUPS = 4
FOURIER_GROUP_DIM = 256
FOURIER_WIDTH = N_FOURIER_GROUPS * FOURIER_GROUP_DIM
N_BRANCHES = 2
IN_WIDTH = FOURIER_WIDTH + Q_WIDTH + 2 * KV_WIDTH + N_BRANCHES * D_MODEL
N_EXPERTS = 32
TOP_K = 4
D_EXPERT = D_MODEL
SWIGLU_LIMIT = 7.0
SWIGLU_ALPHA = 1.702
MOE_BLOCK = 128
RMS_EPS = 1e-6
NEG_INF = -1e30

kernel_name = "hybrid_fnet_swa_moe_encoder"


def rms_norm(x, g):
    xf = x.astype(jnp.float32)
    y = xf * lax.rsqrt(jnp.mean(xf * xf, axis=-1, keepdims=True) + RMS_EPS)
    return (y * g.astype(jnp.float32)).astype(x.dtype)


def alibi_slopes(n_heads):
    return jnp.asarray(2.0 ** (-8.0 * np.arange(1, n_heads + 1) / n_heads), dtype=jnp.float32)


def fourier_mix(u):
    b, s, _ = u.shape
    ug = u.astype(jnp.float32).reshape(b, s, N_FOURIER_GROUPS, FOURIER_GROUP_DIM)
    y = jnp.fft.fft2(ug, axes=(1, 3), norm="ortho").real
    return y.reshape(b, s, FOURIER_WIDTH).astype(u.dtype)


def windowed_gqa_attention(q, k, v, q_norm_g, k_norm_g, sink):
    b, s = q.shape[:2]
    nb = s // Q_BLOCK
    grp = N_Q_HEADS // N_KV_HEADS
    q = rms_norm(q, q_norm_g)
    k = rms_norm(k, k_norm_g)
    qb = q.reshape(b, nb, Q_BLOCK, N_KV_HEADS, grp, HEAD_DIM)
    pad = ((0, 0), (Q_BLOCK, Q_BLOCK), (0, 0), (0, 0))
    kp = jnp.pad(k, pad).reshape(b, nb + 2, Q_BLOCK, N_KV_HEADS, HEAD_DIM)
    vp = jnp.pad(v, pad).reshape(b, nb + 2, Q_BLOCK, N_KV_HEADS, HEAD_DIM)
    kw = jnp.concatenate([kp[:, :-2], kp[:, 1:-1], kp[:, 2:]], axis=2)
    vw = jnp.concatenate([vp[:, :-2], vp[:, 1:-1], vp[:, 2:]], axis=2)
    scale = HEAD_DIM ** -0.5
    scores = jnp.einsum('bnqkgd,bnskd->bnkgqs', qb.astype(jnp.float32),
                        kw.astype(jnp.float32)) * scale
    qi = jnp.arange(Q_BLOCK)[:, None]
    kj = jnp.arange(3 * Q_BLOCK)[None, :]
    rel = qi - kj + Q_BLOCK
    key_pos = (jnp.arange(nb) * Q_BLOCK)[:, None, None] - Q_BLOCK + kj[None]
    valid = (jnp.abs(rel) <= WINDOW)[None] & (key_pos >= 0) & (key_pos < s)
    slopes = alibi_slopes(N_Q_HEADS).reshape(N_KV_HEADS, grp, 1, 1)
    scores = scores - slopes * jnp.abs(rel).astype(jnp.float32)
    scores = jnp.where(valid[None, :, None, None], scores, NEG_INF)
    sink_l = sink.astype(jnp.float32).reshape(N_KV_HEADS, grp, 1, 1)
    m = jnp.maximum(jnp.max(scores, axis=-1, keepdims=True), sink_l)
    p = jnp.exp(scores - m)
    probs = p / (jnp.sum(p, axis=-1, keepdims=True) + jnp.exp(sink_l - m))
    out = jnp.einsum('bnkgqs,bnskd->bnqkgd', probs.astype(v.dtype), vw)
    return out.reshape(b, s, Q_WIDTH)


def moe_ffn(h, w_router, b_router, w_gate_e, b_gate_e, w_up_e, b_up_e, w_down_e, b_down_e):
    b, s, d = h.shape
    n = b * s
    xt = h.reshape(n, d)
    logits = (xt @ w_router + b_router).astype(jnp.float32)
    top_vals, top_idx = lax.top_k(logits, TOP_K)
    gates = jax.nn.softmax(top_vals, axis=-1)
    a = n * TOP_K
    flat_e = top_idx.reshape(a)
    flat_t = jnp.repeat(jnp.arange(n, dtype=jnp.int32), TOP_K)
    flat_g = gates.reshape(a)
    order = jnp.argsort(flat_e)
    se, st, sg = flat_e[order], flat_t[order], flat_g[order]
    counts = jnp.zeros((N_EXPERTS,), jnp.int32).at[flat_e].add(1)
    padded = ((counts + MOE_BLOCK - 1) // MOE_BLOCK) * MOE_BLOCK
    pend = jnp.cumsum(padded)
    pstart = pend - padded
    ustart = jnp.cumsum(counts) - counts
    dest = pstart[se] + jnp.arange(a, dtype=jnp.int32) - ustart[se]
    n_blocks = -(-a // MOE_BLOCK) + N_EXPERTS
    rows = n_blocks * MOE_BLOCK
    row_tok = jnp.full((rows,), n, jnp.int32).at[dest].set(st)
    row_gate = jnp.zeros((rows,), jnp.float32).at[dest].set(sg)
    blk_start = jnp.arange(n_blocks, dtype=jnp.int32) * MOE_BLOCK
    blk_exp = jnp.minimum(jnp.searchsorted(pend, blk_start, side='right'), N_EXPERTS - 1)
    x_pad = jnp.concatenate([xt, jnp.zeros((1, d), xt.dtype)], axis=0)
    x_rows = x_pad[row_tok].reshape(n_blocks, MOE_BLOCK, d)

    def expert_block(args):
        xb, e = args
        g = xb @ w_gate_e[e] + b_gate_e[e]
        u = xb @ w_up_e[e] + b_up_e[e]
        g = jnp.minimum(g, SWIGLU_LIMIT)
        u = jnp.clip(u, -SWIGLU_LIMIT, SWIGLU_LIMIT)
        act = (u + 1.0) * (g * jax.nn.sigmoid(SWIGLU_ALPHA * g))
        return act @ w_down_e[e] + b_down_e[e]

    y_rows = lax.map(expert_block, (x_rows, blk_exp)).reshape(rows, d)
    y_rows = y_rows * row_gate[:, None].astype(y_rows.dtype)
    y = jax.ops.segment_sum(y_rows, row_tok, num_segments=n + 1)[:n]
    return y.reshape(b, s, d)


def setup_inputs(seed: int = 0) -> dict:
    key = jax.random.key(seed)
    ks = jax.random.split(key, 24)
    f32 = jnp.float32
    L = DEPTH

    def nrm(k, shape, scale):
        return jax.random.normal(k, shape, f32) * scale

    return {
        "x": nrm(ks[0], (BATCH, SEQ, D_MODEL), 1.0),
        "norm1_g": 1.0 + nrm(ks[1], (L, D_MODEL), 0.02),
        "w_in": nrm(ks[2], (L, D_MODEL, IN_WIDTH), D_MODEL ** -0.5),
        "b_branch_gate": nrm(ks[3], (L, N_BRANCHES * D_MODEL), 0.01),
        "q_norm_g": 1.0 + nrm(ks[4], (L, HEAD_DIM), 0.02),
        "k_norm_g": 1.0 + nrm(ks[5], (L, HEAD_DIM), 0.02),
        "attn_sink": nrm(ks[6], (L, N_Q_HEADS), 0.5),
        "w_fourier_out": nrm(ks[7], (L, FOURIER_WIDTH, D_MODEL), FOURIER_WIDTH ** -0.5),
        "w_attn_out": nrm(ks[8], (L, Q_WIDTH, D_MODEL), Q_WIDTH ** -0.5),
        "w_o": nrm(ks[9], (L, D_MODEL, D_MODEL), D_MODEL ** -0.5),
        "norm2_g": 1.0 + nrm(ks[10], (L, D_MODEL), 0.02),
        "w_router": nrm(ks[11], (L, D_MODEL, N_EXPERTS), D_MODEL ** -0.5),
        "b_router": nrm(ks[12], (L, N_EXPERTS), 0.01),
        "w_gate_e": nrm(ks[13], (L, N_EXPERTS, D_MODEL, D_EXPERT), D_MODEL ** -0.5),
        "b_gate_e": nrm(ks[14], (L, N_EXPERTS, D_EXPERT), 0.01),
        "w_up_e": nrm(ks[15], (L, N_EXPERTS, D_MODEL, D_EXPERT), D_MODEL ** -0.5),
        "b_up_e": nrm(ks[16], (L, N_EXPERTS, D_EXPERT), 0.01),
        "w_down_e": nrm(ks[17], (L, N_EXPERTS, D_EXPERT, D_MODEL), D_EXPERT ** -0.5),
        "b_down_e": nrm(ks[18], (L, N_EXPERTS, D_MODEL), 0.01),
    }


def reference(x, norm1_g, w_in, b_branch_gate, q_norm_g, k_norm_g, attn_sink,
              w_fourier_out, w_attn_out, w_o, norm2_g, w_router, b_router,
              w_gate_e, b_gate_e, w_up_e, b_up_e, w_down_e, b_down_e):
    b, s, _ = x.shape
    o_q = FOURIER_WIDTH
    o_k = o_q + Q_WIDTH
    o_v = o_k + KV_WIDTH
    o_g = o_v + KV_WIDTH
    for layer in range(DEPTH):
        h = rms_norm(x, norm1_g[layer])
        proj = h @ w_in[layer]
        u_f = proj[..., :o_q]
        q = proj[..., o_q:o_k].reshape(b, s, N_Q_HEADS, HEAD_DIM)
        k = proj[..., o_k:o_v].reshape(b, s, N_KV_HEADS, HEAD_DIM)
        v = proj[..., o_v:o_g].reshape(b, s, N_KV_HEADS, HEAD_DIM)
        gate = jax.nn.sigmoid(proj[..., o_g:] + b_branch_gate[layer])
        y_f = fourier_mix(u_f) @ w_fourier_out[layer]
        y_a = windowed_gqa_attention(q, k, v, q_norm_g[layer], k_norm_g[layer],
                                     attn_sink[layer]) @ w_attn_out[layer]
        merged = gate[..., :D_MODEL] * y_f + gate[..., D_MODEL:] * y_a
        x = x + merged @ w_o[layer]
        h2 = rms_norm(x, norm2_g[layer])
        x = x + moe_ffn(h2, w_router[layer], b_router[layer], w_gate_e[layer], b_gate_e[layer],
                        w_up_e[layer], b_up_e[layer], w_down_e[layer], b_down_e[layer])
    return x
```

```python
import functools
import math

import jax
import jax.numpy as jnp
import numpy as np
from jax import lax
from jax.experimental import pallas as pl
from jax.experimental.pallas import tpu as pltpu

D_MODEL = 2048
BATCH = 4
SEQ = 4096
N_TOK = BATCH * SEQ
N_Q_HEADS = 16
N_KV_HEADS = 4
HEAD_DIM = 64
Q_WIDTH = N_Q_HEADS * HEAD_DIM
KV_WIDTH = N_KV_HEADS * HEAD_DIM
WINDOW = 128
N_FOURIER_GROUPS = 4
FOURIER_GROUP_DIM = 256
FOURIER_WIDTH = N_FOURIER_GROUPS * FOURIER_GROUP_DIM
IN_WIDTH = FOURIER_WIDTH + Q_WIDTH + 2 * KV_WIDTH + 2 * D_MODEL
N_EXPERTS = 32
TOP_K = 4
D_EXPERT = D_MODEL
SWIGLU_LIMIT = 7.0
SWIGLU_ALPHA = 1.702
RMS_EPS = 1e-6
NEG_INF = -1e30

F32 = jnp.float32
BF16 = jnp.bfloat16
U32 = jnp.uint32

V7X_VMEM_BYTES = 64 * 1024 * 1024
LANES = 128
MIB = 1024 * 1024

TM_IN = 1024
TN_IN = 512
NORM_ROWS = 64
TT_F = 512
DFT_RADIX = 64
TQ = 512
QB = 128
TM_MG = 256
RT = 256
R_MAX = 9 * RT
FC = 256
NF = D_EXPERT // FC
ROWS = TOP_K * N_TOK + N_EXPERTS * RT
N_ITEMS = (TOP_K * N_TOK + N_EXPERTS * (RT - 1) + N_EXPERTS * (R_MAX - RT)) // R_MAX + 1
TM_CB = 256
HALF = D_MODEL // 2


def _cparams(sem, vmem_mib):
    return pltpu.CompilerParams(dimension_semantics=sem, vmem_limit_bytes=vmem_mib * MIB)


def _dot(a, b):
    return jnp.dot(a, b, preferred_element_type=F32)


def _pack_halves(lo, hi):
    lo_bits = pltpu.bitcast(lo.astype(BF16).astype(F32), U32)
    hi_bits = pltpu.bitcast(hi.astype(BF16).astype(F32), U32)
    return (hi_bits & jnp.uint32(0xFFFF0000)) | (lo_bits >> jnp.uint32(16))


def _unpack_halves(words):
    lo = pltpu.bitcast(words << jnp.uint32(16), F32)
    hi = pltpu.bitcast(words & jnp.uint32(0xFFFF0000), F32)
    return lo, hi


def _in_proj_kernel(x_ref, g1_ref, w_ref, b_ref, qg_ref, kg_ref, ones_ref,
                    u_ref, q_ref, kv_ref, gate_ref, h_scr):
    j = pl.program_id(1)

    @pl.when(j == 0)
    def _():
        def body(c, carry):
            rows = pl.ds(pl.multiple_of(c * NORM_ROWS, NORM_ROWS), NORM_ROWS)
            x = x_ref[rows, :]
            ms = jnp.mean(x * x, axis=-1, keepdims=True)
            h_scr[rows, :] = (x * lax.rsqrt(ms + RMS_EPS) * g1_ref[...]).astype(BF16)
            return carry
        lax.fori_loop(0, TM_IN // NORM_ROWS, body, 0)

    acc = _dot(h_scr[...], w_ref[...])

    def head_norm(a, gain):
        sq = a * a
        hi = sq.astype(BF16)
        lo = (sq - hi.astype(F32)).astype(BF16)
        ssq = _dot(hi, ones_ref[...]) + _dot(lo, ones_ref[...])
        return a * lax.rsqrt(ssq * (1.0 / HEAD_DIM) + RMS_EPS) * gain

    @pl.when(j < 2)
    def _():
        u_ref[...] = acc.astype(BF16)

    @pl.when((j == 2) | (j == 3))
    def _():
        q_ref[...] = (head_norm(acc, qg_ref[...]) * (HEAD_DIM ** -0.5)).astype(BF16)

    @pl.when(j == 4)
    def _():
        lane = lax.broadcasted_iota(jnp.int32, acc.shape, 1)
        kv_ref[...] = jnp.where(lane < KV_WIDTH, head_norm(acc, kg_ref[...]), acc).astype(BF16)

    @pl.when(j >= 5)
    def _():
        z = acc + b_ref[...]
        gate_ref[...] = (1.0 / (1.0 + jnp.exp(-z))).astype(BF16)


def _in_proj(x2, g1, w_bf, bias, qg_t, kg_t, ones_bd):
    n_col = IN_WIDTH // TN_IN
    grid = (N_TOK // TM_IN, n_col)
    return pl.pallas_call(
        _in_proj_kernel,
        grid=grid,
        in_specs=[
            pl.BlockSpec((TM_IN, D_MODEL), lambda i, j: (i, 0)),
            pl.BlockSpec((1, D_MODEL), lambda i, j: (0, 0)),
            pl.BlockSpec((D_MODEL, TN_IN), lambda i, j: (0, j)),
            pl.BlockSpec((1, TN_IN), lambda i, j: (0, jnp.maximum(j - 5, 0))),
            pl.BlockSpec((1, TN_IN), lambda i, j: (0, 0)),
            pl.BlockSpec((1, TN_IN), lambda i, j: (0, 0)),
            pl.BlockSpec((TN_IN, TN_IN), lambda i, j: (0, 0)),
        ],
        out_specs=[
            pl.BlockSpec((TM_IN, TN_IN), lambda i, j: (i, jnp.minimum(j, 1))),
            pl.BlockSpec((TM_IN, TN_IN), lambda i, j: (i, jnp.clip(j - 2, 0, 1))),
            pl.BlockSpec((TM_IN, TN_IN), lambda i, j: (i, 0)),
            pl.BlockSpec((TM_IN, TN_IN), lambda i, j: (i, jnp.maximum(j - 5, 0))),
        ],
        out_shape=[
            jax.ShapeDtypeStruct((N_TOK, FOURIER_WIDTH), BF16),
            jax.ShapeDtypeStruct((N_TOK, Q_WIDTH), BF16),
            jax.ShapeDtypeStruct((N_TOK, 2 * KV_WIDTH), BF16),
            jax.ShapeDtypeStruct((N_TOK, 2 * D_MODEL), BF16),
        ],
        scratch_shapes=[pltpu.VMEM((TM_IN, D_MODEL), BF16)],
        compiler_params=_cparams(("arbitrary", "arbitrary"), 56),
        name="in_proj",
    )(x2, g1, w_bf, bias, qg_t, kg_t, ones_bd)


def _fourier_kernel(u_ref, f1r_ref, f1i_ref, gr_ref, gi_ref, cc_ref, sc_ref, y_ref, cs_scr, ss_scr):
    b = pl.program_id(1)

    @pl.when(b == 0)
    def _():
        gr = gr_ref[...]
        gi = gi_ref[...]
        for a in range(TT_F // DFT_RADIX):
            f1r = f1r_ref[a:a + 1, :]
            f1i = f1i_ref[a:a + 1, :]
            rows = slice(a * DFT_RADIX, (a + 1) * DFT_RADIX)
            cs_scr[rows, :] = (f1r * gr - f1i * gi).astype(BF16)
            ss_scr[rows, :] = (f1r * gi + f1i * gr).astype(BF16)

    u = u_ref[...]
    a_seq = _dot(cs_scr[...], u)
    b_seq = _dot(ss_scr[...], u)
    for g in range(N_FOURIER_GROUPS):
        cols = slice(g * FOURIER_GROUP_DIM, (g + 1) * FOURIER_GROUP_DIM)
        y = _dot(a_seq[:, cols].astype(BF16), cc_ref[...]) - _dot(b_seq[:, cols].astype(BF16), sc_ref[...])
        y_ref[:, cols] = y.astype(BF16)


def _fourier(u, f1r, f1i, gr, gi, cc, sc):
    nt = SEQ // TT_F
    f1_rows = TT_F // DFT_RADIX
    return pl.pallas_call(
        _fourier_kernel,
        grid=(nt, BATCH),
        in_specs=[
            pl.BlockSpec((SEQ, FOURIER_WIDTH), lambda t, b: (b, 0)),
            pl.BlockSpec((f1_rows, SEQ), lambda t, b: (t, 0)),
            pl.BlockSpec((f1_rows, SEQ), lambda t, b: (t, 0)),
            pl.BlockSpec((DFT_RADIX, SEQ), lambda t, b: (0, 0)),
            pl.BlockSpec((DFT_RADIX, SEQ), lambda t, b: (0, 0)),
            pl.BlockSpec((FOURIER_GROUP_DIM, FOURIER_GROUP_DIM), lambda t, b: (0, 0)),
            pl.BlockSpec((FOURIER_GROUP_DIM, FOURIER_GROUP_DIM), lambda t, b: (0, 0)),
        ],
        out_specs=pl.BlockSpec((TT_F, FOURIER_WIDTH), lambda t, b: (b * nt + t, 0)),
        out_shape=jax.ShapeDtypeStruct((N_TOK, FOURIER_WIDTH), BF16),
        scratch_shapes=[pltpu.VMEM((TT_F, SEQ), BF16), pltpu.VMEM((TT_F, SEQ), BF16)],
        compiler_params=_cparams(("arbitrary", "arbitrary"), 56),
        name="fourier",
    )(u, f1r, f1i, gr, gi, cc, sc)


def _alibi_slope(h):
    return float(2.0 ** (-8.0 * (h + 1) / N_Q_HEADS))


def _attention_kernel(sink_ref, q_ref, prev_ref, cur_ref, next_ref, sel_ref, o_ref, kz_scr, vz_scr):
    t = pl.program_id(1)
    band = jnp.concatenate([prev_ref[...], cur_ref[...], next_ref[...]], axis=0)
    for g in range(N_KV_HEADS):
        c, half = divmod(g, 2)
        kcol = band[:, c * LANES:(c + 1) * LANES]
        vcol = band[:, KV_WIDTH + c * LANES:KV_WIDTH + (c + 1) * LANES]
        for d in range(2):
            sel = sel_ref[half * 2 + d]
            kz_scr[2 * g + d] = _dot(kcol, sel).astype(BF16)
            vz_scr[2 * g + d] = _dot(vcol, sel).astype(BF16)

    def sub_block(i, carry):
        q_rows = pl.ds(pl.multiple_of(i * QB, QB), QB)
        k_rows = pl.ds(pl.multiple_of(i * QB, QB), 3 * QB)
        row = lax.broadcasted_iota(jnp.int32, (QB, 3 * QB), 0)
        col = lax.broadcasted_iota(jnp.int32, (QB, 3 * QB), 1)
        rel = row - col + QB
        absrel = jnp.abs(rel)
        kpos = t * TQ + i * QB - QB + col
        mask = (absrel <= WINDOW) & (kpos >= 0) & (kpos < SEQ)
        absrel_f = absrel.astype(F32)
        for g in range(N_KV_HEADS):
            for p in range(2):
                cols = slice((2 * g + p) * LANES, (2 * g + p + 1) * LANES)
                qc = q_ref[q_rows, cols]
                out = jnp.zeros((QB, LANES), F32)
                for d in range(2):
                    h = 4 * g + 2 * p + d
                    kz = kz_scr[2 * g + d, k_rows, :]
                    s = lax.dot_general(qc, kz, (((1,), (1,)), ((), ())), preferred_element_type=F32)
                    s = jnp.where(mask, s - _alibi_slope(h) * absrel_f, NEG_INF)
                    sink = sink_ref[h]
                    m = jnp.maximum(jnp.max(s, axis=-1, keepdims=True), sink)
                    pr = jnp.exp(s - m)
                    den = jnp.sum(pr, axis=-1, keepdims=True) + jnp.exp(sink - m)
                    out = out + _dot(pr.astype(BF16), vz_scr[2 * g + d, k_rows, :]) / den
                o_ref[q_rows, cols] = out.astype(BF16)
        return carry

    lax.fori_loop(0, TQ // QB, sub_block, 0)


def _attention(sink, q, kv, sel):
    nt = SEQ // TQ
    nb = SEQ // QB
    r = TQ // QB
    grid_spec = pltpu.PrefetchScalarGridSpec(
        num_scalar_prefetch=1,
        grid=(BATCH, nt),
        in_specs=[
            pl.BlockSpec((TQ, Q_WIDTH), lambda b, t, s: (b * nt + t, 0)),
            pl.BlockSpec((QB, 2 * KV_WIDTH), lambda b, t, s: (b * nb + jnp.maximum(t * r - 1, 0), 0)),
            pl.BlockSpec((TQ, 2 * KV_WIDTH), lambda b, t, s: (b * nt + t, 0)),
            pl.BlockSpec((QB, 2 * KV_WIDTH), lambda b, t, s: (b * nb + jnp.minimum(t * r + r, nb - 1), 0)),
            pl.BlockSpec((4, LANES, LANES), lambda b, t, s: (0, 0, 0)),
        ],
        out_specs=pl.BlockSpec((TQ, Q_WIDTH), lambda b, t, s: (b * nt + t, 0)),
        scratch_shapes=[pltpu.VMEM((2 * N_KV_HEADS, TQ + 2 * QB, LANES), BF16),
                        pltpu.VMEM((2 * N_KV_HEADS, TQ + 2 * QB, LANES), BF16)],
    )
    return pl.pallas_call(
        _attention_kernel,
        grid_spec=grid_spec,
        out_shape=jax.ShapeDtypeStruct((N_TOK, Q_WIDTH), BF16),
        compiler_params=_cparams(("arbitrary", "arbitrary"), 32),
        name="attention",
    )(sink, q, kv, kv, kv, sel)


def _merge_kernel(yf_ref, at_ref, gf_ref, ga_ref, x_ref, wfo_ref, wao_ref, wo_ref, g2_ref, wr_ref, br_ref,
                  ltri_ref, x1_ref, h2p_ref, meta_ref, cnt_ref, base_scr):
    i = pl.program_id(0)

    @pl.when(i == 0)
    def _():
        base_scr[...] = jnp.zeros_like(base_scr)

    yf = _dot(yf_ref[...], wfo_ref[...])
    ya = _dot(at_ref[...], wao_ref[...])
    merged = gf_ref[...].astype(F32) * yf + ga_ref[...].astype(F32) * ya
    x1 = x_ref[...] + _dot(merged.astype(BF16), wo_ref[...])
    x1_ref[...] = x1
    ms = jnp.mean(x1 * x1, axis=-1, keepdims=True)
    h2 = x1 * lax.rsqrt(ms + RMS_EPS) * g2_ref[...]
    h2p_ref[...] = _pack_halves(h2[:, :HALF], h2[:, HALF:])
    logits = _dot(h2.astype(BF16), wr_ref[...]) + br_ref[...]

    lane = lax.broadcasted_iota(jnp.int32, logits.shape, 1)
    lane_f = lane.astype(F32)
    vals = logits
    top_v, top_i, onehots = [], [], []
    for _ in range(TOP_K):
        m = jnp.max(vals, axis=-1, keepdims=True)
        idx = jnp.min(jnp.where(vals == m, lane_f, float(LANES)), axis=-1, keepdims=True)
        oh = lane_f == idx
        top_v.append(m)
        top_i.append(idx)
        onehots.append(oh)
        vals = jnp.where(oh, -jnp.inf, vals)
    exps = [jnp.exp(v - top_v[0]) for v in top_v]
    den = exps[0] + exps[1] + exps[2] + exps[3]
    gates = [e / den for e in exps]

    cnt = sum(oh.astype(F32) for oh in onehots)
    prefix = _dot(ltri_ref[...], cnt.astype(BF16))
    tot = base_scr[...] + prefix
    ranks = [jnp.sum(jnp.where(oh, tot, 0.0), axis=-1, keepdims=True) for oh in onehots]
    base_scr[...] = base_scr[...] + jnp.sum(cnt, axis=0, keepdims=True)
    cnt_ref[...] = jnp.broadcast_to(base_scr[...], cnt_ref.shape)

    meta = jnp.zeros(logits.shape, F32)
    for k in range(TOP_K):
        meta = jnp.where(lane == k, top_i[k], meta)
        meta = jnp.where(lane == TOP_K + k, ranks[k], meta)
        meta = jnp.where(lane == 2 * TOP_K + k, gates[k], meta)
    meta_ref[...] = meta


def _merge(yf, attn, gates, x2, wfo, wao, wo, g2, wr, br, ltri):
    tm = TM_MG
    const = lambda i: (0, 0)
    resident = functools.partial(pl.BlockSpec, index_map=const, pipeline_mode=pl.Buffered(1))
    return pl.pallas_call(
        _merge_kernel,
        grid=(N_TOK // tm,),
        in_specs=[
            pl.BlockSpec((tm, FOURIER_WIDTH), lambda i: (i, 0)),
            pl.BlockSpec((tm, Q_WIDTH), lambda i: (i, 0)),
            pl.BlockSpec((tm, D_MODEL), lambda i: (i, 0)),
            pl.BlockSpec((tm, D_MODEL), lambda i: (i, 1)),
            pl.BlockSpec((tm, D_MODEL), lambda i: (i, 0)),
            resident((FOURIER_WIDTH, D_MODEL)),
            resident((Q_WIDTH, D_MODEL)),
            resident((D_MODEL, D_MODEL)),
            pl.BlockSpec((1, D_MODEL), const),
            pl.BlockSpec((D_MODEL, LANES), const),
            pl.BlockSpec((1, LANES), const),
            pl.BlockSpec((tm, tm), const),
        ],
        out_specs=[
            pl.BlockSpec((tm, D_MODEL), lambda i: (i, 0)),
            pl.BlockSpec((tm, HALF), lambda i: (i, 0)),
            pl.BlockSpec((tm, LANES), lambda i: (i, 0)),
            pl.BlockSpec((8, LANES), const),
        ],
        out_shape=[
            jax.ShapeDtypeStruct((N_TOK, D_MODEL), F32),
            jax.ShapeDtypeStruct((N_TOK, HALF), U32),
            jax.ShapeDtypeStruct((N_TOK, LANES), F32),
            jax.ShapeDtypeStruct((8, LANES), F32),
        ],
        scratch_shapes=[pltpu.VMEM((1, LANES), F32)],
        compiler_params=_cparams(("arbitrary",), 56),
        name="merge",
    )(yf, attn, gates, gates, x2, wfo, wao, wo, g2, wr, br, ltri)


def _dispatch_kernel(ra_ref, h2p_hbm, xrows_hbm, zero_scr, sem):
    i = pl.program_id(0)
    base = i * RT

    def row_copy(tok, r):
        return pltpu.make_async_copy(h2p_hbm.at[pl.ds(tok, 1)], xrows_hbm.at[pl.ds(base + r, 1)], sem)

    live = ra_ref[0, 0, 0] >= 0

    @pl.when(jnp.logical_not(live))
    def _():
        zero_scr[...] = jnp.zeros_like(zero_scr)
        fill = pltpu.make_async_copy(zero_scr, xrows_hbm.at[pl.ds(pl.multiple_of(base, RT), RT)], sem)
        fill.start()
        fill.wait()

    @pl.when(live)
    def _():
        def start(r, carry):
            tok = jnp.maximum(ra_ref[0, 0, r], 0) // TOP_K
            row_copy(tok, r).start()
            return carry
        lax.fori_loop(0, RT, start, 0, unroll=8)

        def wait(r, carry):
            row_copy(0, r).wait()
            return carry
        lax.fori_loop(0, RT, wait, 0)


def _dispatch(row_assign3, h2p):
    return pl.pallas_call(
        _dispatch_kernel,
        grid=(ROWS // RT,),
        in_specs=[
            pl.BlockSpec((1, 1, RT), lambda i: (i, 0, 0), memory_space=pltpu.SMEM),
            pl.BlockSpec(memory_space=pl.ANY),
        ],
        out_specs=pl.BlockSpec(memory_space=pl.ANY),
        out_shape=jax.ShapeDtypeStruct((ROWS, HALF), U32),
        scratch_shapes=[pltpu.VMEM((RT, HALF), U32), pltpu.SemaphoreType.DMA(())],
        compiler_params=_cparams(("arbitrary",), 16),
        name="dispatch",
    )(row_assign3, h2p)


def _expert_kernel(e_ref, row0_ref, nsub_ref, live_ref,
                   xrows_hbm, wg_ref, bg_ref, wu_ref, bu_ref, wd_ref, bd_ref,
                   yrows_hbm, xu_scr, acc_scr, wgb, wub, wdb, sem):
    del e_ref, live_ref
    w = pl.program_id(0)
    f = pl.program_id(1)
    ns = nsub_ref[w]
    r0 = row0_ref[w]

    def sub_rows(s):
        return pl.ds(pl.multiple_of(s * RT, RT), RT)

    def hbm_rows(s):
        return pl.ds(pl.multiple_of(r0 + s * RT, RT), RT)

    def load_copy(s):
        return pltpu.make_async_copy(xrows_hbm.at[hbm_rows(s)], xu_scr.at[sub_rows(s)], sem.at[0])

    def store_copy(s):
        return pltpu.make_async_copy(xu_scr.at[sub_rows(s)], yrows_hbm.at[hbm_rows(s)], sem.at[1])

    def for_subtiles(fn):
        def body(s, carry):
            fn(s)
            return carry
        lax.fori_loop(0, ns, body, 0)

    def contribution(s):
        lo, hi = _unpack_halves(xu_scr[sub_rows(s), :])
        x_lo = lo.astype(BF16)
        x_hi = hi.astype(BF16)
        g = _dot(x_lo, wgb[:HALF, :]) + _dot(x_hi, wgb[HALF:, :]) + bg_ref[...]
        u = _dot(x_lo, wub[:HALF, :]) + _dot(x_hi, wub[HALF:, :]) + bu_ref[...]
        g = jnp.minimum(g, SWIGLU_LIMIT)
        u = jnp.clip(u, -SWIGLU_LIMIT, SWIGLU_LIMIT)
        act = (u + 1.0) * (g * (1.0 / (1.0 + jnp.exp(-SWIGLU_ALPHA * g))))
        return _dot(act.astype(BF16), wdb[...])

    @pl.when(ns > 0)
    def _():
        @pl.when(f == 0)
        def _():
            for_subtiles(lambda s: load_copy(s).start())
            for_subtiles(lambda s: load_copy(s).wait())

        wgb[...] = wg_ref[...].astype(BF16)
        wub[...] = wu_ref[...].astype(BF16)
        wdb[...] = wd_ref[...].astype(BF16)

        @pl.when(f == 0)
        def _():
            def first(s):
                acc_scr[sub_rows(s), :] = contribution(s) + bd_ref[...]
            for_subtiles(first)

        @pl.when((f > 0) & (f < NF - 1))
        def _():
            def middle(s):
                acc_scr[sub_rows(s), :] += contribution(s)
            for_subtiles(middle)

        @pl.when(f == NF - 1)
        def _():
            def last(s):
                y = acc_scr[sub_rows(s), :] + contribution(s)
                xu_scr[sub_rows(s), :] = _pack_halves(y[:, :HALF], y[:, HALF:])
            for_subtiles(last)
            for_subtiles(lambda s: store_copy(s).start())
            for_subtiles(lambda s: store_copy(s).wait())


def _experts(item_e, item_row0, item_nsub, item_live, x_rows, wg, bg, wu, bu, wd, bd):
    def w_in_map(w, f, e, r0, ns, lv):
        return (e[w], 0, jnp.where(lv[w] > 0, f, NF - 1))

    def w_down_map(w, f, e, r0, ns, lv):
        return (e[w], jnp.where(lv[w] > 0, f, NF - 1), 0)

    def b_down_map(w, f, e, r0, ns, lv):
        return (e[w], 0, 0)

    grid_spec = pltpu.PrefetchScalarGridSpec(
        num_scalar_prefetch=4,
        grid=(N_ITEMS, NF),
        in_specs=[
            pl.BlockSpec(memory_space=pl.ANY),
            pl.BlockSpec((None, D_MODEL, FC), w_in_map),
            pl.BlockSpec((None, 1, FC), w_in_map),
            pl.BlockSpec((None, D_MODEL, FC), w_in_map),
            pl.BlockSpec((None, 1, FC), w_in_map),
            pl.BlockSpec((None, FC, D_MODEL), w_down_map),
            pl.BlockSpec((None, 1, D_MODEL), b_down_map),
        ],
        out_specs=pl.BlockSpec(memory_space=pl.ANY),
        scratch_shapes=[
            pltpu.VMEM((R_MAX, HALF), U32),
            pltpu.VMEM((R_MAX, D_MODEL), F32),
            pltpu.VMEM((D_MODEL, FC), BF16),
            pltpu.VMEM((D_MODEL, FC), BF16),
            pltpu.VMEM((FC, D_MODEL), BF16),
            pltpu.SemaphoreType.DMA((2,)),
        ],
    )
    return pl.pallas_call(
        _expert_kernel,
        grid_spec=grid_spec,
        out_shape=jax.ShapeDtypeStruct((ROWS, HALF), U32),
        input_output_aliases={4: 0},
        compiler_params=_cparams(("arbitrary", "arbitrary"), 58),
        name="experts",
    )(item_e, item_row0, item_nsub, item_live, x_rows, wg, bg, wu, bu, wd, bd)


def _combine_kernel(dest_ref, meta_ref, x1_ref, yrows_hbm, out_ref, ybuf, sem):
    def row_copy(d, k, t):
        return pltpu.make_async_copy(yrows_hbm.at[pl.ds(d, 1)], ybuf.at[k, pl.ds(t, 1)], sem)

    def start(t, carry):
        for k in range(TOP_K):
            row_copy(dest_ref[0, 0, TOP_K * t + k], k, t).start()
        return carry
    lax.fori_loop(0, TM_CB, start, 0, unroll=2)

    def wait(t, carry):
        for k in range(TOP_K):
            row_copy(0, k, t).wait()
        return carry
    lax.fori_loop(0, TM_CB, wait, 0)

    lo_acc = x1_ref[:, :HALF]
    hi_acc = x1_ref[:, HALF:]
    for k in range(TOP_K):
        gate = meta_ref[:, 2 * TOP_K + k:2 * TOP_K + k + 1]
        lo, hi = _unpack_halves(ybuf[k])
        lo_acc = lo_acc + gate * lo
        hi_acc = hi_acc + gate * hi
    out_ref[:, :HALF] = lo_acc
    out_ref[:, HALF:] = hi_acc


def _combine(dest3, meta, x1, y_rows):
    tm = TM_CB
    return pl.pallas_call(
        _combine_kernel,
        grid=(N_TOK // tm,),
        in_specs=[
            pl.BlockSpec((1, 1, TOP_K * tm), lambda i: (i, 0, 0), memory_space=pltpu.SMEM),
            pl.BlockSpec((tm, LANES), lambda i: (i, 0)),
            pl.BlockSpec((tm, D_MODEL), lambda i: (i, 0)),
            pl.BlockSpec(memory_space=pl.ANY),
        ],
        out_specs=pl.BlockSpec((tm, D_MODEL), lambda i: (i, 0)),
        out_shape=jax.ShapeDtypeStruct((N_TOK, D_MODEL), F32),
        scratch_shapes=[pltpu.VMEM((TOP_K, tm, HALF), U32), pltpu.SemaphoreType.DMA(())],
        compiler_params=_cparams(("arbitrary",), 32),
        name="combine",
    )(dest3, meta, x1, y_rows)


def _dft_tables():
    r = DFT_RADIX
    k = np.arange(r)
    e64 = np.exp(2j * np.pi * np.outer(k, k) / r)
    e4096 = np.exp(2j * np.pi * np.outer(k, k) / SEQ)
    s = np.arange(SEQ)
    f1 = e64[:, s % r]
    g = e64[:, s // r] * e4096[:, s % r] / math.sqrt(SEQ)
    c = np.arange(FOURIER_GROUP_DIM)
    ang = 2.0 * np.pi * np.outer(c, c) / FOURIER_GROUP_DIM
    scale = 1.0 / math.sqrt(FOURIER_GROUP_DIM)
    as32 = lambda a: jnp.asarray(a.astype(np.float32))
    return (as32(f1.real), as32(f1.imag), as32(g.real), as32(g.imag),
            jnp.asarray((np.cos(ang) * scale).astype(np.float32)).astype(BF16),
            jnp.asarray((np.sin(ang) * scale).astype(np.float32)).astype(BF16))


def _head_selectors():
    i = np.arange(LANES)[:, None]
    j = np.arange(LANES)[None, :]
    out = np.zeros((4, LANES, LANES), np.float32)
    for src in range(2):
        for dst in range(2):
            out[2 * src + dst] = (i - HEAD_DIM * src == j - HEAD_DIM * dst) & (j // HEAD_DIM == dst)
    return jnp.asarray(out).astype(BF16)


def kernel(x, norm1_g, w_in, b_branch_gate, q_norm_g, k_norm_g, attn_sink, w_fourier_out, w_attn_out, w_o,
           norm2_g, w_router, b_router, w_gate_e, b_gate_e, w_up_e, b_up_e, w_down_e, b_down_e):
    b, s, d = x.shape
    assert (b, s, d) == (BATCH, SEQ, D_MODEL) and norm1_g.shape[0] == 1
    x2 = x.reshape(N_TOK, D_MODEL)

    heads_per_tile = TN_IN // HEAD_DIM
    ones_bd = jnp.asarray((np.arange(TN_IN)[:, None] // HEAD_DIM == np.arange(TN_IN)[None, :] // HEAD_DIM)
                          .astype(np.float32)).astype(BF16)
    u_f, q, kv, gates = _in_proj(
        x2, norm1_g[0].reshape(1, D_MODEL), w_in[0].astype(BF16), b_branch_gate[0].reshape(1, 2 * D_MODEL),
        jnp.tile(q_norm_g[0], heads_per_tile).reshape(1, TN_IN),
        jnp.tile(k_norm_g[0], heads_per_tile).reshape(1, TN_IN), ones_bd)
    y_f = _fourier(u_f, *_dft_tables())
    attn = _attention(attn_sink[0], q, kv, _head_selectors())

    wr = jnp.zeros((D_MODEL, LANES), BF16).at[:, :N_EXPERTS].set(w_router[0].astype(BF16))
    br = jnp.full((1, LANES), NEG_INF, F32).at[0, :N_EXPERTS].set(b_router[0])
    ltri = jnp.asarray(np.tril(np.ones((TM_MG, TM_MG), np.float32), -1)).astype(BF16)
    x1, h2p, meta, cnt = _merge(y_f, attn, gates, x2, w_fourier_out[0].astype(BF16), w_attn_out[0].astype(BF16),
                                w_o[0].astype(BF16), norm2_g[0].reshape(1, D_MODEL), wr, br, ltri)

    counts = cnt[0, :N_EXPERTS].astype(jnp.int32)
    top_idx = meta[:, 0:TOP_K].astype(jnp.int32)
    rank = meta[:, TOP_K:2 * TOP_K].astype(jnp.int32)
    padded = ((counts + RT - 1) // RT) * RT
    pend = jnp.cumsum(padded)
    pstart = pend - padded
    dest = pstart[top_idx] + rank
    n_assign = TOP_K * N_TOK
    row_assign = jnp.full((ROWS,), -1, jnp.int32).at[dest.reshape(n_assign)].set(
        jnp.arange(n_assign, dtype=jnp.int32))

    items_per_e = (padded + R_MAX - 1) // R_MAX
    it_end = jnp.cumsum(items_per_e)
    it_start = it_end - items_per_e
    total_items = it_end[-1]
    wi = jnp.arange(N_ITEMS, dtype=jnp.int32)
    e_w = jnp.minimum(jnp.searchsorted(it_end, wi, side="right"), N_EXPERTS - 1).astype(jnp.int32)
    j_w = wi - it_start[e_w]
    live = wi < total_items
    rows_w = jnp.clip(padded[e_w] - j_w * R_MAX, 0, R_MAX)
    e_last = e_w[jnp.maximum(total_items - 1, 0)]
    item_e = jnp.where(live, e_w, e_last).astype(jnp.int32)
    item_row0 = jnp.where(live, pstart[e_w] + j_w * R_MAX, 0).astype(jnp.int32)
    item_nsub = jnp.where(live, rows_w // RT, 0).astype(jnp.int32)
    item_live = live.astype(jnp.int32)

    x_rows = _dispatch(row_assign.reshape(ROWS // RT, 1, RT), h2p)
    y_rows = _experts(item_e, item_row0, item_nsub, item_live, x_rows,
                      w_gate_e[0], b_gate_e[0].reshape(N_EXPERTS, 1, D_EXPERT),
                      w_up_e[0], b_up_e[0].reshape(N_EXPERTS, 1, D_EXPERT),
                      w_down_e[0], b_down_e[0].reshape(N_EXPERTS, 1, D_MODEL))
    out = _combine(dest.reshape(N_TOK // TM_CB, 1, TOP_K * TM_CB), meta, x1, y_rows)
    return out.reshape(BATCH, SEQ, D_MODEL)
```

```python
import functools
import math

import jax
import jax.numpy as jnp
import numpy as np
from jax import lax
from jax.experimental import pallas as pl
from jax.experimental.pallas import tpu as pltpu

D_MODEL = 2048
BATCH = 4
SEQ = 4096
N_TOK = BATCH * SEQ
N_Q_HEADS = 16
N_KV_HEADS = 4
HEAD_DIM = 64
Q_WIDTH = N_Q_HEADS * HEAD_DIM
KV_WIDTH = N_KV_HEADS * HEAD_DIM
WINDOW = 128
N_FOURIER_GROUPS = 4
FOURIER_GROUP_DIM = 256
FOURIER_WIDTH = N_FOURIER_GROUPS * FOURIER_GROUP_DIM
IN_WIDTH = FOURIER_WIDTH + Q_WIDTH + 2 * KV_WIDTH + 2 * D_MODEL
N_EXPERTS = 32
TOP_K = 4
D_EXPERT = D_MODEL
SWIGLU_LIMIT = 7.0
SWIGLU_ALPHA = 1.702
RMS_EPS = 1e-6
NEG_INF = -1e30

F32 = jnp.float32
BF16 = jnp.bfloat16
U32 = jnp.uint32

V7X_VMEM_BYTES = 64 * 1024 * 1024
LANES = 128
MIB = 1024 * 1024

TM_IN = 1024
TN_IN = 512
NORM_ROWS = 64
TT_F = 512
DFT_RADIX = 64
TQ = 512
QB = 128
TM_MG = 256
RT = 256
R_MAX = 9 * RT
FC = 256
NF = D_EXPERT // FC
ROWS = TOP_K * N_TOK + N_EXPERTS * RT
N_ITEMS = (TOP_K * N_TOK + N_EXPERTS * (RT - 1) + N_EXPERTS * (R_MAX - RT)) // R_MAX + 1
TM_CB = 256
HALF = D_MODEL // 2


def _cparams(sem, vmem_mib):
    return pltpu.CompilerParams(dimension_semantics=sem, vmem_limit_bytes=vmem_mib * MIB)


def _dot(a, b):
    return jnp.dot(a, b, preferred_element_type=F32)


def _pack_halves(lo, hi):
    lo_bits = pltpu.bitcast(lo.astype(BF16).astype(F32), U32)
    hi_bits = pltpu.bitcast(hi.astype(BF16).astype(F32), U32)
    return (hi_bits & jnp.uint32(0xFFFF0000)) | (lo_bits >> jnp.uint32(16))


def _unpack_halves(words):
    lo = pltpu.bitcast(words << jnp.uint32(16), F32)
    hi = pltpu.bitcast(words & jnp.uint32(0xFFFF0000), F32)
    return lo, hi


def _in_proj_kernel(x_ref, g1_ref, w_ref, b_ref, qg_ref, kg_ref, ones_ref,
                    u_ref, q_ref, kv_ref, gate_ref, h_scr):
    j = pl.program_id(1)

    @pl.when(j == 0)
    def _():
        def body(c, carry):
            rows = pl.ds(pl.multiple_of(c * NORM_ROWS, NORM_ROWS), NORM_ROWS)
            x = x_ref[rows, :]
            ms = jnp.mean(x * x, axis=-1, keepdims=True)
            h_scr[rows, :] = (x * lax.rsqrt(ms + RMS_EPS) * g1_ref[...]).astype(BF16)
            return carry
        lax.fori_loop(0, TM_IN // NORM_ROWS, body, 0)

    acc = _dot(h_scr[...], w_ref[...])

    def head_norm(a, gain):
        sq = a * a
        hi = sq.astype(BF16)
        lo = (sq - hi.astype(F32)).astype(BF16)
        ssq = _dot(hi, ones_ref[...]) + _dot(lo, ones_ref[...])
        return a * lax.rsqrt(ssq * (1.0 / HEAD_DIM) + RMS_EPS) * gain

    @pl.when(j < 2)
    def _():
        u_ref[...] = acc.astype(BF16)

    @pl.when((j == 2) | (j == 3))
    def _():
        q_ref[...] = (head_norm(acc, qg_ref[...]) * (HEAD_DIM ** -0.5)).astype(BF16)

    @pl.when(j == 4)
    def _():
        lane = lax.broadcasted_iota(jnp.int32, acc.shape, 1)
        kv_ref[...] = jnp.where(lane < KV_WIDTH, head_norm(acc, kg_ref[...]), acc).astype(BF16)

    @pl.when(j >= 5)
    def _():
        z = acc + b_ref[...]
        gate_ref[...] = (1.0 / (1.0 + jnp.exp(-z))).astype(BF16)


def _in_proj(x2, g1, w_bf, bias, qg_t, kg_t, ones_bd):
    n_col = IN_WIDTH // TN_IN
    grid = (N_TOK // TM_IN, n_col)
    return pl.pallas_call(
        _in_proj_kernel,
        grid=grid,
        in_specs=[
            pl.BlockSpec((TM_IN, D_MODEL), lambda i, j: (i, 0)),
            pl.BlockSpec((1, D_MODEL), lambda i, j: (0, 0)),
            pl.BlockSpec((D_MODEL, TN_IN), lambda i, j: (0, j)),
            pl.BlockSpec((1, TN_IN), lambda i, j: (0, jnp.maximum(j - 5, 0))),
            pl.BlockSpec((1, TN_IN), lambda i, j: (0, 0)),
            pl.BlockSpec((1, TN_IN), lambda i, j: (0, 0)),
            pl.BlockSpec((TN_IN, TN_IN), lambda i, j: (0, 0)),
        ],
        out_specs=[
            pl.BlockSpec((TM_IN, TN_IN), lambda i, j: (i, jnp.minimum(j, 1))),
            pl.BlockSpec((TM_IN, TN_IN), lambda i, j: (i, jnp.clip(j - 2, 0, 1))),
            pl.BlockSpec((TM_IN, TN_IN), lambda i, j: (i, 0)),
            pl.BlockSpec((TM_IN, TN_IN), lambda i, j: (i, jnp.maximum(j - 5, 0))),
        ],
        out_shape=[
            jax.ShapeDtypeStruct((N_TOK, FOURIER_WIDTH), BF16),
            jax.ShapeDtypeStruct((N_TOK, Q_WIDTH), BF16),
            jax.ShapeDtypeStruct((N_TOK, 2 * KV_WIDTH), BF16),
            jax.ShapeDtypeStruct((N_TOK, 2 * D_MODEL), BF16),
        ],
        scratch_shapes=[pltpu.VMEM((TM_IN, D_MODEL), BF16)],
        compiler_params=_cparams(("arbitrary", "arbitrary"), 56),
        name="in_proj",
    )(x2, g1, w_bf, bias, qg_t, kg_t, ones_bd)


def _fourier_kernel(u_ref, f1r_ref, f1i_ref, gr_ref, gi_ref, cc_ref, sc_ref, y_ref, cs_scr, ss_scr):
    b = pl.program_id(1)

    @pl.when(b == 0)
    def _():
        gr = gr_ref[...]
        gi = gi_ref[...]
        for a in range(TT_F // DFT_RADIX):
            f1r = f1r_ref[a:a + 1, :]
            f1i = f1i_ref[a:a + 1, :]
            rows = slice(a * DFT_RADIX, (a + 1) * DFT_RADIX)
            cs_scr[rows, :] = (f1r * gr - f1i * gi).astype(BF16)
            ss_scr[rows, :] = (f1r * gi + f1i * gr).astype(BF16)

    u = u_ref[...]
    a_seq = _dot(cs_scr[...], u)
    b_seq = _dot(ss_scr[...], u)
    for g in range(N_FOURIER_GROUPS):
        cols = slice(g * FOURIER_GROUP_DIM, (g + 1) * FOURIER_GROUP_DIM)
        y = _dot(a_seq[:, cols].astype(BF16), cc_ref[...]) - _dot(b_seq[:, cols].astype(BF16), sc_ref[...])
        y_ref[:, cols] = y.astype(BF16)


def _fourier(u, f1r, f1i, gr, gi, cc, sc):
    nt = SEQ // TT_F
    f1_rows = TT_F // DFT_RADIX
    return pl.pallas_call(
        _fourier_kernel,
        grid=(nt, BATCH),
        in_specs=[
            pl.BlockSpec((SEQ, FOURIER_WIDTH), lambda t, b: (b, 0)),
            pl.BlockSpec((f1_rows, SEQ), lambda t, b: (t, 0)),
            pl.BlockSpec((f1_rows, SEQ), lambda t, b: (t, 0)),
            pl.BlockSpec((DFT_RADIX, SEQ), lambda t, b: (0, 0)),
            pl.BlockSpec((DFT_RADIX, SEQ), lambda t, b: (0, 0)),
            pl.BlockSpec((FOURIER_GROUP_DIM, FOURIER_GROUP_DIM), lambda t, b: (0, 0)),
            pl.BlockSpec((FOURIER_GROUP_DIM, FOURIER_GROUP_DIM), lambda t, b: (0, 0)),
        ],
        out_specs=pl.BlockSpec((TT_F, FOURIER_WIDTH), lambda t, b: (b * nt + t, 0)),
        out_shape=jax.ShapeDtypeStruct((N_TOK, FOURIER_WIDTH), BF16),
        scratch_shapes=[pltpu.VMEM((TT_F, SEQ), BF16), pltpu.VMEM((TT_F, SEQ), BF16)],
        compiler_params=_cparams(("arbitrary", "arbitrary"), 56),
        name="fourier",
    )(u, f1r, f1i, gr, gi, cc, sc)


def _alibi_slope(h):
    return float(2.0 ** (-8.0 * (h + 1) / N_Q_HEADS))


def _attention_kernel(sink_ref, q_ref, prev_ref, cur_ref, next_ref, sel_ref, o_ref, kz_scr, vz_scr):
    t = pl.program_id(1)
    band = jnp.concatenate([prev_ref[...], cur_ref[...], next_ref[...]], axis=0)
    for g in range(N_KV_HEADS):
        c, half = divmod(g, 2)
        kcol = band[:, c * LANES:(c + 1) * LANES]
        vcol = band[:, KV_WIDTH + c * LANES:KV_WIDTH + (c + 1) * LANES]
        for d in range(2):
            sel = sel_ref[half * 2 + d]
            kz_scr[2 * g + d] = _dot(kcol, sel).astype(BF16)
            vz_scr[2 * g + d] = _dot(vcol, sel).astype(BF16)

    def sub_block(i, carry):
        q_rows = pl.ds(pl.multiple_of(i * QB, QB), QB)
        k_rows = pl.ds(pl.multiple_of(i * QB, QB), 3 * QB)
        row = lax.broadcasted_iota(jnp.int32, (QB, 3 * QB), 0)
        col = lax.broadcasted_iota(jnp.int32, (QB, 3 * QB), 1)
        rel = row - col + QB
        absrel = jnp.abs(rel)
        kpos = t * TQ + i * QB - QB + col
        mask = (absrel <= WINDOW) & (kpos >= 0) & (kpos < SEQ)
        absrel_f = absrel.astype(F32)
        for g in range(N_KV_HEADS):
            for p in range(2):
                cols = slice((2 * g + p) * LANES, (2 * g + p + 1) * LANES)
                qc = q_ref[q_rows, cols]
                out = jnp.zeros((QB, LANES), F32)
                for d in range(2):
                    h = 4 * g + 2 * p + d
                    kz = kz_scr[2 * g + d, k_rows, :]
                    s = lax.dot_general(qc, kz, (((1,), (1,)), ((), ())), preferred_element_type=F32)
                    s = jnp.where(mask, s - _alibi_slope(h) * absrel_f, NEG_INF)
                    sink = sink_ref[h]
                    m = jnp.maximum(jnp.max(s, axis=-1, keepdims=True), sink)
                    pr = jnp.exp(s - m)
                    den = jnp.sum(pr, axis=-1, keepdims=True) + jnp.exp(sink - m)
                    out = out + _dot(pr.astype(BF16), vz_scr[2 * g + d, k_rows, :]) / den
                o_ref[q_rows, cols] = out.astype(BF16)
        return carry

    lax.fori_loop(0, TQ // QB, sub_block, 0)


def _attention(sink, q, kv, sel):
    nt = SEQ // TQ
    nb = SEQ // QB
    r = TQ // QB
    grid_spec = pltpu.PrefetchScalarGridSpec(
        num_scalar_prefetch=1,
        grid=(BATCH, nt),
        in_specs=[
            pl.BlockSpec((TQ, Q_WIDTH), lambda b, t, s: (b * nt + t, 0)),
            pl.BlockSpec((QB, 2 * KV_WIDTH), lambda b, t, s: (b * nb + jnp.maximum(t * r - 1, 0), 0)),
            pl.BlockSpec((TQ, 2 * KV_WIDTH), lambda b, t, s: (b * nt + t, 0)),
            pl.BlockSpec((QB, 2 * KV_WIDTH), lambda b, t, s: (b * nb + jnp.minimum(t * r + r, nb - 1), 0)),
            pl.BlockSpec((4, LANES, LANES), lambda b, t, s: (0, 0, 0)),
        ],
        out_specs=pl.BlockSpec((TQ, Q_WIDTH), lambda b, t, s: (b * nt + t, 0)),
        scratch_shapes=[pltpu.VMEM((2 * N_KV_HEADS, TQ + 2 * QB, LANES), BF16),
                        pltpu.VMEM((2 * N_KV_HEADS, TQ + 2 * QB, LANES), BF16)],
    )
    return pl.pallas_call(
        _attention_kernel,
        grid_spec=grid_spec,
        out_shape=jax.ShapeDtypeStruct((N_TOK, Q_WIDTH), BF16),
        compiler_params=_cparams(("arbitrary", "arbitrary"), 32),
        name="attention",
    )(sink, q, kv, kv, kv, sel)


def _merge_kernel(yf_ref, at_ref, gf_ref, ga_ref, x_ref, wfo_ref, wao_ref, wo_ref, g2_ref, wr_ref, br_ref,
                  ltri_ref, x1_ref, h2p_ref, meta_ref, cnt_ref, base_scr):
    i = pl.program_id(0)

    @pl.when(i == 0)
    def _():
        base_scr[...] = jnp.zeros_like(base_scr)

    yf = _dot(yf_ref[...], wfo_ref[...])
    ya = _dot(at_ref[...], wao_ref[...])
    merged = gf_ref[...].astype(F32) * yf + ga_ref[...].astype(F32) * ya
    x1 = x_ref[...] + _dot(merged.astype(BF16), wo_ref[...])
    x1_ref[...] = x1
    ms = jnp.mean(x1 * x1, axis=-1, keepdims=True)
    h2 = x1 * lax.rsqrt(ms + RMS_EPS) * g2_ref[...]
    h2p_ref[...] = _pack_halves(h2[:, :HALF], h2[:, HALF:])
    logits = _dot(h2.astype(BF16), wr_ref[...]) + br_ref[...]

    lane = lax.broadcasted_iota(jnp.int32, logits.shape, 1)
    lane_f = lane.astype(F32)
    vals = logits
    top_v, top_i, onehots = [], [], []
    for _ in range(TOP_K):
        m = jnp.max(vals, axis=-1, keepdims=True)
        idx = jnp.min(jnp.where(vals == m, lane_f, float(LANES)), axis=-1, keepdims=True)
        oh = lane_f == idx
        top_v.append(m)
        top_i.append(idx)
        onehots.append(oh)
        vals = jnp.where(oh, -jnp.inf, vals)
    exps = [jnp.exp(v - top_v[0]) for v in top_v]
    den = exps[0] + exps[1] + exps[2] + exps[3]
    gates = [e / den for e in exps]

    cnt = sum(oh.astype(F32) for oh in onehots)
    prefix = _dot(ltri_ref[...], cnt.astype(BF16))
    tot = base_scr[...] + prefix
    ranks = [jnp.sum(jnp.where(oh, tot, 0.0), axis=-1, keepdims=True) for oh in onehots]
    base_scr[...] = base_scr[...] + jnp.sum(cnt, axis=0, keepdims=True)
    cnt_ref[...] = jnp.broadcast_to(base_scr[...], cnt_ref.shape)

    meta = jnp.zeros(logits.shape, F32)
    for k in range(TOP_K):
        meta = jnp.where(lane == k, top_i[k], meta)
        meta = jnp.where(lane == TOP_K + k, ranks[k], meta)
        meta = jnp.where(lane == 2 * TOP_K + k, gates[k], meta)
    meta_ref[...] = meta


def _merge(yf, attn, gates, x2, wfo, wao, wo, g2, wr, br, ltri):
    tm = TM_MG
    const = lambda i: (0, 0)
    resident = functools.partial(pl.BlockSpec, index_map=const, pipeline_mode=pl.Buffered(1))
    return pl.pallas_call(
        _merge_kernel,
        grid=(N_TOK // tm,),
        in_specs=[
            pl.BlockSpec((tm, FOURIER_WIDTH), lambda i: (i, 0)),
            pl.BlockSpec((tm, Q_WIDTH), lambda i: (i, 0)),
            pl.BlockSpec((tm, D_MODEL), lambda i: (i, 0)),
            pl.BlockSpec((tm, D_MODEL), lambda i: (i, 1)),
            pl.BlockSpec((tm, D_MODEL), lambda i: (i, 0)),
            resident((FOURIER_WIDTH, D_MODEL)),
            resident((Q_WIDTH, D_MODEL)),
            resident((D_MODEL, D_MODEL)),
            pl.BlockSpec((1, D_MODEL), const),
            pl.BlockSpec((D_MODEL, LANES), const),
            pl.BlockSpec((1, LANES), const),
            pl.BlockSpec((tm, tm), const),
        ],
        out_specs=[
            pl.BlockSpec((tm, D_MODEL), lambda i: (i, 0)),
            pl.BlockSpec((tm, HALF), lambda i: (i, 0)),
            pl.BlockSpec((tm, LANES), lambda i: (i, 0)),
            pl.BlockSpec((8, LANES), const),
        ],
        out_shape=[
            jax.ShapeDtypeStruct((N_TOK, D_MODEL), F32),
            jax.ShapeDtypeStruct((N_TOK, HALF), U32),
            jax.ShapeDtypeStruct((N_TOK, LANES), F32),
            jax.ShapeDtypeStruct((8, LANES), F32),
        ],
        scratch_shapes=[pltpu.VMEM((1, LANES), F32)],
        compiler_params=_cparams(("arbitrary",), 56),
        name="merge",
    )(yf, attn, gates, gates, x2, wfo, wao, wo, g2, wr, br, ltri)


def _dispatch_kernel(ra_ref, h2p_hbm, xrows_ref, sem):
    def row_copy(tok, r):
        return pltpu.make_async_copy(h2p_hbm.at[pl.ds(tok, 1)], xrows_ref.at[pl.ds(r, 1)], sem)

    live = ra_ref[0, 0, 0] >= 0

    @pl.when(jnp.logical_not(live))
    def _():
        xrows_ref[...] = jnp.zeros_like(xrows_ref)

    @pl.when(live)
    def _():
        def start(r, carry):
            tok = jnp.maximum(ra_ref[0, 0, r], 0) // TOP_K
            row_copy(tok, r).start()
            return carry
        lax.fori_loop(0, RT, start, 0, unroll=8)

        def wait(r, carry):
            row_copy(0, r).wait()
            return carry
        lax.fori_loop(0, RT, wait, 0, unroll=8)


def _dispatch(row_assign3, h2p):
    return pl.pallas_call(
        _dispatch_kernel,
        grid=(ROWS // RT,),
        in_specs=[
            pl.BlockSpec((1, 1, RT), lambda i: (i, 0, 0), memory_space=pltpu.SMEM),
            pl.BlockSpec(memory_space=pl.ANY),
        ],
        out_specs=pl.BlockSpec((RT, HALF), lambda i: (i, 0)),
        out_shape=jax.ShapeDtypeStruct((ROWS, HALF), U32),
        scratch_shapes=[pltpu.SemaphoreType.DMA(())],
        compiler_params=_cparams(("arbitrary",), 16),
        name="dispatch",
    )(row_assign3, h2p)


def _expert_kernel(e_ref, row0_ref, nsub_ref, live_ref,
                   xrows_hbm, wg_ref, bg_ref, wu_ref, bu_ref, wd_ref, bd_ref,
                   yrows_hbm, xu_scr, acc_scr, wgb, wub, wdb, sem):
    del e_ref, live_ref
    w = pl.program_id(0)
    f = pl.program_id(1)
    ns = nsub_ref[w]
    r0 = row0_ref[w]

    def sub_rows(s):
        return pl.ds(pl.multiple_of(s * RT, RT), RT)

    def hbm_rows(s):
        return pl.ds(pl.multiple_of(r0 + s * RT, RT), RT)

    def load_copy(s):
        return pltpu.make_async_copy(xrows_hbm.at[hbm_rows(s)], xu_scr.at[sub_rows(s)], sem.at[0])

    def store_copy(s):
        return pltpu.make_async_copy(xu_scr.at[sub_rows(s)], yrows_hbm.at[hbm_rows(s)], sem.at[1])

    def for_subtiles(fn):
        def body(s, carry):
            fn(s)
            return carry
        lax.fori_loop(0, ns, body, 0)

    def contribution(s):
        lo, hi = _unpack_halves(xu_scr[sub_rows(s), :])
        x_lo = lo.astype(BF16)
        x_hi = hi.astype(BF16)
        g = _dot(x_lo, wgb[:HALF, :]) + _dot(x_hi, wgb[HALF:, :]) + bg_ref[...]
        u = _dot(x_lo, wub[:HALF, :]) + _dot(x_hi, wub[HALF:, :]) + bu_ref[...]
        g = jnp.minimum(g, SWIGLU_LIMIT)
        u = jnp.clip(u, -SWIGLU_LIMIT, SWIGLU_LIMIT)
        act = (u + 1.0) * (g * (1.0 / (1.0 + jnp.exp(-SWIGLU_ALPHA * g))))
        return _dot(act.astype(BF16), wdb[...])

    @pl.when(ns > 0)
    def _():
        @pl.when(f == 0)
        def _():
            for_subtiles(lambda s: load_copy(s).start())
            for_subtiles(lambda s: load_copy(s).wait())

        wgb[...] = wg_ref[...].astype(BF16)
        wub[...] = wu_ref[...].astype(BF16)
        wdb[...] = wd_ref[...].astype(BF16)

        @pl.when(f == 0)
        def _():
            def first(s):
                acc_scr[sub_rows(s), :] = contribution(s) + bd_ref[...]
            for_subtiles(first)

        @pl.when((f > 0) & (f < NF - 1))
        def _():
            def middle(s):
                acc_scr[sub_rows(s), :] += contribution(s)
            for_subtiles(middle)

        @pl.when(f == NF - 1)
        def _():
            def last(s):
                y = acc_scr[sub_rows(s), :] + contribution(s)
                xu_scr[sub_rows(s), :] = _pack_halves(y[:, :HALF], y[:, HALF:])
            for_subtiles(last)
            for_subtiles(lambda s: store_copy(s).start())
            for_subtiles(lambda s: store_copy(s).wait())


def _experts(item_e, item_row0, item_nsub, item_live, x_rows, wg, bg, wu, bu, wd, bd):
    def w_in_map(w, f, e, r0, ns, lv):
        return (e[w], 0, jnp.where(lv[w] > 0, f, NF - 1))

    def w_down_map(w, f, e, r0, ns, lv):
        return (e[w], jnp.where(lv[w] > 0, f, NF - 1), 0)

    def b_down_map(w, f, e, r0, ns, lv):
        return (e[w], 0, 0)

    grid_spec = pltpu.PrefetchScalarGridSpec(
        num_scalar_prefetch=4,
        grid=(N_ITEMS, NF),
        in_specs=[
            pl.BlockSpec(memory_space=pl.ANY),
            pl.BlockSpec((None, D_MODEL, FC), w_in_map),
            pl.BlockSpec((None, 1, FC), w_in_map),
            pl.BlockSpec((None, D_MODEL, FC), w_in_map),
            pl.BlockSpec((None, 1, FC), w_in_map),
            pl.BlockSpec((None, FC, D_MODEL), w_down_map),
            pl.BlockSpec((None, 1, D_MODEL), b_down_map),
        ],
        out_specs=pl.BlockSpec(memory_space=pl.ANY),
        scratch_shapes=[
            pltpu.VMEM((R_MAX, HALF), U32),
            pltpu.VMEM((R_MAX, D_MODEL), F32),
            pltpu.VMEM((D_MODEL, FC), BF16),
            pltpu.VMEM((D_MODEL, FC), BF16),
            pltpu.VMEM((FC, D_MODEL), BF16),
            pltpu.SemaphoreType.DMA((2,)),
        ],
    )
    return pl.pallas_call(
        _expert_kernel,
        grid_spec=grid_spec,
        out_shape=jax.ShapeDtypeStruct((ROWS, HALF), U32),
        input_output_aliases={4: 0},
        compiler_params=_cparams(("arbitrary", "arbitrary"), 58),
        name="experts",
    )(item_e, item_row0, item_nsub, item_live, x_rows, wg, bg, wu, bu, wd, bd)


def _combine_kernel(dest_ref, meta_ref, x1_ref, yrows_hbm, out_ref, ybuf, sem):
    def row_copy(d, k, t):
        return pltpu.make_async_copy(yrows_hbm.at[pl.ds(d, 1)], ybuf.at[k, pl.ds(t, 1)], sem)

    def start(t, carry):
        for k in range(TOP_K):
            row_copy(dest_ref[0, 0, TOP_K * t + k], k, t).start()
        return carry
    lax.fori_loop(0, TM_CB, start, 0, unroll=2)

    def wait(t, carry):
        for k in range(TOP_K):
            row_copy(0, k, t).wait()
        return carry
    lax.fori_loop(0, TM_CB, wait, 0)

    lo_acc = x1_ref[:, :HALF]
    hi_acc = x1_ref[:, HALF:]
    for k in range(TOP_K):
        gate = meta_ref[:, 2 * TOP_K + k:2 * TOP_K + k + 1]
        lo, hi = _unpack_halves(ybuf[k])
        lo_acc = lo_acc + gate * lo
        hi_acc = hi_acc + gate * hi
    out_ref[:, :HALF] = lo_acc
    out_ref[:, HALF:] = hi_acc


def _combine(dest3, meta, x1, y_rows):
    tm = TM_CB
    return pl.pallas_call(
        _combine_kernel,
        grid=(N_TOK // tm,),
        in_specs=[
            pl.BlockSpec((1, 1, TOP_K * tm), lambda i: (i, 0, 0), memory_space=pltpu.SMEM),
            pl.BlockSpec((tm, LANES), lambda i: (i, 0)),
            pl.BlockSpec((tm, D_MODEL), lambda i: (i, 0)),
            pl.BlockSpec(memory_space=pl.ANY),
        ],
        out_specs=pl.BlockSpec((tm, D_MODEL), lambda i: (i, 0)),
        out_shape=jax.ShapeDtypeStruct((N_TOK, D_MODEL), F32),
        scratch_shapes=[pltpu.VMEM((TOP_K, tm, HALF), U32), pltpu.SemaphoreType.DMA(())],
        compiler_params=_cparams(("arbitrary",), 32),
        name="combine",
    )(dest3, meta, x1, y_rows)


def _dft_tables():
    r = DFT_RADIX
    k = np.arange(r)
    e64 = np.exp(2j * np.pi * np.outer(k, k) / r)
    e4096 = np.exp(2j * np.pi * np.outer(k, k) / SEQ)
    s = np.arange(SEQ)
    f1 = e64[:, s % r]
    g = e64[:, s // r] * e4096[:, s % r] / math.sqrt(SEQ)
    c = np.arange(FOURIER_GROUP_DIM)
    ang = 2.0 * np.pi * np.outer(c, c) / FOURIER_GROUP_DIM
    scale = 1.0 / math.sqrt(FOURIER_GROUP_DIM)
    as32 = lambda a: jnp.asarray(a.astype(np.float32))
    return (as32(f1.real), as32(f1.imag), as32(g.real), as32(g.imag),
            jnp.asarray((np.cos(ang) * scale).astype(np.float32)).astype(BF16),
            jnp.asarray((np.sin(ang) * scale).astype(np.float32)).astype(BF16))


def _head_selectors():
    i = np.arange(LANES)[:, None]
    j = np.arange(LANES)[None, :]
    out = np.zeros((4, LANES, LANES), np.float32)
    for src in range(2):
        for dst in range(2):
            out[2 * src + dst] = (i - HEAD_DIM * src == j - HEAD_DIM * dst) & (j // HEAD_DIM == dst)
    return jnp.asarray(out).astype(BF16)


def kernel(x, norm1_g, w_in, b_branch_gate, q_norm_g, k_norm_g, attn_sink, w_fourier_out, w_attn_out, w_o,
           norm2_g, w_router, b_router, w_gate_e, b_gate_e, w_up_e, b_up_e, w_down_e, b_down_e):
    b, s, d = x.shape
    assert (b, s, d) == (BATCH, SEQ, D_MODEL) and norm1_g.shape[0] == 1
    x2 = x.reshape(N_TOK, D_MODEL)

    heads_per_tile = TN_IN // HEAD_DIM
    ones_bd = jnp.asarray((np.arange(TN_IN)[:, None] // HEAD_DIM == np.arange(TN_IN)[None, :] // HEAD_DIM)
                          .astype(np.float32)).astype(BF16)
    u_f, q, kv, gates = _in_proj(
        x2, norm1_g[0].reshape(1, D_MODEL), w_in[0].astype(BF16), b_branch_gate[0].reshape(1, 2 * D_MODEL),
        jnp.tile(q_norm_g[0], heads_per_tile).reshape(1, TN_IN),
        jnp.tile(k_norm_g[0], heads_per_tile).reshape(1, TN_IN), ones_bd)
    y_f = _fourier(u_f, *_dft_tables())
    attn = _attention(attn_sink[0], q, kv, _head_selectors())

    wr = jnp.zeros((D_MODEL, LANES), BF16).at[:, :N_EXPERTS].set(w_router[0].astype(BF16))
    br = jnp.full((1, LANES), NEG_INF, F32).at[0, :N_EXPERTS].set(b_router[0])
    ltri = jnp.asarray(np.tril(np.ones((TM_MG, TM_MG), np.float32), -1)).astype(BF16)
    x1, h2p, meta, cnt = _merge(y_f, attn, gates, x2, w_fourier_out[0].astype(BF16), w_attn_out[0].astype(BF16),
                                w_o[0].astype(BF16), norm2_g[0].reshape(1, D_MODEL), wr, br, ltri)

    counts = cnt[0, :N_EXPERTS].astype(jnp.int32)
    top_idx = meta[:, 0:TOP_K].astype(jnp.int32)
    rank = meta[:, TOP_K:2 * TOP_K].astype(jnp.int32)
    padded = ((counts + RT - 1) // RT) * RT
    pend = jnp.cumsum(padded)
    pstart = pend - padded
    dest = pstart[top_idx] + rank
    n_assign = TOP_K * N_TOK
    row_assign = jnp.full((ROWS,), -1, jnp.int32).at[dest.reshape(n_assign)].set(
        jnp.arange(n_assign, dtype=jnp.int32))

    items_per_e = (padded + R_MAX - 1) // R_MAX
    it_end = jnp.cumsum(items_per_e)
    it_start = it_end - items_per_e
    total_items = it_end[-1]
    wi = jnp.arange(N_ITEMS, dtype=jnp.int32)
    e_w = jnp.minimum(jnp.searchsorted(it_end, wi, side="right"), N_EXPERTS - 1).astype(jnp.int32)
    j_w = wi - it_start[e_w]
    live = wi < total_items
    rows_w = jnp.clip(padded[e_w] - j_w * R_MAX, 0, R_MAX)
    e_last = e_w[jnp.maximum(total_items - 1, 0)]
    item_e = jnp.where(live, e_w, e_last).astype(jnp.int32)
    item_row0 = jnp.where(live, pstart[e_w] + j_w * R_MAX, 0).astype(jnp.int32)
    item_nsub = jnp.where(live, rows_w // RT, 0).astype(jnp.int32)
    item_live = live.astype(jnp.int32)

    x_rows = _dispatch(row_assign.reshape(ROWS // RT, 1, RT), h2p)
    y_rows = _experts(item_e, item_row0, item_nsub, item_live, x_rows,
                      w_gate_e[0], b_gate_e[0].reshape(N_EXPERTS, 1, D_EXPERT),
                      w_up_e[0], b_up_e[0].reshape(N_EXPERTS, 1, D_EXPERT),
                      w_down_e[0], b_down_e[0].reshape(N_EXPERTS, 1, D_MODEL))
    out = _combine(dest.reshape(N_TOK // TM_CB, 1, TOP_K * TM_CB), meta, x1, y_rows)
    return out.reshape(BATCH, SEQ, D_MODEL)
```

```python
import functools
import math

import jax
import jax.numpy as jnp
import numpy as np
from jax import lax
from jax.experimental import pallas as pl
from jax.experimental.pallas import tpu as pltpu

D_MODEL = 2048
BATCH = 4
SEQ = 4096
N_TOK = BATCH * SEQ
N_Q_HEADS = 16
N_KV_HEADS = 4
HEAD_DIM = 64
Q_WIDTH = N_Q_HEADS * HEAD_DIM
KV_WIDTH = N_KV_HEADS * HEAD_DIM
WINDOW = 128
N_FOURIER_GROUPS = 4
FOURIER_GROUP_DIM = 256
FOURIER_WIDTH = N_FOURIER_GROUPS * FOURIER_GROUP_DIM
IN_WIDTH = FOURIER_WIDTH + Q_WIDTH + 2 * KV_WIDTH + 2 * D_MODEL
N_EXPERTS = 32
TOP_K = 4
D_EXPERT = D_MODEL
SWIGLU_LIMIT = 7.0
SWIGLU_ALPHA = 1.702
RMS_EPS = 1e-6
NEG_INF = -1e30

F32 = jnp.float32
BF16 = jnp.bfloat16
U32 = jnp.uint32

V7X_VMEM_BYTES = 64 * 1024 * 1024
LANES = 128
SUBLANES = 8
MIB = 1024 * 1024

TM_IN = 1024
TN_IN = 512
NORM_ROWS = 64
TT_F = 512
DFT_RADIX = 64
TQ = 512
QB = 128
TM_MG = 256
RT = 256
R_MAX = 9 * RT
FC = 256
NF = D_EXPERT // FC
ROWS = TOP_K * N_TOK + N_EXPERTS * RT
N_ITEMS = (TOP_K * N_TOK + N_EXPERTS * (RT - 1) + N_EXPERTS * (R_MAX - RT)) // R_MAX + 1
TM_CB = 256
HALF = D_MODEL // 2
N_SUB = R_MAX // RT
GROUP = 3
OUT_CHUNK = 256
GATHER_PER_STEP = RT // NF
SLOT_ROWS = TOP_K * N_TOK + R_MAX


def _cparams(sem, vmem_mib):
    return pltpu.CompilerParams(dimension_semantics=sem, vmem_limit_bytes=vmem_mib * MIB)


def _dot(a, b):
    return jnp.dot(a, b, preferred_element_type=F32)


def _pack_halves(lo, hi):
    lo_bits = pltpu.bitcast(lo.astype(BF16).astype(F32), U32)
    hi_bits = pltpu.bitcast(hi.astype(BF16).astype(F32), U32)
    return (hi_bits & jnp.uint32(0xFFFF0000)) | (lo_bits >> jnp.uint32(16))


def _unpack_halves(words):
    lo = pltpu.bitcast(words << jnp.uint32(16), F32)
    hi = pltpu.bitcast(words & jnp.uint32(0xFFFF0000), F32)
    return lo, hi


def _store_row_tiles(tiles_ref, words):
    n_rows = words.shape[0]
    for j in range(SUBLANES):
        tiles_ref[pl.ds(j, n_rows, stride=SUBLANES), :] = words[:, j * LANES:(j + 1) * LANES]


def _load_row_tiles(tiles_ref, j, n_rows):
    return tiles_ref[pl.ds(j, n_rows, stride=SUBLANES), :]


def _in_proj_kernel(x_ref, g1_ref, w_ref, b_ref, qg_ref, kg_ref, ones_ref,
                    u_ref, q_ref, kv_ref, gate_ref, h_scr):
    j = pl.program_id(1)

    @pl.when(j == 0)
    def _():
        def body(c, carry):
            rows = pl.ds(pl.multiple_of(c * NORM_ROWS, NORM_ROWS), NORM_ROWS)
            x = x_ref[rows, :]
            ms = jnp.mean(x * x, axis=-1, keepdims=True)
            h_scr[rows, :] = (x * lax.rsqrt(ms + RMS_EPS) * g1_ref[...]).astype(BF16)
            return carry
        lax.fori_loop(0, TM_IN // NORM_ROWS, body, 0)

    acc = _dot(h_scr[...], w_ref[...])

    def head_norm(a, gain):
        sq = a * a
        hi = sq.astype(BF16)
        lo = (sq - hi.astype(F32)).astype(BF16)
        ssq = _dot(hi, ones_ref[...]) + _dot(lo, ones_ref[...])
        return a * lax.rsqrt(ssq * (1.0 / HEAD_DIM) + RMS_EPS) * gain

    @pl.when(j < 2)
    def _():
        u_ref[...] = acc.astype(BF16)

    @pl.when((j == 2) | (j == 3))
    def _():
        q_ref[...] = (head_norm(acc, qg_ref[...]) * (HEAD_DIM ** -0.5)).astype(BF16)

    @pl.when(j == 4)
    def _():
        lane = lax.broadcasted_iota(jnp.int32, acc.shape, 1)
        kv_ref[...] = jnp.where(lane < KV_WIDTH, head_norm(acc, kg_ref[...]), acc).astype(BF16)

    @pl.when(j >= 5)
    def _():
        z = acc + b_ref[...]
        gate_ref[...] = (1.0 / (1.0 + jnp.exp(-z))).astype(BF16)


def _in_proj(x2, g1, w_bf, bias, qg_t, kg_t, ones_bd):
    n_col = IN_WIDTH // TN_IN
    grid = (N_TOK // TM_IN, n_col)
    return pl.pallas_call(
        _in_proj_kernel,
        grid=grid,
        in_specs=[
            pl.BlockSpec((TM_IN, D_MODEL), lambda i, j: (i, 0)),
            pl.BlockSpec((1, D_MODEL), lambda i, j: (0, 0)),
            pl.BlockSpec((D_MODEL, TN_IN), lambda i, j: (0, j)),
            pl.BlockSpec((1, TN_IN), lambda i, j: (0, jnp.maximum(j - 5, 0))),
            pl.BlockSpec((1, TN_IN), lambda i, j: (0, 0)),
            pl.BlockSpec((1, TN_IN), lambda i, j: (0, 0)),
            pl.BlockSpec((TN_IN, TN_IN), lambda i, j: (0, 0)),
        ],
        out_specs=[
            pl.BlockSpec((TM_IN, TN_IN), lambda i, j: (i, jnp.minimum(j, 1))),
            pl.BlockSpec((TM_IN, TN_IN), lambda i, j: (i, jnp.clip(j - 2, 0, 1))),
            pl.BlockSpec((TM_IN, TN_IN), lambda i, j: (i, 0)),
            pl.BlockSpec((TM_IN, TN_IN), lambda i, j: (i, jnp.maximum(j - 5, 0))),
        ],
        out_shape=[
            jax.ShapeDtypeStruct((N_TOK, FOURIER_WIDTH), BF16),
            jax.ShapeDtypeStruct((N_TOK, Q_WIDTH), BF16),
            jax.ShapeDtypeStruct((N_TOK, 2 * KV_WIDTH), BF16),
            jax.ShapeDtypeStruct((N_TOK, 2 * D_MODEL), BF16),
        ],
        scratch_shapes=[pltpu.VMEM((TM_IN, D_MODEL), BF16)],
        compiler_params=_cparams(("arbitrary", "arbitrary"), 56),
        name="in_proj",
    )(x2, g1, w_bf, bias, qg_t, kg_t, ones_bd)


def _fourier_kernel(u_ref, f1r_ref, f1i_ref, gr_ref, gi_ref, cc_ref, sc_ref, y_ref, cs_scr, ss_scr):
    b = pl.program_id(1)

    @pl.when(b == 0)
    def _():
        gr = gr_ref[...]
        gi = gi_ref[...]
        for a in range(TT_F // DFT_RADIX):
            f1r = f1r_ref[a:a + 1, :]
            f1i = f1i_ref[a:a + 1, :]
            rows = slice(a * DFT_RADIX, (a + 1) * DFT_RADIX)
            cs_scr[rows, :] = (f1r * gr - f1i * gi).astype(BF16)
            ss_scr[rows, :] = (f1r * gi + f1i * gr).astype(BF16)

    u = u_ref[...]
    a_seq = _dot(cs_scr[...], u)
    b_seq = _dot(ss_scr[...], u)
    for g in range(N_FOURIER_GROUPS):
        cols = slice(g * FOURIER_GROUP_DIM, (g + 1) * FOURIER_GROUP_DIM)
        y = _dot(a_seq[:, cols].astype(BF16), cc_ref[...]) - _dot(b_seq[:, cols].astype(BF16), sc_ref[...])
        y_ref[:, cols] = y.astype(BF16)


def _fourier(u, f1r, f1i, gr, gi, cc, sc):
    nt = SEQ // TT_F
    f1_rows = TT_F // DFT_RADIX
    return pl.pallas_call(
        _fourier_kernel,
        grid=(nt, BATCH),
        in_specs=[
            pl.BlockSpec((SEQ, FOURIER_WIDTH), lambda t, b: (b, 0)),
            pl.BlockSpec((f1_rows, SEQ), lambda t, b: (t, 0)),
            pl.BlockSpec((f1_rows, SEQ), lambda t, b: (t, 0)),
            pl.BlockSpec((DFT_RADIX, SEQ), lambda t, b: (0, 0)),
            pl.BlockSpec((DFT_RADIX, SEQ), lambda t, b: (0, 0)),
            pl.BlockSpec((FOURIER_GROUP_DIM, FOURIER_GROUP_DIM), lambda t, b: (0, 0)),
            pl.BlockSpec((FOURIER_GROUP_DIM, FOURIER_GROUP_DIM), lambda t, b: (0, 0)),
        ],
        out_specs=pl.BlockSpec((TT_F, FOURIER_WIDTH), lambda t, b: (b * nt + t, 0)),
        out_shape=jax.ShapeDtypeStruct((N_TOK, FOURIER_WIDTH), BF16),
        scratch_shapes=[pltpu.VMEM((TT_F, SEQ), BF16), pltpu.VMEM((TT_F, SEQ), BF16)],
        compiler_params=_cparams(("arbitrary", "arbitrary"), 56),
        name="fourier",
    )(u, f1r, f1i, gr, gi, cc, sc)


def _alibi_slope(h):
    return float(2.0 ** (-8.0 * (h + 1) / N_Q_HEADS))


def _attention_kernel(sink_ref, q_ref, prev_ref, cur_ref, next_ref, sel_ref, o_ref, kz_scr, vz_scr):
    t = pl.program_id(1)
    band = jnp.concatenate([prev_ref[...], cur_ref[...], next_ref[...]], axis=0)
    for g in range(N_KV_HEADS):
        c, half = divmod(g, 2)
        kcol = band[:, c * LANES:(c + 1) * LANES]
        vcol = band[:, KV_WIDTH + c * LANES:KV_WIDTH + (c + 1) * LANES]
        for d in range(2):
            sel = sel_ref[half * 2 + d]
            kz_scr[2 * g + d] = _dot(kcol, sel).astype(BF16)
            vz_scr[2 * g + d] = _dot(vcol, sel).astype(BF16)

    def sub_block(i, carry):
        q_rows = pl.ds(pl.multiple_of(i * QB, QB), QB)
        k_rows = pl.ds(pl.multiple_of(i * QB, QB), 3 * QB)
        row = lax.broadcasted_iota(jnp.int32, (QB, 3 * QB), 0)
        col = lax.broadcasted_iota(jnp.int32, (QB, 3 * QB), 1)
        rel = row - col + QB
        absrel = jnp.abs(rel)
        kpos = t * TQ + i * QB - QB + col
        mask = (absrel <= WINDOW) & (kpos >= 0) & (kpos < SEQ)
        absrel_f = absrel.astype(F32)
        for g in range(N_KV_HEADS):
            for p in range(2):
                cols = slice((2 * g + p) * LANES, (2 * g + p + 1) * LANES)
                qc = q_ref[q_rows, cols]
                out = jnp.zeros((QB, LANES), F32)
                for d in range(2):
                    h = 4 * g + 2 * p + d
                    kz = kz_scr[2 * g + d, k_rows, :]
                    s = lax.dot_general(qc, kz, (((1,), (1,)), ((), ())), preferred_element_type=F32)
                    s = jnp.where(mask, s - _alibi_slope(h) * absrel_f, NEG_INF)
                    sink = sink_ref[h]
                    m = jnp.maximum(jnp.max(s, axis=-1, keepdims=True), sink)
                    pr = jnp.exp(s - m)
                    den = jnp.sum(pr, axis=-1, keepdims=True) + jnp.exp(sink - m)
                    out = out + _dot(pr.astype(BF16), vz_scr[2 * g + d, k_rows, :]) / den
                o_ref[q_rows, cols] = out.astype(BF16)
        return carry

    lax.fori_loop(0, TQ // QB, sub_block, 0)


def _attention(sink, q, kv, sel):
    nt = SEQ // TQ
    nb = SEQ // QB
    r = TQ // QB
    grid_spec = pltpu.PrefetchScalarGridSpec(
        num_scalar_prefetch=1,
        grid=(BATCH, nt),
        in_specs=[
            pl.BlockSpec((TQ, Q_WIDTH), lambda b, t, s: (b * nt + t, 0)),
            pl.BlockSpec((QB, 2 * KV_WIDTH), lambda b, t, s: (b * nb + jnp.maximum(t * r - 1, 0), 0)),
            pl.BlockSpec((TQ, 2 * KV_WIDTH), lambda b, t, s: (b * nt + t, 0)),
            pl.BlockSpec((QB, 2 * KV_WIDTH), lambda b, t, s: (b * nb + jnp.minimum(t * r + r, nb - 1), 0)),
            pl.BlockSpec((4, LANES, LANES), lambda b, t, s: (0, 0, 0)),
        ],
        out_specs=pl.BlockSpec((TQ, Q_WIDTH), lambda b, t, s: (b * nt + t, 0)),
        scratch_shapes=[pltpu.VMEM((2 * N_KV_HEADS, TQ + 2 * QB, LANES), BF16),
                        pltpu.VMEM((2 * N_KV_HEADS, TQ + 2 * QB, LANES), BF16)],
    )
    return pl.pallas_call(
        _attention_kernel,
        grid_spec=grid_spec,
        out_shape=jax.ShapeDtypeStruct((N_TOK, Q_WIDTH), BF16),
        compiler_params=_cparams(("arbitrary", "arbitrary"), 32),
        name="attention",
    )(sink, q, kv, kv, kv, sel)


def _merge_kernel(yf_ref, at_ref, gf_ref, ga_ref, x_ref, wfo_ref, wao_ref, wo_ref, g2_ref, wr_ref, br_ref,
                  ltri_ref, x1_ref, h2p_ref, meta_ref, cnt_ref, base_scr):
    i = pl.program_id(0)

    @pl.when(i == 0)
    def _():
        base_scr[...] = jnp.zeros_like(base_scr)

    yf = _dot(yf_ref[...], wfo_ref[...])
    ya = _dot(at_ref[...], wao_ref[...])
    merged = gf_ref[...].astype(F32) * yf + ga_ref[...].astype(F32) * ya
    x1 = x_ref[...] + _dot(merged.astype(BF16), wo_ref[...])
    x1_ref[...] = x1
    ms = jnp.mean(x1 * x1, axis=-1, keepdims=True)
    h2 = x1 * lax.rsqrt(ms + RMS_EPS) * g2_ref[...]
    _store_row_tiles(h2p_ref, _pack_halves(h2[:, :HALF], h2[:, HALF:]))
    logits = _dot(h2.astype(BF16), wr_ref[...]) + br_ref[...]

    lane = lax.broadcasted_iota(jnp.int32, logits.shape, 1)
    lane_f = lane.astype(F32)
    vals = logits
    top_v, top_i, onehots = [], [], []
    for _ in range(TOP_K):
        m = jnp.max(vals, axis=-1, keepdims=True)
        idx = jnp.min(jnp.where(vals == m, lane_f, float(LANES)), axis=-1, keepdims=True)
        oh = lane_f == idx
        top_v.append(m)
        top_i.append(idx)
        onehots.append(oh)
        vals = jnp.where(oh, -jnp.inf, vals)
    exps = [jnp.exp(v - top_v[0]) for v in top_v]
    den = exps[0] + exps[1] + exps[2] + exps[3]
    gates = [e / den for e in exps]

    cnt = sum(oh.astype(F32) for oh in onehots)
    prefix = _dot(ltri_ref[...], cnt.astype(BF16))
    tot = base_scr[...] + prefix
    ranks = [jnp.sum(jnp.where(oh, tot, 0.0), axis=-1, keepdims=True) for oh in onehots]
    base_scr[...] = base_scr[...] + jnp.sum(cnt, axis=0, keepdims=True)
    cnt_ref[...] = jnp.broadcast_to(base_scr[...], cnt_ref.shape)

    meta = jnp.zeros(logits.shape, F32)
    for k in range(TOP_K):
        meta = jnp.where(lane == k, top_i[k], meta)
        meta = jnp.where(lane == TOP_K + k, ranks[k], meta)
        meta = jnp.where(lane == 2 * TOP_K + k, gates[k], meta)
    meta_ref[...] = meta


def _merge(yf, attn, gates, x2, wfo, wao, wo, g2, wr, br, ltri):
    tm = TM_MG
    const = lambda i: (0, 0)
    resident = functools.partial(pl.BlockSpec, index_map=const, pipeline_mode=pl.Buffered(1))
    return pl.pallas_call(
        _merge_kernel,
        grid=(N_TOK // tm,),
        in_specs=[
            pl.BlockSpec((tm, FOURIER_WIDTH), lambda i: (i, 0)),
            pl.BlockSpec((tm, Q_WIDTH), lambda i: (i, 0)),
            pl.BlockSpec((tm, D_MODEL), lambda i: (i, 0)),
            pl.BlockSpec((tm, D_MODEL), lambda i: (i, 1)),
            pl.BlockSpec((tm, D_MODEL), lambda i: (i, 0)),
            resident((FOURIER_WIDTH, D_MODEL)),
            resident((Q_WIDTH, D_MODEL)),
            resident((D_MODEL, D_MODEL)),
            pl.BlockSpec((1, D_MODEL), const),
            pl.BlockSpec((D_MODEL, LANES), const),
            pl.BlockSpec((1, LANES), const),
            pl.BlockSpec((tm, tm), const),
        ],
        out_specs=[
            pl.BlockSpec((tm, D_MODEL), lambda i: (i, 0)),
            pl.BlockSpec((tm * SUBLANES, LANES), lambda i: (i, 0)),
            pl.BlockSpec((tm, LANES), lambda i: (i, 0)),
            pl.BlockSpec((8, LANES), const),
        ],
        out_shape=[
            jax.ShapeDtypeStruct((N_TOK, D_MODEL), F32),
            jax.ShapeDtypeStruct((N_TOK * SUBLANES, LANES), U32),
            jax.ShapeDtypeStruct((N_TOK, LANES), F32),
            jax.ShapeDtypeStruct((8, LANES), F32),
        ],
        scratch_shapes=[pltpu.VMEM((1, LANES), F32)],
        compiler_params=_cparams(("arbitrary",), 56),
        name="merge",
    )(yf, attn, gates, gates, x2, wfo, wao, wo, g2, wr, br, ltri)


def _expert_kernel(e_ref, live_ref, tok_cur_ref, tok_nxt_ref, dst_cur_ref,
                   h2p_hbm, wg_ref, bg_ref, wu_ref, bu_ref, wd_ref, bd_ref,
                   yslots_hbm, xu_scr, acc_scr, xs_scr, wgb, wub, wdb, gsem, ssem):
    del e_ref
    w = pl.program_id(0)
    f = pl.program_id(1)
    live = live_ref[w] > 0
    prev_live = (w > 0) & (live_ref[jnp.maximum(w - 1, 0)] > 0)
    cur = w & 1
    nxt = 1 - cur
    tile_rows = RT * SUBLANES

    def sub_rows(s):
        return pl.ds(pl.multiple_of(s * RT, RT), RT)

    def sub_tiles(slot, s):
        return xu_scr.at[slot, pl.ds(pl.multiple_of(s * tile_rows, tile_rows), tile_rows), :]

    def row_tile(row):
        return pl.ds(pl.multiple_of(row * SUBLANES, SUBLANES), SUBLANES)

    def gather_copy(tok_ref, slot, row):
        tok = tok_ref[0, 0, row]
        return pltpu.make_async_copy(h2p_hbm.at[row_tile(tok), :], xu_scr.at[slot, row_tile(row), :], gsem.at[slot])

    def scatter_copy(row):
        dst = dst_cur_ref[0, 0, row]
        return pltpu.make_async_copy(xu_scr.at[cur, row_tile(row), :], yslots_hbm.at[row_tile(dst), :], ssem)

    def for_all_rows(fn):
        def body(row, carry):
            fn(row)
            return carry
        lax.fori_loop(0, R_MAX, body, 0, unroll=8)

    def prefetch_next(s):
        base = s * RT + f * GATHER_PER_STEP
        for g in range(GATHER_PER_STEP):
            gather_copy(tok_nxt_ref, nxt, base + g).start()

    def cast_weights():
        wgb[...] = wg_ref[...].astype(BF16)
        wub[...] = wu_ref[...].astype(BF16)
        wdb[...] = wd_ref[...].astype(BF16)

    def stage_inputs(first_sub, n_sub, stage_row0):
        n_rows = n_sub * RT
        tiles = xu_scr.at[cur, pl.ds(pl.multiple_of(first_sub * tile_rows, tile_rows), n_sub * tile_rows), :]
        stage = xs_scr.at[pl.ds(stage_row0, n_rows), :]
        for j in range(SUBLANES):
            lo, hi = _unpack_halves(_load_row_tiles(tiles, j, n_rows))
            stage[:, j * LANES:(j + 1) * LANES] = lo.astype(BF16)
            stage[:, HALF + j * LANES:HALF + (j + 1) * LANES] = hi.astype(BF16)

    def activations(n_sub, stage_row0):
        x = xs_scr[pl.ds(stage_row0, n_sub * RT), :]
        g = _dot(x, wgb[...]) + bg_ref[...]
        u = _dot(x, wub[...]) + bu_ref[...]
        g = jnp.minimum(g, SWIGLU_LIMIT)
        u = jnp.clip(u, -SWIGLU_LIMIT, SWIGLU_LIMIT)
        act = (u + 1.0) * (g * (1.0 / (1.0 + jnp.exp(-SWIGLU_ALPHA * g))))
        return act.astype(BF16)

    def accumulate(j):
        n_rows = GROUP * RT
        rows = pl.ds(pl.multiple_of(j * n_rows, n_rows), n_rows)
        stage_inputs(j * GROUP, GROUP, 0)
        c = _dot(activations(GROUP, 0), wdb[...])
        acc_scr[rows, :] = jnp.where(f == 0, c, acc_scr[rows, :] + c)

    def finish(s):
        act = activations(1, (s % GROUP) * RT)
        rows = sub_rows(s)
        tiles = sub_tiles(cur, s)
        for c in range(HALF // OUT_CHUNK):
            lo_cols = slice(c * OUT_CHUNK, (c + 1) * OUT_CHUNK)
            hi_cols = slice(HALF + c * OUT_CHUNK, HALF + (c + 1) * OUT_CHUNK)
            y_lo = acc_scr[rows, lo_cols] + _dot(act, wdb[:, lo_cols]) + bd_ref[:, lo_cols]
            y_hi = acc_scr[rows, hi_cols] + _dot(act, wdb[:, hi_cols]) + bd_ref[:, hi_cols]
            words = _pack_halves(y_lo, y_hi)
            for jj in range(OUT_CHUNK // LANES):
                j = c * (OUT_CHUNK // LANES) + jj
                tiles[pl.ds(j, RT, stride=SUBLANES), :] = words[:, jj * LANES:(jj + 1) * LANES]

    @pl.when(f == 0)
    def _():
        @pl.when(w == 0)
        def _():
            def zero(s, carry):
                acc_scr[sub_rows(s), :] = jnp.zeros((RT, D_MODEL), F32)
                xu_scr[1, pl.ds(pl.multiple_of(s * tile_rows, tile_rows), tile_rows), :] = (
                    jnp.zeros((tile_rows, LANES), U32))
                return carry
            lax.fori_loop(0, N_SUB, zero, 0)
            pad_rows = pl.ds(TOP_K * N_TOK * SUBLANES, R_MAX * SUBLANES)
            fill = pltpu.make_async_copy(xu_scr.at[1], yslots_hbm.at[pad_rows, :], ssem)
            fill.start()
            fill.wait()
            for_all_rows(lambda row: gather_copy(tok_cur_ref, cur, row).start())

        @pl.when((w == 0) | prev_live)
        def _():
            first_tile = pl.ds(0, SUBLANES)
            row_in = pltpu.make_async_copy(h2p_hbm.at[first_tile, :], xu_scr.at[cur, first_tile, :], gsem.at[cur])
            for_all_rows(lambda row: row_in.wait())

        @pl.when(prev_live)
        def _():
            first_tile = pl.ds(0, SUBLANES)
            row_out = pltpu.make_async_copy(xu_scr.at[cur, first_tile, :], yslots_hbm.at[first_tile, :], ssem)
            for_all_rows(lambda row: row_out.wait())

    @pl.when(live & (f < NF - 1))
    def _():
        cast_weights()

        def group(j, carry):
            accumulate(j)
            for i in range(GROUP):
                prefetch_next(j * GROUP + i)
            return carry
        lax.fori_loop(0, N_SUB // GROUP, group, 0)

    @pl.when(live & (f == NF - 1))
    def _():
        cast_weights()
        stage_inputs(0, 1, 0)
        for s in range(N_SUB):
            finish(s)
            if s + 1 < N_SUB:
                stage_inputs(s + 1, 1, ((s + 1) % GROUP) * RT)
            for r in range(RT):
                scatter_copy(s * RT + r).start()
            prefetch_next(s)


def _experts(item_e, item_live, item_tok, item_dst, h2p, wg, bg, wu, bu, wd, bd):
    def w_in_map(w, f, e, lv):
        return (e[w], 0, jnp.where(lv[w] > 0, f, NF - 1))

    def w_down_map(w, f, e, lv):
        return (e[w], jnp.where(lv[w] > 0, f, NF - 1), 0)

    def b_down_map(w, f, e, lv):
        return (e[w], 0, 0)

    grid_spec = pltpu.PrefetchScalarGridSpec(
        num_scalar_prefetch=2,
        grid=(N_ITEMS, NF),
        in_specs=[
            pl.BlockSpec((1, 1, R_MAX), lambda w, f, e, lv: (w, 0, 0), memory_space=pltpu.SMEM),
            pl.BlockSpec((1, 1, R_MAX), lambda w, f, e, lv: (jnp.minimum(w + 1, N_ITEMS - 1), 0, 0),
                         memory_space=pltpu.SMEM),
            pl.BlockSpec((1, 1, R_MAX), lambda w, f, e, lv: (w, 0, 0), memory_space=pltpu.SMEM),
            pl.BlockSpec(memory_space=pl.ANY),
            pl.BlockSpec((None, D_MODEL, FC), w_in_map),
            pl.BlockSpec((None, 1, FC), w_in_map),
            pl.BlockSpec((None, D_MODEL, FC), w_in_map),
            pl.BlockSpec((None, 1, FC), w_in_map),
            pl.BlockSpec((None, FC, D_MODEL), w_down_map),
            pl.BlockSpec((None, 1, D_MODEL), b_down_map),
        ],
        out_specs=pl.BlockSpec(memory_space=pl.ANY),
        scratch_shapes=[
            pltpu.VMEM((2, R_MAX * SUBLANES, LANES), U32),
            pltpu.VMEM((R_MAX, D_MODEL), F32),
            pltpu.VMEM((GROUP * RT, D_MODEL), BF16),
            pltpu.VMEM((D_MODEL, FC), BF16),
            pltpu.VMEM((D_MODEL, FC), BF16),
            pltpu.VMEM((FC, D_MODEL), BF16),
            pltpu.SemaphoreType.DMA((2,)),
            pltpu.SemaphoreType.DMA(()),
        ],
    )
    return pl.pallas_call(
        _expert_kernel,
        grid_spec=grid_spec,
        out_shape=jax.ShapeDtypeStruct((SLOT_ROWS * SUBLANES, LANES), U32),
        compiler_params=_cparams(("arbitrary", "arbitrary"), 60),
        name="experts",
    )(item_e, item_live, item_tok, item_tok, item_dst, h2p, wg, bg, wu, bu, wd, bd)


def _combine_kernel(meta_ref, x1_ref, y0_ref, y1_ref, y2_ref, y3_ref, out_ref):
    gates = [meta_ref[:, 2 * TOP_K + k:2 * TOP_K + k + 1] for k in range(TOP_K)]
    for j in range(SUBLANES):
        lo_cols = slice(j * LANES, (j + 1) * LANES)
        hi_cols = slice(HALF + j * LANES, HALF + (j + 1) * LANES)
        lo_acc = x1_ref[:, lo_cols]
        hi_acc = x1_ref[:, hi_cols]
        for k, y_ref in enumerate((y0_ref, y1_ref, y2_ref, y3_ref)):
            lo, hi = _unpack_halves(_load_row_tiles(y_ref, j, TM_CB))
            lo_acc = lo_acc + gates[k] * lo
            hi_acc = hi_acc + gates[k] * hi
        out_ref[:, lo_cols] = lo_acc
        out_ref[:, hi_cols] = hi_acc


def _combine(meta, x1, y_slots):
    tm = TM_CB
    nblk = N_TOK // tm
    slot_specs = [pl.BlockSpec((tm * SUBLANES, LANES), lambda i, k=k: (k * nblk + i, 0)) for k in range(TOP_K)]
    return pl.pallas_call(
        _combine_kernel,
        grid=(nblk,),
        in_specs=[
            pl.BlockSpec((tm, LANES), lambda i: (i, 0)),
            pl.BlockSpec((tm, D_MODEL), lambda i: (i, 0)),
        ] + slot_specs,
        out_specs=pl.BlockSpec((tm, D_MODEL), lambda i: (i, 0)),
        out_shape=jax.ShapeDtypeStruct((N_TOK, D_MODEL), F32),
        compiler_params=_cparams(("arbitrary",), 32),
        name="combine",
    )(meta, x1, y_slots, y_slots, y_slots, y_slots)


def _dft_tables():
    r = DFT_RADIX
    k = np.arange(r)
    e64 = np.exp(2j * np.pi * np.outer(k, k) / r)
    e4096 = np.exp(2j * np.pi * np.outer(k, k) / SEQ)
    s = np.arange(SEQ)
    f1 = e64[:, s % r]
    g = e64[:, s // r] * e4096[:, s % r] / math.sqrt(SEQ)
    c = np.arange(FOURIER_GROUP_DIM)
    ang = 2.0 * np.pi * np.outer(c, c) / FOURIER_GROUP_DIM
    scale = 1.0 / math.sqrt(FOURIER_GROUP_DIM)
    as32 = lambda a: jnp.asarray(a.astype(np.float32))
    return (as32(f1.real), as32(f1.imag), as32(g.real), as32(g.imag),
            jnp.asarray((np.cos(ang) * scale).astype(np.float32)).astype(BF16),
            jnp.asarray((np.sin(ang) * scale).astype(np.float32)).astype(BF16))


def _head_selectors():
    i = np.arange(LANES)[:, None]
    j = np.arange(LANES)[None, :]
    out = np.zeros((4, LANES, LANES), np.float32)
    for src in range(2):
        for dst in range(2):
            out[2 * src + dst] = (i - HEAD_DIM * src == j - HEAD_DIM * dst) & (j // HEAD_DIM == dst)
    return jnp.asarray(out).astype(BF16)


def kernel(x, norm1_g, w_in, b_branch_gate, q_norm_g, k_norm_g, attn_sink, w_fourier_out, w_attn_out, w_o,
           norm2_g, w_router, b_router, w_gate_e, b_gate_e, w_up_e, b_up_e, w_down_e, b_down_e):
    b, s, d = x.shape
    assert (b, s, d) == (BATCH, SEQ, D_MODEL) and norm1_g.shape[0] == 1
    x2 = x.reshape(N_TOK, D_MODEL)

    heads_per_tile = TN_IN // HEAD_DIM
    ones_bd = jnp.asarray((np.arange(TN_IN)[:, None] // HEAD_DIM == np.arange(TN_IN)[None, :] // HEAD_DIM)
                          .astype(np.float32)).astype(BF16)
    u_f, q, kv, gates = _in_proj(
        x2, norm1_g[0].reshape(1, D_MODEL), w_in[0].astype(BF16), b_branch_gate[0].reshape(1, 2 * D_MODEL),
        jnp.tile(q_norm_g[0], heads_per_tile).reshape(1, TN_IN),
        jnp.tile(k_norm_g[0], heads_per_tile).reshape(1, TN_IN), ones_bd)
    y_f = _fourier(u_f, *_dft_tables())
    attn = _attention(attn_sink[0], q, kv, _head_selectors())

    wr = jnp.zeros((D_MODEL, LANES), BF16).at[:, :N_EXPERTS].set(w_router[0].astype(BF16))
    br = jnp.full((1, LANES), NEG_INF, F32).at[0, :N_EXPERTS].set(b_router[0])
    ltri = jnp.asarray(np.tril(np.ones((TM_MG, TM_MG), np.float32), -1)).astype(BF16)
    x1, h2p, meta, cnt = _merge(y_f, attn, gates, x2, w_fourier_out[0].astype(BF16), w_attn_out[0].astype(BF16),
                                w_o[0].astype(BF16), norm2_g[0].reshape(1, D_MODEL), wr, br, ltri)

    tables = _routing_tables(cnt[0, :N_EXPERTS].astype(jnp.int32), meta[:, 0:TOP_K].astype(jnp.int32),
                             meta[:, TOP_K:2 * TOP_K].astype(jnp.int32))
    y_slots = _experts(*tables, h2p,
                       w_gate_e[0], b_gate_e[0].reshape(N_EXPERTS, 1, D_EXPERT),
                       w_up_e[0], b_up_e[0].reshape(N_EXPERTS, 1, D_EXPERT),
                       w_down_e[0], b_down_e[0].reshape(N_EXPERTS, 1, D_MODEL))
    out = _combine(meta, x1, y_slots)
    return out.reshape(BATCH, SEQ, D_MODEL)


def _routing_tables(counts, top_idx, rank):
    padded = ((counts + RT - 1) // RT) * RT
    pend = jnp.cumsum(padded)
    pstart = pend - padded
    dest = pstart[top_idx] + rank
    n_assign = TOP_K * N_TOK
    row_assign = jnp.full((ROWS,), -1, jnp.int32).at[dest.reshape(n_assign)].set(
        jnp.arange(n_assign, dtype=jnp.int32))

    items_per_e = (padded + R_MAX - 1) // R_MAX
    it_end = jnp.cumsum(items_per_e)
    it_start = it_end - items_per_e
    total_items = it_end[-1]
    wi = jnp.arange(N_ITEMS, dtype=jnp.int32)
    e_w = jnp.minimum(jnp.searchsorted(it_end, wi, side="right"), N_EXPERTS - 1).astype(jnp.int32)
    j_w = wi - it_start[e_w]
    live = wi < total_items
    rows_w = jnp.clip(padded[e_w] - j_w * R_MAX, 0, R_MAX)
    e_last = e_w[jnp.maximum(total_items - 1, 0)]
    item_e = jnp.where(live, e_w, e_last).astype(jnp.int32)
    item_row0 = jnp.where(live, pstart[e_w] + j_w * R_MAX, 0).astype(jnp.int32)
    item_nsub = jnp.where(live, rows_w // RT, 0).astype(jnp.int32)
    item_live = live.astype(jnp.int32)
    r = jnp.arange(R_MAX, dtype=jnp.int32)
    src = jnp.minimum(item_row0[:, None] + r[None, :], ROWS - 1)
    item_rows = jnp.where(r[None, :] < (item_nsub * RT)[:, None], row_assign[src], -1)
    item_tok = (jnp.maximum(item_rows, 0) // TOP_K).reshape(N_ITEMS, 1, R_MAX)
    item_dst = jnp.where(item_rows >= 0, (item_rows % TOP_K) * N_TOK + item_rows // TOP_K,
                         TOP_K * N_TOK + r[None, :]).reshape(N_ITEMS, 1, R_MAX)
    return item_e, item_live, item_tok, item_dst
```

```python
import functools
import math

import jax
import jax.numpy as jnp
import numpy as np
from jax import lax
from jax.experimental import pallas as pl
from jax.experimental.pallas import tpu as pltpu

D_MODEL = 2048
BATCH = 4
SEQ = 4096
N_TOK = BATCH * SEQ
N_Q_HEADS = 16
N_KV_HEADS = 4
HEAD_DIM = 64
Q_WIDTH = N_Q_HEADS * HEAD_DIM
KV_WIDTH = N_KV_HEADS * HEAD_DIM
WINDOW = 128
N_FOURIER_GROUPS = 4
FOURIER_GROUP_DIM = 256
FOURIER_WIDTH = N_FOURIER_GROUPS * FOURIER_GROUP_DIM
IN_WIDTH = FOURIER_WIDTH + Q_WIDTH + 2 * KV_WIDTH + 2 * D_MODEL
N_EXPERTS = 32
TOP_K = 4
D_EXPERT = D_MODEL
SWIGLU_LIMIT = 7.0
SWIGLU_ALPHA = 1.702
RMS_EPS = 1e-6
NEG_INF = -1e30

F32 = jnp.float32
BF16 = jnp.bfloat16
U32 = jnp.uint32

V7X_VMEM_BYTES = 64 * 1024 * 1024
LANES = 128
SUBLANES = 8
MIB = 1024 * 1024

TM_IN = 1024
TN_IN = 512
NORM_ROWS = 64
IN_SLABS = 4
TT_F = 512
DFT_RADIX = 64
TQ = 512
QB = 128
TM_MG = 256
RT = 256
R_MAX = 9 * RT
FC = 256
NF = D_EXPERT // FC
ROWS = TOP_K * N_TOK + N_EXPERTS * RT
N_ITEMS = (TOP_K * N_TOK + N_EXPERTS * (RT - 1) + N_EXPERTS * (R_MAX - RT)) // R_MAX + 1
TM_CB = 256
HALF = D_MODEL // 2
N_SUB = R_MAX // RT
GROUP = 3
OUT_CHUNK = 256
GATHER_PER_STEP = RT // NF
SLOT_ROWS = TOP_K * N_TOK + R_MAX


def _cparams(sem, vmem_mib):
    return pltpu.CompilerParams(dimension_semantics=sem, vmem_limit_bytes=vmem_mib * MIB)


def _dot(a, b):
    return jnp.dot(a, b, preferred_element_type=F32)


def _pack_halves(lo, hi):
    lo_bits = pltpu.bitcast(lo.astype(BF16).astype(F32), U32)
    hi_bits = pltpu.bitcast(hi.astype(BF16).astype(F32), U32)
    return (hi_bits & jnp.uint32(0xFFFF0000)) | (lo_bits >> jnp.uint32(16))


def _unpack_halves(words):
    lo = pltpu.bitcast(words << jnp.uint32(16), F32)
    hi = pltpu.bitcast(words & jnp.uint32(0xFFFF0000), F32)
    return lo, hi


def _store_row_tiles(tiles_ref, words):
    n_rows = words.shape[0]
    for j in range(SUBLANES):
        tiles_ref[pl.ds(j, n_rows, stride=SUBLANES), :] = words[:, j * LANES:(j + 1) * LANES]


def _load_row_tiles(tiles_ref, j, n_rows):
    return tiles_ref[pl.ds(j, n_rows, stride=SUBLANES), :]


def _in_proj_kernel(x_ref, g1_ref, w_ref, b_ref, qg_ref, kg_ref, ones_ref,
                    u_ref, q_ref, kv_ref, gate_ref, h_scr):
    j = pl.program_id(1)

    @pl.when(j == 0)
    def _():
        def body(c, carry):
            rows = pl.ds(pl.multiple_of(c * NORM_ROWS, NORM_ROWS), NORM_ROWS)
            x = x_ref[rows, :]
            ms = jnp.mean(x * x, axis=-1, keepdims=True)
            h_scr[rows, :] = (x * lax.rsqrt(ms + RMS_EPS) * g1_ref[...]).astype(BF16)
            return carry
        lax.fori_loop(0, TM_IN // NORM_ROWS, body, 0)

    def head_norm(a, gain):
        sq = a * a
        hi = sq.astype(BF16)
        lo = (sq - hi.astype(F32)).astype(BF16)
        ssq = _dot(hi, ones_ref[...]) + _dot(lo, ones_ref[...])
        return a * lax.rsqrt(ssq * (1.0 / HEAD_DIM) + RMS_EPS) * gain

    def project(out_ref, epilogue):
        slab = TM_IN // IN_SLABS
        for s in range(IN_SLABS):
            rows = slice(s * slab, (s + 1) * slab)
            acc = _dot(h_scr[rows, :], w_ref[...])
            out_ref[rows, :] = epilogue(acc).astype(BF16)

    @pl.when(j < 2)
    def _():
        project(u_ref, lambda acc: acc)

    @pl.when((j == 2) | (j == 3))
    def _():
        project(q_ref, lambda acc: head_norm(acc, qg_ref[...]) * (HEAD_DIM ** -0.5))

    @pl.when(j == 4)
    def _():
        def k_norm_v_pass(acc):
            lane = lax.broadcasted_iota(jnp.int32, acc.shape, 1)
            return jnp.where(lane < KV_WIDTH, head_norm(acc, kg_ref[...]), acc)
        project(kv_ref, k_norm_v_pass)

    @pl.when(j >= 5)
    def _():
        project(gate_ref, lambda acc: 1.0 / (1.0 + jnp.exp(-(acc + b_ref[...]))))


def _in_proj(x2, g1, w_bf, bias, qg_t, kg_t, ones_bd):
    n_col = IN_WIDTH // TN_IN
    grid = (N_TOK // TM_IN, n_col)
    return pl.pallas_call(
        _in_proj_kernel,
        grid=grid,
        in_specs=[
            pl.BlockSpec((TM_IN, D_MODEL), lambda i, j: (i, 0)),
            pl.BlockSpec((1, D_MODEL), lambda i, j: (0, 0)),
            pl.BlockSpec((D_MODEL, TN_IN), lambda i, j: (0, j)),
            pl.BlockSpec((1, TN_IN), lambda i, j: (0, jnp.maximum(j - 5, 0))),
            pl.BlockSpec((1, TN_IN), lambda i, j: (0, 0)),
            pl.BlockSpec((1, TN_IN), lambda i, j: (0, 0)),
            pl.BlockSpec((TN_IN, TN_IN), lambda i, j: (0, 0)),
        ],
        out_specs=[
            pl.BlockSpec((TM_IN, TN_IN), lambda i, j: (i, jnp.minimum(j, 1))),
            pl.BlockSpec((TM_IN, TN_IN), lambda i, j: (i, jnp.clip(j - 2, 0, 1))),
            pl.BlockSpec((TM_IN, TN_IN), lambda i, j: (i, 0)),
            pl.BlockSpec((TM_IN, TN_IN), lambda i, j: (i, jnp.maximum(j - 5, 0))),
        ],
        out_shape=[
            jax.ShapeDtypeStruct((N_TOK, FOURIER_WIDTH), BF16),
            jax.ShapeDtypeStruct((N_TOK, Q_WIDTH), BF16),
            jax.ShapeDtypeStruct((N_TOK, 2 * KV_WIDTH), BF16),
            jax.ShapeDtypeStruct((N_TOK, 2 * D_MODEL), BF16),
        ],
        scratch_shapes=[pltpu.VMEM((TM_IN, D_MODEL), BF16)],
        compiler_params=_cparams(("arbitrary", "arbitrary"), 56),
        name="in_proj",
    )(x2, g1, w_bf, bias, qg_t, kg_t, ones_bd)


def _fourier_kernel(u_ref, f1r_ref, f1i_ref, gr_ref, gi_ref, cc_ref, sc_ref, y_ref, cs_scr, ss_scr):
    b = pl.program_id(1)

    @pl.when(b == 0)
    def _():
        gr = gr_ref[...]
        gi = gi_ref[...]
        for a in range(TT_F // DFT_RADIX):
            f1r = f1r_ref[a:a + 1, :]
            f1i = f1i_ref[a:a + 1, :]
            rows = slice(a * DFT_RADIX, (a + 1) * DFT_RADIX)
            cs_scr[rows, :] = (f1r * gr - f1i * gi).astype(BF16)
            ss_scr[rows, :] = (f1r * gi + f1i * gr).astype(BF16)

    u = u_ref[...]
    a_seq = _dot(cs_scr[...], u)
    b_seq = _dot(ss_scr[...], u)
    for g in range(N_FOURIER_GROUPS):
        cols = slice(g * FOURIER_GROUP_DIM, (g + 1) * FOURIER_GROUP_DIM)
        y = _dot(a_seq[:, cols].astype(BF16), cc_ref[...]) - _dot(b_seq[:, cols].astype(BF16), sc_ref[...])
        y_ref[:, cols] = y.astype(BF16)


def _fourier(u, f1r, f1i, gr, gi, cc, sc):
    nt = SEQ // TT_F
    f1_rows = TT_F // DFT_RADIX
    return pl.pallas_call(
        _fourier_kernel,
        grid=(nt, BATCH),
        in_specs=[
            pl.BlockSpec((SEQ, FOURIER_WIDTH), lambda t, b: (b, 0)),
            pl.BlockSpec((f1_rows, SEQ), lambda t, b: (t, 0)),
            pl.BlockSpec((f1_rows, SEQ), lambda t, b: (t, 0)),
            pl.BlockSpec((DFT_RADIX, SEQ), lambda t, b: (0, 0)),
            pl.BlockSpec((DFT_RADIX, SEQ), lambda t, b: (0, 0)),
            pl.BlockSpec((FOURIER_GROUP_DIM, FOURIER_GROUP_DIM), lambda t, b: (0, 0)),
            pl.BlockSpec((FOURIER_GROUP_DIM, FOURIER_GROUP_DIM), lambda t, b: (0, 0)),
        ],
        out_specs=pl.BlockSpec((TT_F, FOURIER_WIDTH), lambda t, b: (b * nt + t, 0)),
        out_shape=jax.ShapeDtypeStruct((N_TOK, FOURIER_WIDTH), BF16),
        scratch_shapes=[pltpu.VMEM((TT_F, SEQ), BF16), pltpu.VMEM((TT_F, SEQ), BF16)],
        compiler_params=_cparams(("arbitrary", "arbitrary"), 56),
        name="fourier",
    )(u, f1r, f1i, gr, gi, cc, sc)


def _alibi_slope(h):
    return float(2.0 ** (-8.0 * (h + 1) / N_Q_HEADS))


def _attention_kernel(sink_ref, q_ref, prev_ref, cur_ref, next_ref, sel_ref, o_ref, kz_scr, vz_scr):
    t = pl.program_id(1)
    band = jnp.concatenate([prev_ref[...], cur_ref[...], next_ref[...]], axis=0)
    for g in range(N_KV_HEADS):
        c, half = divmod(g, 2)
        kcol = band[:, c * LANES:(c + 1) * LANES]
        vcol = band[:, KV_WIDTH + c * LANES:KV_WIDTH + (c + 1) * LANES]
        for d in range(2):
            sel = sel_ref[half * 2 + d]
            kz_scr[2 * g + d] = _dot(kcol, sel).astype(BF16)
            vz_scr[2 * g + d] = _dot(vcol, sel).astype(BF16)

    def sub_block(i, carry):
        q_rows = pl.ds(pl.multiple_of(i * QB, QB), QB)
        k_rows = pl.ds(pl.multiple_of(i * QB, QB), 3 * QB)
        row = lax.broadcasted_iota(jnp.int32, (QB, 3 * QB), 0)
        col = lax.broadcasted_iota(jnp.int32, (QB, 3 * QB), 1)
        rel = row - col + QB
        absrel = jnp.abs(rel)
        kpos = t * TQ + i * QB - QB + col
        mask = (absrel <= WINDOW) & (kpos >= 0) & (kpos < SEQ)
        absrel_f = absrel.astype(F32)
        for g in range(N_KV_HEADS):
            for p in range(2):
                cols = slice((2 * g + p) * LANES, (2 * g + p + 1) * LANES)
                qc = q_ref[q_rows, cols]
                out = jnp.zeros((QB, LANES), F32)
                for d in range(2):
                    h = 4 * g + 2 * p + d
                    kz = kz_scr[2 * g + d, k_rows, :]
                    s = lax.dot_general(qc, kz, (((1,), (1,)), ((), ())), preferred_element_type=F32)
                    s = jnp.where(mask, s - _alibi_slope(h) * absrel_f, NEG_INF)
                    sink = sink_ref[h]
                    m = jnp.maximum(jnp.max(s, axis=-1, keepdims=True), sink)
                    pr = jnp.exp(s - m)
                    den = jnp.sum(pr, axis=-1, keepdims=True) + jnp.exp(sink - m)
                    out = out + _dot(pr.astype(BF16), vz_scr[2 * g + d, k_rows, :]) / den
                o_ref[q_rows, cols] = out.astype(BF16)
        return carry

    lax.fori_loop(0, TQ // QB, sub_block, 0)


def _attention(sink, q, kv, sel):
    nt = SEQ // TQ
    nb = SEQ // QB
    r = TQ // QB
    grid_spec = pltpu.PrefetchScalarGridSpec(
        num_scalar_prefetch=1,
        grid=(BATCH, nt),
        in_specs=[
            pl.BlockSpec((TQ, Q_WIDTH), lambda b, t, s: (b * nt + t, 0)),
            pl.BlockSpec((QB, 2 * KV_WIDTH), lambda b, t, s: (b * nb + jnp.maximum(t * r - 1, 0), 0)),
            pl.BlockSpec((TQ, 2 * KV_WIDTH), lambda b, t, s: (b * nt + t, 0)),
            pl.BlockSpec((QB, 2 * KV_WIDTH), lambda b, t, s: (b * nb + jnp.minimum(t * r + r, nb - 1), 0)),
            pl.BlockSpec((4, LANES, LANES), lambda b, t, s: (0, 0, 0)),
        ],
        out_specs=pl.BlockSpec((TQ, Q_WIDTH), lambda b, t, s: (b * nt + t, 0)),
        scratch_shapes=[pltpu.VMEM((2 * N_KV_HEADS, TQ + 2 * QB, LANES), BF16),
                        pltpu.VMEM((2 * N_KV_HEADS, TQ + 2 * QB, LANES), BF16)],
    )
    return pl.pallas_call(
        _attention_kernel,
        grid_spec=grid_spec,
        out_shape=jax.ShapeDtypeStruct((N_TOK, Q_WIDTH), BF16),
        compiler_params=_cparams(("arbitrary", "arbitrary"), 32),
        name="attention",
    )(sink, q, kv, kv, kv, sel)


def _merge_kernel(yf_ref, at_ref, gf_ref, ga_ref, x_ref, wfo_ref, wao_ref, wo_ref, g2_ref, wr_ref, br_ref,
                  ltri_ref, x1_ref, h2p_ref, meta_ref, cnt_ref, base_scr):
    i = pl.program_id(0)

    @pl.when(i == 0)
    def _():
        base_scr[...] = jnp.zeros_like(base_scr)

    yf = _dot(yf_ref[...], wfo_ref[...])
    ya = _dot(at_ref[...], wao_ref[...])
    merged = gf_ref[...].astype(F32) * yf + ga_ref[...].astype(F32) * ya
    x1 = x_ref[...] + _dot(merged.astype(BF16), wo_ref[...])
    x1_ref[...] = x1
    ms = jnp.mean(x1 * x1, axis=-1, keepdims=True)
    h2 = x1 * lax.rsqrt(ms + RMS_EPS) * g2_ref[...]
    _store_row_tiles(h2p_ref, _pack_halves(h2[:, :HALF], h2[:, HALF:]))
    logits = _dot(h2.astype(BF16), wr_ref[...]) + br_ref[...]

    lane = lax.broadcasted_iota(jnp.int32, logits.shape, 1)
    lane_f = lane.astype(F32)
    vals = logits
    top_v, top_i, onehots = [], [], []
    for _ in range(TOP_K):
        m = jnp.max(vals, axis=-1, keepdims=True)
        idx = jnp.min(jnp.where(vals == m, lane_f, float(LANES)), axis=-1, keepdims=True)
        oh = lane_f == idx
        top_v.append(m)
        top_i.append(idx)
        onehots.append(oh)
        vals = jnp.where(oh, -jnp.inf, vals)
    exps = [jnp.exp(v - top_v[0]) for v in top_v]
    den = exps[0] + exps[1] + exps[2] + exps[3]
    gates = [e / den for e in exps]

    cnt = sum(oh.astype(F32) for oh in onehots)
    prefix = _dot(ltri_ref[...], cnt.astype(BF16))
    tot = base_scr[...] + prefix
    ranks = [jnp.sum(jnp.where(oh, tot, 0.0), axis=-1, keepdims=True) for oh in onehots]
    base_scr[...] = base_scr[...] + jnp.sum(cnt, axis=0, keepdims=True)
    cnt_ref[...] = jnp.broadcast_to(base_scr[...], cnt_ref.shape)

    meta = jnp.zeros(logits.shape, F32)
    for k in range(TOP_K):
        meta = jnp.where(lane == k, top_i[k], meta)
        meta = jnp.where(lane == TOP_K + k, ranks[k], meta)
        meta = jnp.where(lane == 2 * TOP_K + k, gates[k], meta)
    meta_ref[...] = meta


def _merge(yf, attn, gates, x2, wfo, wao, wo, g2, wr, br, ltri):
    tm = TM_MG
    const = lambda i: (0, 0)
    resident = functools.partial(pl.BlockSpec, index_map=const, pipeline_mode=pl.Buffered(1))
    return pl.pallas_call(
        _merge_kernel,
        grid=(N_TOK // tm,),
        in_specs=[
            pl.BlockSpec((tm, FOURIER_WIDTH), lambda i: (i, 0)),
            pl.BlockSpec((tm, Q_WIDTH), lambda i: (i, 0)),
            pl.BlockSpec((tm, D_MODEL), lambda i: (i, 0)),
            pl.BlockSpec((tm, D_MODEL), lambda i: (i, 1)),
            pl.BlockSpec((tm, D_MODEL), lambda i: (i, 0)),
            resident((FOURIER_WIDTH, D_MODEL)),
            resident((Q_WIDTH, D_MODEL)),
            resident((D_MODEL, D_MODEL)),
            pl.BlockSpec((1, D_MODEL), const),
            pl.BlockSpec((D_MODEL, LANES), const),
            pl.BlockSpec((1, LANES), const),
            pl.BlockSpec((tm, tm), const),
        ],
        out_specs=[
            pl.BlockSpec((tm, D_MODEL), lambda i: (i, 0)),
            pl.BlockSpec((tm * SUBLANES, LANES), lambda i: (i, 0)),
            pl.BlockSpec((tm, LANES), lambda i: (i, 0)),
            pl.BlockSpec((8, LANES), const),
        ],
        out_shape=[
            jax.ShapeDtypeStruct((N_TOK, D_MODEL), F32),
            jax.ShapeDtypeStruct((N_TOK * SUBLANES, LANES), U32),
            jax.ShapeDtypeStruct((N_TOK, LANES), F32),
            jax.ShapeDtypeStruct((8, LANES), F32),
        ],
        scratch_shapes=[pltpu.VMEM((1, LANES), F32)],
        compiler_params=_cparams(("arbitrary",), 56),
        name="merge",
    )(yf, attn, gates, gates, x2, wfo, wao, wo, g2, wr, br, ltri)


def _expert_kernel(e_ref, live_ref, tok_cur_ref, tok_nxt_ref, dst_cur_ref,
                   h2p_hbm, wg_ref, bg_ref, wu_ref, bu_ref, wd_ref, bd_ref,
                   yslots_hbm, xu_scr, acc_scr, xs_scr, wgb, wub, wdb, gsem, ssem):
    del e_ref
    w = pl.program_id(0)
    f = pl.program_id(1)
    live = live_ref[w] > 0
    prev_live = (w > 0) & (live_ref[jnp.maximum(w - 1, 0)] > 0)
    cur = w & 1
    nxt = 1 - cur
    tile_rows = RT * SUBLANES

    def sub_rows(s):
        return pl.ds(pl.multiple_of(s * RT, RT), RT)

    def sub_tiles(slot, s):
        return xu_scr.at[slot, pl.ds(pl.multiple_of(s * tile_rows, tile_rows), tile_rows), :]

    def row_tile(row):
        return pl.ds(pl.multiple_of(row * SUBLANES, SUBLANES), SUBLANES)

    def gather_copy(tok_ref, slot, row):
        tok = tok_ref[0, 0, row]
        return pltpu.make_async_copy(h2p_hbm.at[row_tile(tok), :], xu_scr.at[slot, row_tile(row), :], gsem.at[slot])

    def scatter_copy(row):
        dst = dst_cur_ref[0, 0, row]
        return pltpu.make_async_copy(xu_scr.at[cur, row_tile(row), :], yslots_hbm.at[row_tile(dst), :], ssem)

    def for_all_rows(fn):
        def body(row, carry):
            fn(row)
            return carry
        lax.fori_loop(0, R_MAX, body, 0, unroll=8)

    def prefetch_next(s):
        base = s * RT + f * GATHER_PER_STEP
        for g in range(GATHER_PER_STEP):
            gather_copy(tok_nxt_ref, nxt, base + g).start(priority=g % 2)

    def cast_weights():
        wgb[...] = wg_ref[...].astype(BF16)
        wub[...] = wu_ref[...].astype(BF16)
        wdb[...] = wd_ref[...].astype(BF16)

    def stage_inputs(first_sub, n_sub, stage_row0):
        n_rows = n_sub * RT
        tiles = xu_scr.at[cur, pl.ds(pl.multiple_of(first_sub * tile_rows, tile_rows), n_sub * tile_rows), :]
        stage = xs_scr.at[pl.ds(stage_row0, n_rows), :]
        for j in range(SUBLANES):
            lo, hi = _unpack_halves(_load_row_tiles(tiles, j, n_rows))
            stage[:, j * LANES:(j + 1) * LANES] = lo.astype(BF16)
            stage[:, HALF + j * LANES:HALF + (j + 1) * LANES] = hi.astype(BF16)

    def activations(n_sub, stage_row0):
        x = xs_scr[pl.ds(stage_row0, n_sub * RT), :]
        g = _dot(x, wgb[...]) + bg_ref[...]
        u = _dot(x, wub[...]) + bu_ref[...]
        g = jnp.minimum(g, SWIGLU_LIMIT)
        u = jnp.clip(u, -SWIGLU_LIMIT, SWIGLU_LIMIT)
        act = (u + 1.0) * (g * (1.0 / (1.0 + jnp.exp(-SWIGLU_ALPHA * g))))
        return act.astype(BF16)

    def accumulate(j):
        n_rows = GROUP * RT
        rows = pl.ds(pl.multiple_of(j * n_rows, n_rows), n_rows)
        stage_inputs(j * GROUP, GROUP, 0)
        c = _dot(activations(GROUP, 0), wdb[...])
        acc_scr[rows, :] = jnp.where(f == 0, c, acc_scr[rows, :] + c)

    def finish(s):
        act = activations(1, (s % GROUP) * RT)
        rows = sub_rows(s)
        tiles = sub_tiles(cur, s)
        for c in range(HALF // OUT_CHUNK):
            lo_cols = slice(c * OUT_CHUNK, (c + 1) * OUT_CHUNK)
            hi_cols = slice(HALF + c * OUT_CHUNK, HALF + (c + 1) * OUT_CHUNK)
            y_lo = acc_scr[rows, lo_cols] + _dot(act, wdb[:, lo_cols]) + bd_ref[:, lo_cols]
            y_hi = acc_scr[rows, hi_cols] + _dot(act, wdb[:, hi_cols]) + bd_ref[:, hi_cols]
            words = _pack_halves(y_lo, y_hi)
            for jj in range(OUT_CHUNK // LANES):
                j = c * (OUT_CHUNK // LANES) + jj
                tiles[pl.ds(j, RT, stride=SUBLANES), :] = words[:, jj * LANES:(jj + 1) * LANES]

    @pl.when(f == 0)
    def _():
        @pl.when(w == 0)
        def _():
            def zero(s, carry):
                acc_scr[sub_rows(s), :] = jnp.zeros((RT, D_MODEL), F32)
                xu_scr[1, pl.ds(pl.multiple_of(s * tile_rows, tile_rows), tile_rows), :] = (
                    jnp.zeros((tile_rows, LANES), U32))
                return carry
            lax.fori_loop(0, N_SUB, zero, 0)
            pad_rows = pl.ds(TOP_K * N_TOK * SUBLANES, R_MAX * SUBLANES)
            fill = pltpu.make_async_copy(xu_scr.at[1], yslots_hbm.at[pad_rows, :], ssem)
            fill.start()
            fill.wait()
            for_all_rows(lambda row: gather_copy(tok_cur_ref, cur, row).start())

        @pl.when((w == 0) | prev_live)
        def _():
            first_tile = pl.ds(0, SUBLANES)
            row_in = pltpu.make_async_copy(h2p_hbm.at[first_tile, :], xu_scr.at[cur, first_tile, :], gsem.at[cur])
            for_all_rows(lambda row: row_in.wait())

        @pl.when(prev_live)
        def _():
            first_tile = pl.ds(0, SUBLANES)
            row_out = pltpu.make_async_copy(xu_scr.at[cur, first_tile, :], yslots_hbm.at[first_tile, :], ssem)
            for_all_rows(lambda row: row_out.wait())

    @pl.when(live & (f < NF - 1))
    def _():
        cast_weights()

        def group(j, carry):
            accumulate(j)
            for i in range(GROUP):
                prefetch_next(j * GROUP + i)
            return carry
        lax.fori_loop(0, N_SUB // GROUP, group, 0)

    @pl.when(live & (f == NF - 1))
    def _():
        cast_weights()
        stage_inputs(0, 1, 0)
        for s in range(N_SUB):
            finish(s)
            if s + 1 < N_SUB:
                stage_inputs(s + 1, 1, ((s + 1) % GROUP) * RT)
            for r in range(RT):
                scatter_copy(s * RT + r).start(priority=r % 2)
            prefetch_next(s)


def _experts(item_e, item_live, item_tok, item_dst, h2p, wg, bg, wu, bu, wd, bd):
    def w_in_map(w, f, e, lv):
        return (e[w], 0, jnp.where(lv[w] > 0, f, NF - 1))

    def w_down_map(w, f, e, lv):
        return (e[w], jnp.where(lv[w] > 0, f, NF - 1), 0)

    def b_down_map(w, f, e, lv):
        return (e[w], 0, 0)

    grid_spec = pltpu.PrefetchScalarGridSpec(
        num_scalar_prefetch=2,
        grid=(N_ITEMS, NF),
        in_specs=[
            pl.BlockSpec((1, 1, R_MAX), lambda w, f, e, lv: (w, 0, 0), memory_space=pltpu.SMEM),
            pl.BlockSpec((1, 1, R_MAX), lambda w, f, e, lv: (jnp.minimum(w + 1, N_ITEMS - 1), 0, 0),
                         memory_space=pltpu.SMEM),
            pl.BlockSpec((1, 1, R_MAX), lambda w, f, e, lv: (w, 0, 0), memory_space=pltpu.SMEM),
            pl.BlockSpec(memory_space=pl.ANY),
            pl.BlockSpec((None, D_MODEL, FC), w_in_map),
            pl.BlockSpec((None, 1, FC), w_in_map),
            pl.BlockSpec((None, D_MODEL, FC), w_in_map),
            pl.BlockSpec((None, 1, FC), w_in_map),
            pl.BlockSpec((None, FC, D_MODEL), w_down_map),
            pl.BlockSpec((None, 1, D_MODEL), b_down_map),
        ],
        out_specs=pl.BlockSpec(memory_space=pl.ANY),
        scratch_shapes=[
            pltpu.VMEM((2, R_MAX * SUBLANES, LANES), U32),
            pltpu.VMEM((R_MAX, D_MODEL), F32),
            pltpu.VMEM((GROUP * RT, D_MODEL), BF16),
            pltpu.VMEM((D_MODEL, FC), BF16),
            pltpu.VMEM((D_MODEL, FC), BF16),
            pltpu.VMEM((FC, D_MODEL), BF16),
            pltpu.SemaphoreType.DMA((2,)),
            pltpu.SemaphoreType.DMA(()),
        ],
    )
    return pl.pallas_call(
        _expert_kernel,
        grid_spec=grid_spec,
        out_shape=jax.ShapeDtypeStruct((SLOT_ROWS * SUBLANES, LANES), U32),
        compiler_params=_cparams(("arbitrary", "arbitrary"), 60),
        name="experts",
    )(item_e, item_live, item_tok, item_tok, item_dst, h2p, wg, bg, wu, bu, wd, bd)


def _combine_kernel(meta_ref, x1_ref, y0_ref, y1_ref, y2_ref, y3_ref, out_ref):
    gates = [meta_ref[:, 2 * TOP_K + k:2 * TOP_K + k + 1] for k in range(TOP_K)]
    for j in range(SUBLANES):
        lo_cols = slice(j * LANES, (j + 1) * LANES)
        hi_cols = slice(HALF + j * LANES, HALF + (j + 1) * LANES)
        lo_acc = x1_ref[:, lo_cols]
        hi_acc = x1_ref[:, hi_cols]
        for k, y_ref in enumerate((y0_ref, y1_ref, y2_ref, y3_ref)):
            lo, hi = _unpack_halves(_load_row_tiles(y_ref, j, TM_CB))
            lo_acc = lo_acc + gates[k] * lo
            hi_acc = hi_acc + gates[k] * hi
        out_ref[:, lo_cols] = lo_acc
        out_ref[:, hi_cols] = hi_acc


def _combine(meta, x1, y_slots):
    tm = TM_CB
    nblk = N_TOK // tm
    slot_specs = [pl.BlockSpec((tm * SUBLANES, LANES), lambda i, k=k: (k * nblk + i, 0)) for k in range(TOP_K)]
    return pl.pallas_call(
        _combine_kernel,
        grid=(nblk,),
        in_specs=[
            pl.BlockSpec((tm, LANES), lambda i: (i, 0)),
            pl.BlockSpec((tm, D_MODEL), lambda i: (i, 0)),
        ] + slot_specs,
        out_specs=pl.BlockSpec((tm, D_MODEL), lambda i: (i, 0)),
        out_shape=jax.ShapeDtypeStruct((N_TOK, D_MODEL), F32),
        compiler_params=_cparams(("arbitrary",), 32),
        name="combine",
    )(meta, x1, y_slots, y_slots, y_slots, y_slots)


def _dft_tables():
    r = DFT_RADIX
    k = np.arange(r)
    e64 = np.exp(2j * np.pi * np.outer(k, k) / r)
    e4096 = np.exp(2j * np.pi * np.outer(k, k) / SEQ)
    s = np.arange(SEQ)
    f1 = e64[:, s % r]
    g = e64[:, s // r] * e4096[:, s % r] / math.sqrt(SEQ)
    c = np.arange(FOURIER_GROUP_DIM)
    ang = 2.0 * np.pi * np.outer(c, c) / FOURIER_GROUP_DIM
    scale = 1.0 / math.sqrt(FOURIER_GROUP_DIM)
    as32 = lambda a: jnp.asarray(a.astype(np.float32))
    return (as32(f1.real), as32(f1.imag), as32(g.real), as32(g.imag),
            jnp.asarray((np.cos(ang) * scale).astype(np.float32)).astype(BF16),
            jnp.asarray((np.sin(ang) * scale).astype(np.float32)).astype(BF16))


def _head_selectors():
    i = np.arange(LANES)[:, None]
    j = np.arange(LANES)[None, :]
    out = np.zeros((4, LANES, LANES), np.float32)
    for src in range(2):
        for dst in range(2):
            out[2 * src + dst] = (i - HEAD_DIM * src == j - HEAD_DIM * dst) & (j // HEAD_DIM == dst)
    return jnp.asarray(out).astype(BF16)


def kernel(x, norm1_g, w_in, b_branch_gate, q_norm_g, k_norm_g, attn_sink, w_fourier_out, w_attn_out, w_o,
           norm2_g, w_router, b_router, w_gate_e, b_gate_e, w_up_e, b_up_e, w_down_e, b_down_e):
    b, s, d = x.shape
    assert (b, s, d) == (BATCH, SEQ, D_MODEL) and norm1_g.shape[0] == 1
    x2 = x.reshape(N_TOK, D_MODEL)

    heads_per_tile = TN_IN // HEAD_DIM
    ones_bd = jnp.asarray((np.arange(TN_IN)[:, None] // HEAD_DIM == np.arange(TN_IN)[None, :] // HEAD_DIM)
                          .astype(np.float32)).astype(BF16)
    u_f, q, kv, gates = _in_proj(
        x2, norm1_g[0].reshape(1, D_MODEL), w_in[0].astype(BF16), b_branch_gate[0].reshape(1, 2 * D_MODEL),
        jnp.tile(q_norm_g[0], heads_per_tile).reshape(1, TN_IN),
        jnp.tile(k_norm_g[0], heads_per_tile).reshape(1, TN_IN), ones_bd)
    y_f = _fourier(u_f, *_dft_tables())
    attn = _attention(attn_sink[0], q, kv, _head_selectors())

    wr = jnp.zeros((D_MODEL, LANES), BF16).at[:, :N_EXPERTS].set(w_router[0].astype(BF16))
    br = jnp.full((1, LANES), NEG_INF, F32).at[0, :N_EXPERTS].set(b_router[0])
    ltri = jnp.asarray(np.tril(np.ones((TM_MG, TM_MG), np.float32), -1)).astype(BF16)
    x1, h2p, meta, cnt = _merge(y_f, attn, gates, x2, w_fourier_out[0].astype(BF16), w_attn_out[0].astype(BF16),
                                w_o[0].astype(BF16), norm2_g[0].reshape(1, D_MODEL), wr, br, ltri)

    tables = _routing_tables(cnt[0, :N_EXPERTS].astype(jnp.int32), meta[:, 0:TOP_K].astype(jnp.int32),
                             meta[:, TOP_K:2 * TOP_K].astype(jnp.int32))
    y_slots = _experts(*tables, h2p,
                       w_gate_e[0], b_gate_e[0].reshape(N_EXPERTS, 1, D_EXPERT),
                       w_up_e[0], b_up_e[0].reshape(N_EXPERTS, 1, D_EXPERT),
                       w_down_e[0], b_down_e[0].reshape(N_EXPERTS, 1, D_MODEL))
    out = _combine(meta, x1, y_slots)
    return out.reshape(BATCH, SEQ, D_MODEL)


def _routing_tables(counts, top_idx, rank):
    padded = ((counts + RT - 1) // RT) * RT
    pend = jnp.cumsum(padded)
    pstart = pend - padded
    dest = pstart[top_idx] + rank
    n_assign = TOP_K * N_TOK
    row_assign = jnp.full((ROWS,), -1, jnp.int32).at[dest.reshape(n_assign)].set(
        jnp.arange(n_assign, dtype=jnp.int32), unique_indices=True)

    items_per_e = (padded + R_MAX - 1) // R_MAX
    it_end = jnp.cumsum(items_per_e)
    it_start = it_end - items_per_e
    total_items = it_end[-1]
    wi = jnp.arange(N_ITEMS, dtype=jnp.int32)
    e_w = jnp.minimum(jnp.searchsorted(it_end, wi, side="right"), N_EXPERTS - 1).astype(jnp.int32)
    j_w = wi - it_start[e_w]
    live = wi < total_items
    rows_w = jnp.clip(padded[e_w] - j_w * R_MAX, 0, R_MAX)
    e_last = e_w[jnp.maximum(total_items - 1, 0)]
    item_e = jnp.where(live, e_w, e_last).astype(jnp.int32)
    item_row0 = jnp.where(live, pstart[e_w] + j_w * R_MAX, 0).astype(jnp.int32)
    item_nsub = jnp.where(live, rows_w // RT, 0).astype(jnp.int32)
    item_live = live.astype(jnp.int32)
    r = jnp.arange(R_MAX, dtype=jnp.int32)
    src = jnp.minimum(item_row0[:, None] + r[None, :], ROWS - 1)
    item_rows = jnp.where(r[None, :] < (item_nsub * RT)[:, None], row_assign[src], -1)
    item_tok = (jnp.maximum(item_rows, 0) // TOP_K).reshape(N_ITEMS, 1, R_MAX)
    item_dst = jnp.where(item_rows >= 0, (item_rows % TOP_K) * N_TOK + item_rows // TOP_K,
                         TOP_K * N_TOK + r[None, :]).reshape(N_ITEMS, 1, R_MAX)
    return item_e, item_live, item_tok, item_dst
```

```python
import functools
import math

import jax
import jax.numpy as jnp
import numpy as np
from jax import lax
from jax.experimental import pallas as pl
from jax.experimental.pallas import tpu as pltpu

D_MODEL = 2048
BATCH = 4
SEQ = 4096
N_TOK = BATCH * SEQ
N_Q_HEADS = 16
N_KV_HEADS = 4
HEAD_DIM = 64
Q_WIDTH = N_Q_HEADS * HEAD_DIM
KV_WIDTH = N_KV_HEADS * HEAD_DIM
WINDOW = 128
N_FOURIER_GROUPS = 4
FOURIER_GROUP_DIM = 256
FOURIER_WIDTH = N_FOURIER_GROUPS * FOURIER_GROUP_DIM
IN_WIDTH = FOURIER_WIDTH + Q_WIDTH + 2 * KV_WIDTH + 2 * D_MODEL
N_EXPERTS = 32
TOP_K = 4
D_EXPERT = D_MODEL
SWIGLU_LIMIT = 7.0
SWIGLU_ALPHA = 1.702
RMS_EPS = 1e-6
NEG_INF = -1e30

F32 = jnp.float32
BF16 = jnp.bfloat16
U32 = jnp.uint32

V7X_VMEM_BYTES = 64 * 1024 * 1024
LANES = 128
SUBLANES = 8
MIB = 1024 * 1024

TM_IN = 1024
TN_IN = 512
NORM_ROWS = 64
IN_SLABS = 4
TT_F = 512
DFT_RADIX = 64
TQ = 512
QB = 128
TM_MG = 256
RT = 256
R_MAX = 9 * RT
FC = 256
NF = D_EXPERT // FC
ROWS = TOP_K * N_TOK + N_EXPERTS * RT
N_ITEMS = (TOP_K * N_TOK + N_EXPERTS * (RT - 1) + N_EXPERTS * (R_MAX - RT)) // R_MAX + 1
TM_CB = 256
HALF = D_MODEL // 2
N_SUB = R_MAX // RT
GROUP = 3
OUT_CHUNK = 256
GATHER_PER_STEP = RT // NF
SLOT_ROWS = TOP_K * N_TOK + R_MAX


def _cparams(sem, vmem_mib):
    return pltpu.CompilerParams(dimension_semantics=sem, vmem_limit_bytes=vmem_mib * MIB)


def _dot(a, b):
    return jnp.dot(a, b, preferred_element_type=F32)


def _pack_halves(lo, hi):
    lo_bits = pltpu.bitcast(lo.astype(BF16).astype(F32), U32)
    hi_bits = pltpu.bitcast(hi.astype(BF16).astype(F32), U32)
    return (hi_bits & jnp.uint32(0xFFFF0000)) | (lo_bits >> jnp.uint32(16))


def _unpack_halves(words):
    lo = pltpu.bitcast(words << jnp.uint32(16), F32)
    hi = pltpu.bitcast(words & jnp.uint32(0xFFFF0000), F32)
    return lo, hi


def _store_row_tiles(tiles_ref, words):
    n_rows = words.shape[0]
    for j in range(SUBLANES):
        tiles_ref[pl.ds(j, n_rows, stride=SUBLANES), :] = words[:, j * LANES:(j + 1) * LANES]


def _load_row_tiles(tiles_ref, j, n_rows):
    return tiles_ref[pl.ds(j, n_rows, stride=SUBLANES), :]


def _in_proj_kernel(x_ref, g1_ref, w_ref, b_ref, qg_ref, kg_ref, ones_ref,
                    u_ref, q_ref, kv_ref, gate_ref, h_scr):
    j = pl.program_id(1)

    @pl.when(j == 0)
    def _():
        def body(c, carry):
            rows = pl.ds(pl.multiple_of(c * NORM_ROWS, NORM_ROWS), NORM_ROWS)
            x = x_ref[rows, :]
            ms = jnp.mean(x * x, axis=-1, keepdims=True)
            h_scr[rows, :] = (x * lax.rsqrt(ms + RMS_EPS) * g1_ref[...]).astype(BF16)
            return carry
        lax.fori_loop(0, TM_IN // NORM_ROWS, body, 0)

    def head_norm(a, gain):
        sq = a * a
        hi = sq.astype(BF16)
        lo = (sq - hi.astype(F32)).astype(BF16)
        ssq = _dot(hi, ones_ref[...]) + _dot(lo, ones_ref[...])
        return a * lax.rsqrt(ssq * (1.0 / HEAD_DIM) + RMS_EPS) * gain

    def project(out_ref, epilogue):
        slab = TM_IN // IN_SLABS
        for s in range(IN_SLABS):
            rows = slice(s * slab, (s + 1) * slab)
            acc = _dot(h_scr[rows, :], w_ref[...])
            out_ref[rows, :] = epilogue(acc).astype(BF16)

    @pl.when(j < 2)
    def _():
        project(u_ref, lambda acc: acc)

    @pl.when((j == 2) | (j == 3))
    def _():
        project(q_ref, lambda acc: head_norm(acc, qg_ref[...]) * (HEAD_DIM ** -0.5))

    @pl.when(j == 4)
    def _():
        def k_norm_v_pass(acc):
            lane = lax.broadcasted_iota(jnp.int32, acc.shape, 1)
            return jnp.where(lane < KV_WIDTH, head_norm(acc, kg_ref[...]), acc)
        project(kv_ref, k_norm_v_pass)

    @pl.when(j >= 5)
    def _():
        project(gate_ref, lambda acc: 1.0 / (1.0 + jnp.exp(-(acc + b_ref[...]))))


def _in_proj(x2, g1, w_bf, bias, qg_t, kg_t, ones_bd):
    n_col = IN_WIDTH // TN_IN
    grid = (N_TOK // TM_IN, n_col)
    return pl.pallas_call(
        _in_proj_kernel,
        grid=grid,
        in_specs=[
            pl.BlockSpec((TM_IN, D_MODEL), lambda i, j: (i, 0)),
            pl.BlockSpec((1, D_MODEL), lambda i, j: (0, 0)),
            pl.BlockSpec((None, D_MODEL, TN_IN), lambda i, j: (j, 0, 0)),
            pl.BlockSpec((1, TN_IN), lambda i, j: (0, jnp.maximum(j - 5, 0))),
            pl.BlockSpec((1, TN_IN), lambda i, j: (0, 0)),
            pl.BlockSpec((1, TN_IN), lambda i, j: (0, 0)),
            pl.BlockSpec((TN_IN, TN_IN), lambda i, j: (0, 0)),
        ],
        out_specs=[
            pl.BlockSpec((TM_IN, TN_IN), lambda i, j: (i, jnp.minimum(j, 1))),
            pl.BlockSpec((TM_IN, TN_IN), lambda i, j: (i, jnp.clip(j - 2, 0, 1))),
            pl.BlockSpec((TM_IN, TN_IN), lambda i, j: (i, 0)),
            pl.BlockSpec((TM_IN, TN_IN), lambda i, j: (i, jnp.maximum(j - 5, 0))),
        ],
        out_shape=[
            jax.ShapeDtypeStruct((N_TOK, FOURIER_WIDTH), BF16),
            jax.ShapeDtypeStruct((N_TOK, Q_WIDTH), BF16),
            jax.ShapeDtypeStruct((N_TOK, 2 * KV_WIDTH), BF16),
            jax.ShapeDtypeStruct((N_TOK, 2 * D_MODEL), BF16),
        ],
        scratch_shapes=[pltpu.VMEM((TM_IN, D_MODEL), BF16)],
        compiler_params=_cparams(("arbitrary", "arbitrary"), 56),
        name="in_proj",
    )(x2, g1, w_bf, bias, qg_t, kg_t, ones_bd)


def _fourier_kernel(u_ref, f1r_ref, f1i_ref, gr_ref, gi_ref, cc_ref, sc_ref, y_ref, cs_scr, ss_scr):
    b = pl.program_id(1)

    @pl.when(b == 0)
    def _():
        gr = gr_ref[...]
        gi = gi_ref[...]
        for a in range(TT_F // DFT_RADIX):
            f1r = f1r_ref[a:a + 1, :]
            f1i = f1i_ref[a:a + 1, :]
            rows = slice(a * DFT_RADIX, (a + 1) * DFT_RADIX)
            cs_scr[rows, :] = (f1r * gr - f1i * gi).astype(BF16)
            ss_scr[rows, :] = (f1r * gi + f1i * gr).astype(BF16)

    u = u_ref[...]
    a_seq = _dot(cs_scr[...], u)
    b_seq = _dot(ss_scr[...], u)
    for g in range(N_FOURIER_GROUPS):
        cols = slice(g * FOURIER_GROUP_DIM, (g + 1) * FOURIER_GROUP_DIM)
        y = _dot(a_seq[:, cols].astype(BF16), cc_ref[...]) - _dot(b_seq[:, cols].astype(BF16), sc_ref[...])
        y_ref[:, cols] = y.astype(BF16)


def _fourier(u, f1r, f1i, gr, gi, cc, sc):
    nt = SEQ // TT_F
    f1_rows = TT_F // DFT_RADIX
    return pl.pallas_call(
        _fourier_kernel,
        grid=(nt, BATCH),
        in_specs=[
            pl.BlockSpec((SEQ, FOURIER_WIDTH), lambda t, b: (b, 0)),
            pl.BlockSpec((f1_rows, SEQ), lambda t, b: (t, 0)),
            pl.BlockSpec((f1_rows, SEQ), lambda t, b: (t, 0)),
            pl.BlockSpec((DFT_RADIX, SEQ), lambda t, b: (0, 0)),
            pl.BlockSpec((DFT_RADIX, SEQ), lambda t, b: (0, 0)),
            pl.BlockSpec((FOURIER_GROUP_DIM, FOURIER_GROUP_DIM), lambda t, b: (0, 0)),
            pl.BlockSpec((FOURIER_GROUP_DIM, FOURIER_GROUP_DIM), lambda t, b: (0, 0)),
        ],
        out_specs=pl.BlockSpec((TT_F, FOURIER_WIDTH), lambda t, b: (b * nt + t, 0)),
        out_shape=jax.ShapeDtypeStruct((N_TOK, FOURIER_WIDTH), BF16),
        scratch_shapes=[pltpu.VMEM((TT_F, SEQ), BF16), pltpu.VMEM((TT_F, SEQ), BF16)],
        compiler_params=_cparams(("arbitrary", "arbitrary"), 56),
        name="fourier",
    )(u, f1r, f1i, gr, gi, cc, sc)


def _alibi_slope(h):
    return float(2.0 ** (-8.0 * (h + 1) / N_Q_HEADS))


def _attention_kernel(sink_ref, q_ref, prev_ref, cur_ref, next_ref, sel_ref, o_ref, kz_scr, vz_scr):
    t = pl.program_id(1)
    band = jnp.concatenate([prev_ref[...], cur_ref[...], next_ref[...]], axis=0)
    for g in range(N_KV_HEADS):
        c, half = divmod(g, 2)
        kcol = band[:, c * LANES:(c + 1) * LANES]
        vcol = band[:, KV_WIDTH + c * LANES:KV_WIDTH + (c + 1) * LANES]
        for d in range(2):
            sel = sel_ref[half * 2 + d]
            kz_scr[2 * g + d] = _dot(kcol, sel).astype(BF16)
            vz_scr[2 * g + d] = _dot(vcol, sel).astype(BF16)

    def sub_block(i, carry):
        q_rows = pl.ds(pl.multiple_of(i * QB, QB), QB)
        k_rows = pl.ds(pl.multiple_of(i * QB, QB), 3 * QB)
        row = lax.broadcasted_iota(jnp.int32, (QB, 3 * QB), 0)
        col = lax.broadcasted_iota(jnp.int32, (QB, 3 * QB), 1)
        rel = row - col + QB
        absrel = jnp.abs(rel)
        kpos = t * TQ + i * QB - QB + col
        mask = (absrel <= WINDOW) & (kpos >= 0) & (kpos < SEQ)
        absrel_f = absrel.astype(F32)
        for g in range(N_KV_HEADS):
            for p in range(2):
                cols = slice((2 * g + p) * LANES, (2 * g + p + 1) * LANES)
                qc = q_ref[q_rows, cols]
                out = jnp.zeros((QB, LANES), F32)
                for d in range(2):
                    h = 4 * g + 2 * p + d
                    kz = kz_scr[2 * g + d, k_rows, :]
                    s = lax.dot_general(qc, kz, (((1,), (1,)), ((), ())), preferred_element_type=F32)
                    s = jnp.where(mask, s - _alibi_slope(h) * absrel_f, NEG_INF)
                    sink = sink_ref[h]
                    m = jnp.maximum(jnp.max(s, axis=-1, keepdims=True), sink)
                    pr = jnp.exp(s - m)
                    den = jnp.sum(pr, axis=-1, keepdims=True) + jnp.exp(sink - m)
                    out = out + _dot(pr.astype(BF16), vz_scr[2 * g + d, k_rows, :]) / den
                o_ref[q_rows, cols] = out.astype(BF16)
        return carry

    lax.fori_loop(0, TQ // QB, sub_block, 0)


def _attention(sink, q, kv, sel):
    nt = SEQ // TQ
    nb = SEQ // QB
    r = TQ // QB
    grid_spec = pltpu.PrefetchScalarGridSpec(
        num_scalar_prefetch=1,
        grid=(BATCH, nt),
        in_specs=[
            pl.BlockSpec((TQ, Q_WIDTH), lambda b, t, s: (b * nt + t, 0)),
            pl.BlockSpec((QB, 2 * KV_WIDTH), lambda b, t, s: (b * nb + jnp.maximum(t * r - 1, 0), 0)),
            pl.BlockSpec((TQ, 2 * KV_WIDTH), lambda b, t, s: (b * nt + t, 0)),
            pl.BlockSpec((QB, 2 * KV_WIDTH), lambda b, t, s: (b * nb + jnp.minimum(t * r + r, nb - 1), 0)),
            pl.BlockSpec((4, LANES, LANES), lambda b, t, s: (0, 0, 0)),
        ],
        out_specs=pl.BlockSpec((TQ, Q_WIDTH), lambda b, t, s: (b * nt + t, 0)),
        scratch_shapes=[pltpu.VMEM((2 * N_KV_HEADS, TQ + 2 * QB, LANES), BF16),
                        pltpu.VMEM((2 * N_KV_HEADS, TQ + 2 * QB, LANES), BF16)],
    )
    return pl.pallas_call(
        _attention_kernel,
        grid_spec=grid_spec,
        out_shape=jax.ShapeDtypeStruct((N_TOK, Q_WIDTH), BF16),
        compiler_params=_cparams(("arbitrary", "arbitrary"), 32),
        name="attention",
    )(sink, q, kv, kv, kv, sel)


def _merge_kernel(yf_ref, at_ref, gf_ref, ga_ref, x_ref, wfo_ref, wao_ref, wo_ref, g2_ref, wr_ref, br_ref,
                  ltri_ref, x1_ref, h2p_ref, meta_ref, cnt_ref, base_scr):
    i = pl.program_id(0)

    @pl.when(i == 0)
    def _():
        base_scr[...] = jnp.zeros_like(base_scr)

    yf = _dot(yf_ref[...], wfo_ref[...])
    ya = _dot(at_ref[...], wao_ref[...])
    merged = gf_ref[...].astype(F32) * yf + ga_ref[...].astype(F32) * ya
    x1 = x_ref[...] + _dot(merged.astype(BF16), wo_ref[...])
    x1_ref[...] = x1
    ms = jnp.mean(x1 * x1, axis=-1, keepdims=True)
    h2 = x1 * lax.rsqrt(ms + RMS_EPS) * g2_ref[...]
    _store_row_tiles(h2p_ref, _pack_halves(h2[:, :HALF], h2[:, HALF:]))
    logits = _dot(h2.astype(BF16), wr_ref[...]) + br_ref[...]

    lane = lax.broadcasted_iota(jnp.int32, logits.shape, 1)
    lane_f = lane.astype(F32)
    vals = logits
    top_v, top_i, onehots = [], [], []
    for _ in range(TOP_K):
        m = jnp.max(vals, axis=-1, keepdims=True)
        idx = jnp.min(jnp.where(vals == m, lane_f, float(LANES)), axis=-1, keepdims=True)
        oh = lane_f == idx
        top_v.append(m)
        top_i.append(idx)
        onehots.append(oh)
        vals = jnp.where(oh, -jnp.inf, vals)
    exps = [jnp.exp(v - top_v[0]) for v in top_v]
    den = exps[0] + exps[1] + exps[2] + exps[3]
    gates = [e / den for e in exps]

    cnt = sum(oh.astype(F32) for oh in onehots)
    prefix = _dot(ltri_ref[...], cnt.astype(BF16))
    tot = base_scr[...] + prefix
    ranks = [jnp.sum(jnp.where(oh, tot, 0.0), axis=-1, keepdims=True) for oh in onehots]
    base_scr[...] = base_scr[...] + jnp.sum(cnt, axis=0, keepdims=True)
    cnt_ref[...] = jnp.broadcast_to(base_scr[...], cnt_ref.shape)

    meta = jnp.zeros(logits.shape, F32)
    for k in range(TOP_K):
        meta = jnp.where(lane == k, top_i[k], meta)
        meta = jnp.where(lane == TOP_K + k, ranks[k], meta)
        meta = jnp.where(lane == 2 * TOP_K + k, gates[k], meta)
    meta_ref[...] = meta


def _merge(yf, attn, gates, x2, wfo, wao, wo, g2, wr, br, ltri):
    tm = TM_MG
    const = lambda i: (0, 0)
    resident = functools.partial(pl.BlockSpec, index_map=const, pipeline_mode=pl.Buffered(1))
    return pl.pallas_call(
        _merge_kernel,
        grid=(N_TOK // tm,),
        in_specs=[
            pl.BlockSpec((tm, FOURIER_WIDTH), lambda i: (i, 0)),
            pl.BlockSpec((tm, Q_WIDTH), lambda i: (i, 0)),
            pl.BlockSpec((tm, D_MODEL), lambda i: (i, 0)),
            pl.BlockSpec((tm, D_MODEL), lambda i: (i, 1)),
            pl.BlockSpec((tm, D_MODEL), lambda i: (i, 0)),
            resident((FOURIER_WIDTH, D_MODEL)),
            resident((Q_WIDTH, D_MODEL)),
            resident((D_MODEL, D_MODEL)),
            pl.BlockSpec((1, D_MODEL), const),
            pl.BlockSpec((D_MODEL, LANES), const),
            pl.BlockSpec((1, LANES), const),
            pl.BlockSpec((tm, tm), const),
        ],
        out_specs=[
            pl.BlockSpec((tm, D_MODEL), lambda i: (i, 0)),
            pl.BlockSpec((tm * SUBLANES, LANES), lambda i: (i, 0)),
            pl.BlockSpec((tm, LANES), lambda i: (i, 0)),
            pl.BlockSpec((8, LANES), const),
        ],
        out_shape=[
            jax.ShapeDtypeStruct((N_TOK, D_MODEL), F32),
            jax.ShapeDtypeStruct((N_TOK * SUBLANES, LANES), U32),
            jax.ShapeDtypeStruct((N_TOK, LANES), F32),
            jax.ShapeDtypeStruct((8, LANES), F32),
        ],
        scratch_shapes=[pltpu.VMEM((1, LANES), F32)],
        compiler_params=_cparams(("arbitrary",), 56),
        name="merge",
    )(yf, attn, gates, gates, x2, wfo, wao, wo, g2, wr, br, ltri)


def _expert_kernel(e_ref, live_ref, tok_cur_ref, tok_nxt_ref, dst_cur_ref,
                   h2p_hbm, wg_ref, bg_ref, wu_ref, bu_ref, wd_ref, bd_ref,
                   yslots_hbm, xu_scr, acc_scr, xs_scr, wgb, wub, wdb, gsem, ssem):
    del e_ref
    w = pl.program_id(0)
    f = pl.program_id(1)
    live = live_ref[w] > 0
    prev_live = (w > 0) & (live_ref[jnp.maximum(w - 1, 0)] > 0)
    cur = w & 1
    nxt = 1 - cur
    tile_rows = RT * SUBLANES

    def sub_rows(s):
        return pl.ds(pl.multiple_of(s * RT, RT), RT)

    def sub_tiles(slot, s):
        return xu_scr.at[slot, pl.ds(pl.multiple_of(s * tile_rows, tile_rows), tile_rows), :]

    def row_tile(row):
        return pl.ds(pl.multiple_of(row * SUBLANES, SUBLANES), SUBLANES)

    def gather_copy(tok_ref, slot, row):
        tok = tok_ref[0, 0, row]
        return pltpu.make_async_copy(h2p_hbm.at[row_tile(tok), :], xu_scr.at[slot, row_tile(row), :], gsem.at[slot])

    def scatter_copy(row):
        dst = dst_cur_ref[0, 0, row]
        return pltpu.make_async_copy(xu_scr.at[cur, row_tile(row), :], yslots_hbm.at[row_tile(dst), :], ssem)

    def for_all_rows(fn):
        def body(row, carry):
            fn(row)
            return carry
        lax.fori_loop(0, R_MAX, body, 0, unroll=8)

    def prefetch_next(s):
        base = s * RT + f * GATHER_PER_STEP
        for g in range(GATHER_PER_STEP):
            gather_copy(tok_nxt_ref, nxt, base + g).start(priority=g % 2)

    def cast_weights():
        wgb[...] = wg_ref[...].astype(BF16)
        wub[...] = wu_ref[...].astype(BF16)
        wdb[...] = wd_ref[...].astype(BF16)

    def stage_inputs(first_sub, n_sub, stage_row0):
        n_rows = n_sub * RT
        tiles = xu_scr.at[cur, pl.ds(pl.multiple_of(first_sub * tile_rows, tile_rows), n_sub * tile_rows), :]
        stage = xs_scr.at[pl.ds(stage_row0, n_rows), :]
        for j in range(SUBLANES):
            lo, hi = _unpack_halves(_load_row_tiles(tiles, j, n_rows))
            stage[:, j * LANES:(j + 1) * LANES] = lo.astype(BF16)
            stage[:, HALF + j * LANES:HALF + (j + 1) * LANES] = hi.astype(BF16)

    def activations(n_sub, stage_row0):
        x = xs_scr[pl.ds(stage_row0, n_sub * RT), :]
        g = _dot(x, wgb[...]) + bg_ref[...]
        u = _dot(x, wub[...]) + bu_ref[...]
        g = jnp.minimum(g, SWIGLU_LIMIT)
        u = jnp.clip(u, -SWIGLU_LIMIT, SWIGLU_LIMIT)
        act = (u + 1.0) * (g * (1.0 / (1.0 + jnp.exp(-SWIGLU_ALPHA * g))))
        return act.astype(BF16)

    def accumulate(j):
        n_rows = GROUP * RT
        rows = pl.ds(pl.multiple_of(j * n_rows, n_rows), n_rows)
        stage_inputs(j * GROUP, GROUP, 0)
        c = _dot(activations(GROUP, 0), wdb[...])
        acc_scr[rows, :] = jnp.where(f == 0, c, acc_scr[rows, :] + c)

    def finish(s):
        act = activations(1, (s % GROUP) * RT)
        rows = sub_rows(s)
        tiles = sub_tiles(cur, s)
        for c in range(HALF // OUT_CHUNK):
            lo_cols = slice(c * OUT_CHUNK, (c + 1) * OUT_CHUNK)
            hi_cols = slice(HALF + c * OUT_CHUNK, HALF + (c + 1) * OUT_CHUNK)
            y_lo = acc_scr[rows, lo_cols] + _dot(act, wdb[:, lo_cols]) + bd_ref[:, lo_cols]
            y_hi = acc_scr[rows, hi_cols] + _dot(act, wdb[:, hi_cols]) + bd_ref[:, hi_cols]
            words = _pack_halves(y_lo, y_hi)
            for jj in range(OUT_CHUNK // LANES):
                j = c * (OUT_CHUNK // LANES) + jj
                tiles[pl.ds(j, RT, stride=SUBLANES), :] = words[:, jj * LANES:(jj + 1) * LANES]

    @pl.when(f == 0)
    def _():
        @pl.when(w == 0)
        def _():
            def zero(s, carry):
                acc_scr[sub_rows(s), :] = jnp.zeros((RT, D_MODEL), F32)
                xu_scr[1, pl.ds(pl.multiple_of(s * tile_rows, tile_rows), tile_rows), :] = (
                    jnp.zeros((tile_rows, LANES), U32))
                return carry
            lax.fori_loop(0, N_SUB, zero, 0)
            pad_rows = pl.ds(TOP_K * N_TOK * SUBLANES, R_MAX * SUBLANES)
            fill = pltpu.make_async_copy(xu_scr.at[1], yslots_hbm.at[pad_rows, :], ssem)
            fill.start()
            fill.wait()
            for_all_rows(lambda row: gather_copy(tok_cur_ref, cur, row).start())

        @pl.when((w == 0) | prev_live)
        def _():
            first_tile = pl.ds(0, SUBLANES)
            row_in = pltpu.make_async_copy(h2p_hbm.at[first_tile, :], xu_scr.at[cur, first_tile, :], gsem.at[cur])
            for_all_rows(lambda row: row_in.wait())

        @pl.when(prev_live)
        def _():
            first_tile = pl.ds(0, SUBLANES)
            row_out = pltpu.make_async_copy(xu_scr.at[cur, first_tile, :], yslots_hbm.at[first_tile, :], ssem)
            for_all_rows(lambda row: row_out.wait())

    @pl.when(live & (f < NF - 1))
    def _():
        cast_weights()

        def group(j, carry):
            accumulate(j)
            for i in range(GROUP):
                prefetch_next(j * GROUP + i)
            return carry
        lax.fori_loop(0, N_SUB // GROUP, group, 0)

    @pl.when(live & (f == NF - 1))
    def _():
        cast_weights()
        stage_inputs(0, 1, 0)
        for s in range(N_SUB):
            finish(s)
            if s + 1 < N_SUB:
                stage_inputs(s + 1, 1, ((s + 1) % GROUP) * RT)
            for r in range(RT):
                scatter_copy(s * RT + r).start(priority=r % 2)
            prefetch_next(s)


def _experts(item_e, item_live, item_tok, item_dst, h2p, wg, bg, wu, bu, wd, bd):
    def w_in_map(w, f, e, lv):
        return (e[w], 0, jnp.where(lv[w] > 0, f, NF - 1))

    def w_down_map(w, f, e, lv):
        return (e[w], jnp.where(lv[w] > 0, f, NF - 1), 0)

    def b_down_map(w, f, e, lv):
        return (e[w], 0, 0)

    grid_spec = pltpu.PrefetchScalarGridSpec(
        num_scalar_prefetch=2,
        grid=(N_ITEMS, NF),
        in_specs=[
            pl.BlockSpec((1, 1, R_MAX), lambda w, f, e, lv: (w, 0, 0), memory_space=pltpu.SMEM),
            pl.BlockSpec((1, 1, R_MAX), lambda w, f, e, lv: (jnp.minimum(w + 1, N_ITEMS - 1), 0, 0),
                         memory_space=pltpu.SMEM),
            pl.BlockSpec((1, 1, R_MAX), lambda w, f, e, lv: (w, 0, 0), memory_space=pltpu.SMEM),
            pl.BlockSpec(memory_space=pl.ANY),
            pl.BlockSpec((None, D_MODEL, FC), w_in_map),
            pl.BlockSpec((None, 1, FC), w_in_map),
            pl.BlockSpec((None, D_MODEL, FC), w_in_map),
            pl.BlockSpec((None, 1, FC), w_in_map),
            pl.BlockSpec((None, FC, D_MODEL), w_down_map),
            pl.BlockSpec((None, 1, D_MODEL), b_down_map),
        ],
        out_specs=pl.BlockSpec(memory_space=pl.ANY),
        scratch_shapes=[
            pltpu.VMEM((2, R_MAX * SUBLANES, LANES), U32),
            pltpu.VMEM((R_MAX, D_MODEL), F32),
            pltpu.VMEM((GROUP * RT, D_MODEL), BF16),
            pltpu.VMEM((D_MODEL, FC), BF16),
            pltpu.VMEM((D_MODEL, FC), BF16),
            pltpu.VMEM((FC, D_MODEL), BF16),
            pltpu.SemaphoreType.DMA((2,)),
            pltpu.SemaphoreType.DMA(()),
        ],
    )
    return pl.pallas_call(
        _expert_kernel,
        grid_spec=grid_spec,
        out_shape=jax.ShapeDtypeStruct((SLOT_ROWS * SUBLANES, LANES), U32),
        compiler_params=_cparams(("arbitrary", "arbitrary"), 60),
        name="experts",
    )(item_e, item_live, item_tok, item_tok, item_dst, h2p, wg, bg, wu, bu, wd, bd)


def _combine_kernel(meta_ref, x1_ref, y0_ref, y1_ref, y2_ref, y3_ref, out_ref):
    gates = [meta_ref[:, 2 * TOP_K + k:2 * TOP_K + k + 1] for k in range(TOP_K)]
    for j in range(SUBLANES):
        lo_cols = slice(j * LANES, (j + 1) * LANES)
        hi_cols = slice(HALF + j * LANES, HALF + (j + 1) * LANES)
        lo_acc = x1_ref[:, lo_cols]
        hi_acc = x1_ref[:, hi_cols]
        for k, y_ref in enumerate((y0_ref, y1_ref, y2_ref, y3_ref)):
            lo, hi = _unpack_halves(_load_row_tiles(y_ref, j, TM_CB))
            lo_acc = lo_acc + gates[k] * lo
            hi_acc = hi_acc + gates[k] * hi
        out_ref[:, lo_cols] = lo_acc
        out_ref[:, hi_cols] = hi_acc


def _combine(meta, x1, y_slots):
    tm = TM_CB
    nblk = N_TOK // tm
    slot_specs = [pl.BlockSpec((tm * SUBLANES, LANES), lambda i, k=k: (k * nblk + i, 0)) for k in range(TOP_K)]
    return pl.pallas_call(
        _combine_kernel,
        grid=(nblk,),
        in_specs=[
            pl.BlockSpec((tm, LANES), lambda i: (i, 0)),
            pl.BlockSpec((tm, D_MODEL), lambda i: (i, 0)),
        ] + slot_specs,
        out_specs=pl.BlockSpec((tm, D_MODEL), lambda i: (i, 0)),
        out_shape=jax.ShapeDtypeStruct((N_TOK, D_MODEL), F32),
        compiler_params=_cparams(("arbitrary",), 32),
        name="combine",
    )(meta, x1, y_slots, y_slots, y_slots, y_slots)


def _dft_tables():
    r = DFT_RADIX
    k = np.arange(r)
    e64 = np.exp(2j * np.pi * np.outer(k, k) / r)
    e4096 = np.exp(2j * np.pi * np.outer(k, k) / SEQ)
    s = np.arange(SEQ)
    f1 = e64[:, s % r]
    g = e64[:, s // r] * e4096[:, s % r] / math.sqrt(SEQ)
    c = np.arange(FOURIER_GROUP_DIM)
    ang = 2.0 * np.pi * np.outer(c, c) / FOURIER_GROUP_DIM
    scale = 1.0 / math.sqrt(FOURIER_GROUP_DIM)
    as32 = lambda a: jnp.asarray(a.astype(np.float32))
    return (as32(f1.real), as32(f1.imag), as32(g.real), as32(g.imag),
            jnp.asarray((np.cos(ang) * scale).astype(np.float32)).astype(BF16),
            jnp.asarray((np.sin(ang) * scale).astype(np.float32)).astype(BF16))


def _head_selectors():
    i = np.arange(LANES)[:, None]
    j = np.arange(LANES)[None, :]
    out = np.zeros((4, LANES, LANES), np.float32)
    for src in range(2):
        for dst in range(2):
            out[2 * src + dst] = (i - HEAD_DIM * src == j - HEAD_DIM * dst) & (j // HEAD_DIM == dst)
    return jnp.asarray(out).astype(BF16)


def kernel(x, norm1_g, w_in, b_branch_gate, q_norm_g, k_norm_g, attn_sink, w_fourier_out, w_attn_out, w_o,
           norm2_g, w_router, b_router, w_gate_e, b_gate_e, w_up_e, b_up_e, w_down_e, b_down_e):
    b, s, d = x.shape
    assert (b, s, d) == (BATCH, SEQ, D_MODEL) and norm1_g.shape[0] == 1
    x2 = x.reshape(N_TOK, D_MODEL)

    heads_per_tile = TN_IN // HEAD_DIM
    ones_bd = jnp.asarray((np.arange(TN_IN)[:, None] // HEAD_DIM == np.arange(TN_IN)[None, :] // HEAD_DIM)
                          .astype(np.float32)).astype(BF16)
    u_f, q, kv, gates = _in_proj(
        x2, norm1_g[0].reshape(1, D_MODEL),
        w_in[0].astype(BF16).reshape(D_MODEL, IN_WIDTH // TN_IN, TN_IN).transpose(1, 0, 2),
        b_branch_gate[0].reshape(1, 2 * D_MODEL),
        jnp.tile(q_norm_g[0], heads_per_tile).reshape(1, TN_IN),
        jnp.tile(k_norm_g[0], heads_per_tile).reshape(1, TN_IN), ones_bd)
    y_f = _fourier(u_f, *_dft_tables())
    attn = _attention(attn_sink[0], q, kv, _head_selectors())

    wr = jnp.zeros((D_MODEL, LANES), BF16).at[:, :N_EXPERTS].set(w_router[0].astype(BF16))
    br = jnp.full((1, LANES), NEG_INF, F32).at[0, :N_EXPERTS].set(b_router[0])
    ltri = jnp.asarray(np.tril(np.ones((TM_MG, TM_MG), np.float32), -1)).astype(BF16)
    x1, h2p, meta, cnt = _merge(y_f, attn, gates, x2, w_fourier_out[0].astype(BF16), w_attn_out[0].astype(BF16),
                                w_o[0].astype(BF16), norm2_g[0].reshape(1, D_MODEL), wr, br, ltri)

    tables = _routing_tables(cnt[0, :N_EXPERTS].astype(jnp.int32), meta[:, 0:TOP_K].astype(jnp.int32),
                             meta[:, TOP_K:2 * TOP_K].astype(jnp.int32))
    y_slots = _experts(*tables, h2p,
                       w_gate_e[0], b_gate_e[0].reshape(N_EXPERTS, 1, D_EXPERT),
                       w_up_e[0], b_up_e[0].reshape(N_EXPERTS, 1, D_EXPERT),
                       w_down_e[0], b_down_e[0].reshape(N_EXPERTS, 1, D_MODEL))
    out = _combine(meta, x1, y_slots)
    return out.reshape(BATCH, SEQ, D_MODEL)


def _routing_tables(counts, top_idx, rank):
    padded = ((counts + RT - 1) // RT) * RT
    pend = jnp.cumsum(padded)
    pstart = pend - padded
    dest = pstart[top_idx] + rank
    n_assign = TOP_K * N_TOK
    row_assign = jnp.full((ROWS,), -1, jnp.int32).at[dest.reshape(n_assign)].set(
        jnp.arange(n_assign, dtype=jnp.int32), unique_indices=True)

    items_per_e = (padded + R_MAX - 1) // R_MAX
    it_end = jnp.cumsum(items_per_e)
    it_start = it_end - items_per_e
    total_items = it_end[-1]
    wi = jnp.arange(N_ITEMS, dtype=jnp.int32)
    e_w = jnp.minimum(jnp.searchsorted(it_end, wi, side="right"), N_EXPERTS - 1).astype(jnp.int32)
    j_w = wi - it_start[e_w]
    live = wi < total_items
    rows_w = jnp.clip(padded[e_w] - j_w * R_MAX, 0, R_MAX)
    e_last = e_w[jnp.maximum(total_items - 1, 0)]
    item_e = jnp.where(live, e_w, e_last).astype(jnp.int32)
    item_row0 = jnp.where(live, pstart[e_w] + j_w * R_MAX, 0).astype(jnp.int32)
    item_nsub = jnp.where(live, rows_w // RT, 0).astype(jnp.int32)
    item_live = live.astype(jnp.int32)
    r = jnp.arange(R_MAX, dtype=jnp.int32)
    src = jnp.minimum(item_row0[:, None] + r[None, :], ROWS - 1)
    item_rows = jnp.where(r[None, :] < (item_nsub * RT)[:, None], row_assign[src], -1)
    item_tok = (jnp.maximum(item_rows, 0) // TOP_K).reshape(N_ITEMS, 1, R_MAX)
    item_dst = jnp.where(item_rows >= 0, (item_rows % TOP_K) * N_TOK + item_rows // TOP_K,
                         TOP_K * N_TOK + r[None, :]).reshape(N_ITEMS, 1, R_MAX)
    return item_e, item_live, item_tok, item_dst
```

```python
import functools
import math

import jax
import jax.numpy as jnp
import numpy as np
from jax import lax
from jax.experimental import pallas as pl
from jax.experimental.pallas import tpu as pltpu

D_MODEL = 2048
BATCH = 4
SEQ = 4096
N_TOK = BATCH * SEQ
N_Q_HEADS = 16
N_KV_HEADS = 4
HEAD_DIM = 64
Q_WIDTH = N_Q_HEADS * HEAD_DIM
KV_WIDTH = N_KV_HEADS * HEAD_DIM
WINDOW = 128
N_FOURIER_GROUPS = 4
FOURIER_GROUP_DIM = 256
FOURIER_WIDTH = N_FOURIER_GROUPS * FOURIER_GROUP_DIM
IN_WIDTH = FOURIER_WIDTH + Q_WIDTH + 2 * KV_WIDTH + 2 * D_MODEL
N_EXPERTS = 32
TOP_K = 4
D_EXPERT = D_MODEL
SWIGLU_LIMIT = 7.0
SWIGLU_ALPHA = 1.702
RMS_EPS = 1e-6
NEG_INF = -1e30

F32 = jnp.float32
BF16 = jnp.bfloat16
U32 = jnp.uint32

V7X_VMEM_BYTES = 64 * 1024 * 1024
LANES = 128
SUBLANES = 8
MIB = 1024 * 1024

TM_IN = 512
TN_IN = 512
NORM_ROWS = 64
TT_F = 512
DFT_RADIX = 64
TQ = 512
QB = 128
TM_MG = 256
RT = 256
R_MAX = 9 * RT
FC = 256
NF = D_EXPERT // FC
ROWS = TOP_K * N_TOK + N_EXPERTS * RT
N_ITEMS = (TOP_K * N_TOK + N_EXPERTS * (RT - 1) + N_EXPERTS * (R_MAX - RT)) // R_MAX + 1
TM_CB = 256
HALF = D_MODEL // 2
N_SUB = R_MAX // RT
GROUP = 3
OUT_CHUNK = 256
GATHER_PER_STEP = RT // NF
SLOT_ROWS = TOP_K * N_TOK + R_MAX


def _cparams(sem, vmem_mib):
    return pltpu.CompilerParams(dimension_semantics=sem, vmem_limit_bytes=vmem_mib * MIB)


def _dot(a, b):
    return jnp.dot(a, b, preferred_element_type=F32)


def _pack_halves(lo, hi):
    lo_bits = pltpu.bitcast(lo.astype(BF16).astype(F32), U32)
    hi_bits = pltpu.bitcast(hi.astype(BF16).astype(F32), U32)
    return (hi_bits & jnp.uint32(0xFFFF0000)) | (lo_bits >> jnp.uint32(16))


def _unpack_halves(words):
    lo = pltpu.bitcast(words << jnp.uint32(16), F32)
    hi = pltpu.bitcast(words & jnp.uint32(0xFFFF0000), F32)
    return lo, hi


def _store_row_tiles(tiles_ref, words):
    n_rows = words.shape[0]
    for j in range(SUBLANES):
        tiles_ref[pl.ds(j, n_rows, stride=SUBLANES), :] = words[:, j * LANES:(j + 1) * LANES]


def _load_row_tiles(tiles_ref, j, n_rows):
    return tiles_ref[pl.ds(j, n_rows, stride=SUBLANES), :]


def _in_proj_kernel(x_ref, g1_ref, w_ref, b_ref, qg_ref, kg_ref, ones_ref,
                    u_ref, q_ref, kv_ref, gate_ref, h_scr):
    def body(c, carry):
        rows = pl.ds(pl.multiple_of(c * NORM_ROWS, NORM_ROWS), NORM_ROWS)
        x = x_ref[rows, :]
        ms = jnp.mean(x * x, axis=-1, keepdims=True)
        h_scr[rows, :] = (x * lax.rsqrt(ms + RMS_EPS) * g1_ref[...]).astype(BF16)
        return carry
    lax.fori_loop(0, TM_IN // NORM_ROWS, body, 0)

    def head_norm(a, gain):
        sq = a * a
        hi = sq.astype(BF16)
        lo = (sq - hi.astype(F32)).astype(BF16)
        ssq = _dot(hi, ones_ref[...]) + _dot(lo, ones_ref[...])
        return a * lax.rsqrt(ssq * (1.0 / HEAD_DIM) + RMS_EPS) * gain

    h = h_scr[...]
    for c in range(IN_WIDTH // TN_IN):
        col0 = c * TN_IN
        acc = _dot(h, w_ref[:, col0:col0 + TN_IN])
        if col0 < FOURIER_WIDTH:
            u_ref[:, col0:col0 + TN_IN] = acc.astype(BF16)
        elif col0 < FOURIER_WIDTH + Q_WIDTH:
            o = col0 - FOURIER_WIDTH
            q_ref[:, o:o + TN_IN] = (head_norm(acc, qg_ref[...]) * (HEAD_DIM ** -0.5)).astype(BF16)
        elif col0 < FOURIER_WIDTH + Q_WIDTH + 2 * KV_WIDTH:
            lane = lax.broadcasted_iota(jnp.int32, acc.shape, 1)
            kv_ref[...] = jnp.where(lane < KV_WIDTH, head_norm(acc, kg_ref[...]), acc).astype(BF16)
        else:
            o = col0 - (FOURIER_WIDTH + Q_WIDTH + 2 * KV_WIDTH)
            z = acc + b_ref[:, o:o + TN_IN]
            gate_ref[:, o:o + TN_IN] = (1.0 / (1.0 + jnp.exp(-z))).astype(BF16)


def _in_proj(x2, g1, w_bf, bias, qg_t, kg_t, ones_bd):
    const = lambda i: (0, 0)
    row = lambda i: (i, 0)
    return pl.pallas_call(
        _in_proj_kernel,
        grid=(N_TOK // TM_IN,),
        in_specs=[
            pl.BlockSpec((TM_IN, D_MODEL), row),
            pl.BlockSpec((1, D_MODEL), const),
            pl.BlockSpec((D_MODEL, IN_WIDTH), const, pipeline_mode=pl.Buffered(1)),
            pl.BlockSpec((1, 2 * D_MODEL), const),
            pl.BlockSpec((1, TN_IN), const),
            pl.BlockSpec((1, TN_IN), const),
            pl.BlockSpec((TN_IN, TN_IN), const),
        ],
        out_specs=[
            pl.BlockSpec((TM_IN, FOURIER_WIDTH), row),
            pl.BlockSpec((TM_IN, Q_WIDTH), row),
            pl.BlockSpec((TM_IN, 2 * KV_WIDTH), row),
            pl.BlockSpec((TM_IN, 2 * D_MODEL), row),
        ],
        out_shape=[
            jax.ShapeDtypeStruct((N_TOK, FOURIER_WIDTH), BF16),
            jax.ShapeDtypeStruct((N_TOK, Q_WIDTH), BF16),
            jax.ShapeDtypeStruct((N_TOK, 2 * KV_WIDTH), BF16),
            jax.ShapeDtypeStruct((N_TOK, 2 * D_MODEL), BF16),
        ],
        scratch_shapes=[pltpu.VMEM((TM_IN, D_MODEL), BF16)],
        compiler_params=_cparams(("arbitrary",), 60),
        name="in_proj",
    )(x2, g1, w_bf, bias, qg_t, kg_t, ones_bd)


def _fourier_kernel(u_ref, f1r_ref, f1i_ref, gr_ref, gi_ref, cc_ref, sc_ref, y_ref, cs_scr, ss_scr):
    b = pl.program_id(1)

    @pl.when(b == 0)
    def _():
        gr = gr_ref[...]
        gi = gi_ref[...]
        for a in range(TT_F // DFT_RADIX):
            f1r = f1r_ref[a:a + 1, :]
            f1i = f1i_ref[a:a + 1, :]
            rows = slice(a * DFT_RADIX, (a + 1) * DFT_RADIX)
            cs_scr[rows, :] = (f1r * gr - f1i * gi).astype(BF16)
            ss_scr[rows, :] = (f1r * gi + f1i * gr).astype(BF16)

    u = u_ref[...]
    a_seq = _dot(cs_scr[...], u)
    b_seq = _dot(ss_scr[...], u)
    for g in range(N_FOURIER_GROUPS):
        cols = slice(g * FOURIER_GROUP_DIM, (g + 1) * FOURIER_GROUP_DIM)
        y = _dot(a_seq[:, cols].astype(BF16), cc_ref[...]) - _dot(b_seq[:, cols].astype(BF16), sc_ref[...])
        y_ref[:, cols] = y.astype(BF16)


def _fourier(u, f1r, f1i, gr, gi, cc, sc):
    nt = SEQ // TT_F
    f1_rows = TT_F // DFT_RADIX
    return pl.pallas_call(
        _fourier_kernel,
        grid=(nt, BATCH),
        in_specs=[
            pl.BlockSpec((SEQ, FOURIER_WIDTH), lambda t, b: (b, 0)),
            pl.BlockSpec((f1_rows, SEQ), lambda t, b: (t, 0)),
            pl.BlockSpec((f1_rows, SEQ), lambda t, b: (t, 0)),
            pl.BlockSpec((DFT_RADIX, SEQ), lambda t, b: (0, 0)),
            pl.BlockSpec((DFT_RADIX, SEQ), lambda t, b: (0, 0)),
            pl.BlockSpec((FOURIER_GROUP_DIM, FOURIER_GROUP_DIM), lambda t, b: (0, 0)),
            pl.BlockSpec((FOURIER_GROUP_DIM, FOURIER_GROUP_DIM), lambda t, b: (0, 0)),
        ],
        out_specs=pl.BlockSpec((TT_F, FOURIER_WIDTH), lambda t, b: (b * nt + t, 0)),
        out_shape=jax.ShapeDtypeStruct((N_TOK, FOURIER_WIDTH), BF16),
        scratch_shapes=[pltpu.VMEM((TT_F, SEQ), BF16), pltpu.VMEM((TT_F, SEQ), BF16)],
        compiler_params=_cparams(("arbitrary", "arbitrary"), 56),
        name="fourier",
    )(u, f1r, f1i, gr, gi, cc, sc)


def _alibi_slope(h):
    return float(2.0 ** (-8.0 * (h + 1) / N_Q_HEADS))


def _attention_kernel(sink_ref, q_ref, prev_ref, cur_ref, next_ref, sel_ref, o_ref, kz_scr, vz_scr):
    t = pl.program_id(1)
    band = jnp.concatenate([prev_ref[...], cur_ref[...], next_ref[...]], axis=0)
    for g in range(N_KV_HEADS):
        c, half = divmod(g, 2)
        kcol = band[:, c * LANES:(c + 1) * LANES]
        vcol = band[:, KV_WIDTH + c * LANES:KV_WIDTH + (c + 1) * LANES]
        for d in range(2):
            sel = sel_ref[half * 2 + d]
            kz_scr[2 * g + d] = jnp.transpose(_dot(kcol, sel)).astype(BF16)
            vz_scr[2 * g + d] = _dot(vcol, sel).astype(BF16)

    row = lax.broadcasted_iota(jnp.int32, (2 * QB, 3 * QB), 0)
    col = lax.broadcasted_iota(jnp.int32, (2 * QB, 3 * QB), 1)
    absrel = jnp.abs((row & (QB - 1)) - col + QB)
    absrel_f = absrel.astype(F32)
    top = lax.broadcasted_iota(jnp.int32, (2 * QB, 1), 0) < QB
    for i in range(TQ // QB):
        q_rows = slice(i * QB, (i + 1) * QB)
        k_rows = slice(i * QB, (i + 3) * QB)
        kpos = t * TQ + (i - 1) * QB + col
        mask = (absrel <= WINDOW) & (kpos >= 0) & (kpos < SEQ)
        for g in range(N_KV_HEADS):
            cols = [slice((2 * g + p) * LANES, (2 * g + p + 1) * LANES) for p in range(2)]
            qq = jnp.concatenate([q_ref[q_rows, cols[0]], q_ref[q_rows, cols[1]]], axis=0)
            out = jnp.zeros((2 * QB, LANES), F32)
            for d in range(2):
                h_top, h_bot = 4 * g + d, 4 * g + 2 + d
                slope = jnp.where(top, _alibi_slope(h_top), _alibi_slope(h_bot))
                sink = jnp.where(top, sink_ref[h_top], sink_ref[h_bot])
                s = _dot(qq, kz_scr[2 * g + d, :, k_rows])
                s = jnp.where(mask, s - slope * absrel_f, NEG_INF)
                m = jnp.maximum(jnp.max(s, axis=-1, keepdims=True), sink)
                pr = jnp.exp(s - m)
                den = jnp.sum(pr, axis=-1, keepdims=True) + jnp.exp(sink - m)
                out = out + _dot(pr.astype(BF16), vz_scr[2 * g + d, k_rows, :]) / den
            o_ref[q_rows, cols[0]] = out[:QB].astype(BF16)
            o_ref[q_rows, cols[1]] = out[QB:].astype(BF16)


def _attention(sink, q, kv, sel):
    nt = SEQ // TQ
    nb = SEQ // QB
    r = TQ // QB
    grid_spec = pltpu.PrefetchScalarGridSpec(
        num_scalar_prefetch=1,
        grid=(BATCH, nt),
        in_specs=[
            pl.BlockSpec((TQ, Q_WIDTH), lambda b, t, s: (b * nt + t, 0)),
            pl.BlockSpec((QB, 2 * KV_WIDTH), lambda b, t, s: (b * nb + jnp.maximum(t * r - 1, 0), 0)),
            pl.BlockSpec((TQ, 2 * KV_WIDTH), lambda b, t, s: (b * nt + t, 0)),
            pl.BlockSpec((QB, 2 * KV_WIDTH), lambda b, t, s: (b * nb + jnp.minimum(t * r + r, nb - 1), 0)),
            pl.BlockSpec((4, LANES, LANES), lambda b, t, s: (0, 0, 0)),
        ],
        out_specs=pl.BlockSpec((TQ, Q_WIDTH), lambda b, t, s: (b * nt + t, 0)),
        scratch_shapes=[pltpu.VMEM((2 * N_KV_HEADS, LANES, TQ + 2 * QB), BF16),
                        pltpu.VMEM((2 * N_KV_HEADS, TQ + 2 * QB, LANES), BF16)],
    )
    return pl.pallas_call(
        _attention_kernel,
        grid_spec=grid_spec,
        out_shape=jax.ShapeDtypeStruct((N_TOK, Q_WIDTH), BF16),
        compiler_params=_cparams(("arbitrary", "arbitrary"), 32),
        name="attention",
    )(sink, q, kv, kv, kv, sel)


def _merge_kernel(yf_ref, at_ref, gf_ref, ga_ref, x_ref, wfo_ref, wao_ref, wo_ref, g2_ref, wr_ref, br_ref,
                  ltri_ref, x1_ref, h2p_ref, meta_ref, cnt_ref, base_scr):
    i = pl.program_id(0)

    @pl.when(i == 0)
    def _():
        base_scr[...] = jnp.zeros_like(base_scr)

    yf = _dot(yf_ref[...], wfo_ref[...])
    ya = _dot(at_ref[...], wao_ref[...])
    merged = gf_ref[...].astype(F32) * yf + ga_ref[...].astype(F32) * ya
    x1 = x_ref[...] + _dot(merged.astype(BF16), wo_ref[...])
    x1_ref[...] = x1
    ms = jnp.mean(x1 * x1, axis=-1, keepdims=True)
    h2 = x1 * lax.rsqrt(ms + RMS_EPS) * g2_ref[...]
    _store_row_tiles(h2p_ref, _pack_halves(h2[:, :HALF], h2[:, HALF:]))
    logits = _dot(h2.astype(BF16), wr_ref[...]) + br_ref[...]

    lane = lax.broadcasted_iota(jnp.int32, logits.shape, 1)
    lane_f = lane.astype(F32)
    vals = logits
    top_v, top_i, onehots = [], [], []
    for _ in range(TOP_K):
        m = jnp.max(vals, axis=-1, keepdims=True)
        idx = jnp.min(jnp.where(vals == m, lane_f, float(LANES)), axis=-1, keepdims=True)
        oh = lane_f == idx
        top_v.append(m)
        top_i.append(idx)
        onehots.append(oh)
        vals = jnp.where(oh, -jnp.inf, vals)
    exps = [jnp.exp(v - top_v[0]) for v in top_v]
    den = exps[0] + exps[1] + exps[2] + exps[3]
    gates = [e / den for e in exps]

    cnt = sum(oh.astype(F32) for oh in onehots)
    prefix = _dot(ltri_ref[...], cnt.astype(BF16))
    tot = base_scr[...] + prefix
    ranks = [jnp.sum(jnp.where(oh, tot, 0.0), axis=-1, keepdims=True) for oh in onehots]
    base_scr[...] = base_scr[...] + jnp.sum(cnt, axis=0, keepdims=True)
    cnt_ref[...] = jnp.broadcast_to(base_scr[...], cnt_ref.shape)

    meta = jnp.zeros(logits.shape, F32)
    for k in range(TOP_K):
        meta = jnp.where(lane == k, top_i[k], meta)
        meta = jnp.where(lane == TOP_K + k, ranks[k], meta)
        meta = jnp.where(lane == 2 * TOP_K + k, gates[k], meta)
    meta_ref[...] = meta


def _merge(yf, attn, gates, x2, wfo, wao, wo, g2, wr, br, ltri):
    tm = TM_MG
    const = lambda i: (0, 0)
    resident = functools.partial(pl.BlockSpec, index_map=const, pipeline_mode=pl.Buffered(1))
    return pl.pallas_call(
        _merge_kernel,
        grid=(N_TOK // tm,),
        in_specs=[
            pl.BlockSpec((tm, FOURIER_WIDTH), lambda i: (i, 0)),
            pl.BlockSpec((tm, Q_WIDTH), lambda i: (i, 0)),
            pl.BlockSpec((tm, D_MODEL), lambda i: (i, 0)),
            pl.BlockSpec((tm, D_MODEL), lambda i: (i, 1)),
            pl.BlockSpec((tm, D_MODEL), lambda i: (i, 0)),
            resident((FOURIER_WIDTH, D_MODEL)),
            resident((Q_WIDTH, D_MODEL)),
            resident((D_MODEL, D_MODEL)),
            pl.BlockSpec((1, D_MODEL), const),
            pl.BlockSpec((D_MODEL, LANES), const),
            pl.BlockSpec((1, LANES), const),
            pl.BlockSpec((tm, tm), const),
        ],
        out_specs=[
            pl.BlockSpec((tm, D_MODEL), lambda i: (i, 0)),
            pl.BlockSpec((tm * SUBLANES, LANES), lambda i: (i, 0)),
            pl.BlockSpec((tm, LANES), lambda i: (i, 0)),
            pl.BlockSpec((8, LANES), const),
        ],
        out_shape=[
            jax.ShapeDtypeStruct((N_TOK, D_MODEL), F32),
            jax.ShapeDtypeStruct((N_TOK * SUBLANES, LANES), U32),
            jax.ShapeDtypeStruct((N_TOK, LANES), F32),
            jax.ShapeDtypeStruct((8, LANES), F32),
        ],
        scratch_shapes=[pltpu.VMEM((1, LANES), F32)],
        compiler_params=_cparams(("arbitrary",), 56),
        name="merge",
    )(yf, attn, gates, gates, x2, wfo, wao, wo, g2, wr, br, ltri)


def _expert_kernel(e_ref, live_ref, tok_cur_ref, tok_nxt_ref, dst_cur_ref,
                   h2p_hbm, wg_ref, bg_ref, wu_ref, bu_ref, wd_ref, bd_ref,
                   yslots_hbm, xu_scr, acc_scr, xs_scr, wgb, wub, wdb, gsem, ssem):
    del e_ref
    w = pl.program_id(0)
    f = pl.program_id(1)
    live = live_ref[w] > 0
    prev_live = (w > 0) & (live_ref[jnp.maximum(w - 1, 0)] > 0)
    cur = w & 1
    nxt = 1 - cur
    tile_rows = RT * SUBLANES

    def sub_rows(s):
        return pl.ds(pl.multiple_of(s * RT, RT), RT)

    def sub_tiles(slot, s):
        return xu_scr.at[slot, pl.ds(pl.multiple_of(s * tile_rows, tile_rows), tile_rows), :]

    def row_tile(row):
        return pl.ds(pl.multiple_of(row * SUBLANES, SUBLANES), SUBLANES)

    def gather_copy(tok_ref, slot, row):
        tok = tok_ref[0, 0, row]
        return pltpu.make_async_copy(h2p_hbm.at[row_tile(tok), :], xu_scr.at[slot, row_tile(row), :], gsem.at[slot])

    def scatter_copy(row):
        dst = dst_cur_ref[0, 0, row]
        return pltpu.make_async_copy(xu_scr.at[cur, row_tile(row), :], yslots_hbm.at[row_tile(dst), :], ssem)

    def for_all_rows(fn):
        def body(row, carry):
            fn(row)
            return carry
        lax.fori_loop(0, R_MAX, body, 0, unroll=8)

    def prefetch_next(s):
        base = s * RT + f * GATHER_PER_STEP
        for g in range(GATHER_PER_STEP):
            gather_copy(tok_nxt_ref, nxt, base + g).start(priority=g % 2)

    def cast_weights():
        wgb[...] = wg_ref[...].astype(BF16)
        wub[...] = wu_ref[...].astype(BF16)
        wdb[...] = wd_ref[...].astype(BF16)

    def stage_inputs(first_sub, n_sub, stage_row0):
        n_rows = n_sub * RT
        tiles = xu_scr.at[cur, pl.ds(pl.multiple_of(first_sub * tile_rows, tile_rows), n_sub * tile_rows), :]
        stage = xs_scr.at[pl.ds(stage_row0, n_rows), :]
        for j in range(SUBLANES):
            lo, hi = _unpack_halves(_load_row_tiles(tiles, j, n_rows))
            stage[:, j * LANES:(j + 1) * LANES] = lo.astype(BF16)
            stage[:, HALF + j * LANES:HALF + (j + 1) * LANES] = hi.astype(BF16)

    def activations(n_sub, stage_row0):
        x = xs_scr[pl.ds(stage_row0, n_sub * RT), :]
        g = _dot(x, wgb[...]) + bg_ref[...]
        u = _dot(x, wub[...]) + bu_ref[...]
        g = jnp.minimum(g, SWIGLU_LIMIT)
        u = jnp.clip(u, -SWIGLU_LIMIT, SWIGLU_LIMIT)
        act = (u + 1.0) * (g * (1.0 / (1.0 + jnp.exp(-SWIGLU_ALPHA * g))))
        return act.astype(BF16)

    def accumulate(j):
        n_rows = GROUP * RT
        rows = pl.ds(pl.multiple_of(j * n_rows, n_rows), n_rows)
        stage_inputs(j * GROUP, GROUP, 0)
        c = _dot(activations(GROUP, 0), wdb[...])
        acc_scr[rows, :] = jnp.where(f == 0, c, acc_scr[rows, :] + c)

    def finish(s):
        act = activations(1, (s % GROUP) * RT)
        rows = sub_rows(s)
        tiles = sub_tiles(cur, s)
        for c in range(HALF // OUT_CHUNK):
            lo_cols = slice(c * OUT_CHUNK, (c + 1) * OUT_CHUNK)
            hi_cols = slice(HALF + c * OUT_CHUNK, HALF + (c + 1) * OUT_CHUNK)
            y_lo = acc_scr[rows, lo_cols] + _dot(act, wdb[:, lo_cols]) + bd_ref[:, lo_cols]
            y_hi = acc_scr[rows, hi_cols] + _dot(act, wdb[:, hi_cols]) + bd_ref[:, hi_cols]
            words = _pack_halves(y_lo, y_hi)
            for jj in range(OUT_CHUNK // LANES):
                j = c * (OUT_CHUNK // LANES) + jj
                tiles[pl.ds(j, RT, stride=SUBLANES), :] = words[:, jj * LANES:(jj + 1) * LANES]

    @pl.when(f == 0)
    def _():
        @pl.when(w == 0)
        def _():
            def zero(s, carry):
                acc_scr[sub_rows(s), :] = jnp.zeros((RT, D_MODEL), F32)
                xu_scr[1, pl.ds(pl.multiple_of(s * tile_rows, tile_rows), tile_rows), :] = (
                    jnp.zeros((tile_rows, LANES), U32))
                return carry
            lax.fori_loop(0, N_SUB, zero, 0)
            pad_rows = pl.ds(TOP_K * N_TOK * SUBLANES, R_MAX * SUBLANES)
            fill = pltpu.make_async_copy(xu_scr.at[1], yslots_hbm.at[pad_rows, :], ssem)
            fill.start()
            fill.wait()
            for_all_rows(lambda row: gather_copy(tok_cur_ref, cur, row).start())

        @pl.when((w == 0) | prev_live)
        def _():
            first_tile = pl.ds(0, SUBLANES)
            row_in = pltpu.make_async_copy(h2p_hbm.at[first_tile, :], xu_scr.at[cur, first_tile, :], gsem.at[cur])
            for_all_rows(lambda row: row_in.wait())

        @pl.when(prev_live)
        def _():
            first_tile = pl.ds(0, SUBLANES)
            row_out = pltpu.make_async_copy(xu_scr.at[cur, first_tile, :], yslots_hbm.at[first_tile, :], ssem)
            for_all_rows(lambda row: row_out.wait())

    @pl.when(live & (f < NF - 1))
    def _():
        cast_weights()

        def group(j, carry):
            accumulate(j)
            for i in range(GROUP):
                prefetch_next(j * GROUP + i)
            return carry
        lax.fori_loop(0, N_SUB // GROUP, group, 0)

    @pl.when(live & (f == NF - 1))
    def _():
        cast_weights()
        stage_inputs(0, 1, 0)
        for s in range(N_SUB):
            finish(s)
            if s + 1 < N_SUB:
                stage_inputs(s + 1, 1, ((s + 1) % GROUP) * RT)
            for r in range(RT):
                scatter_copy(s * RT + r).start(priority=r % 2)
            prefetch_next(s)


def _experts(item_e, item_live, item_tok, item_dst, h2p, wg, bg, wu, bu, wd, bd):
    def w_in_map(w, f, e, lv):
        return (e[w], 0, jnp.where(lv[w] > 0, f, NF - 1))

    def w_down_map(w, f, e, lv):
        return (e[w], jnp.where(lv[w] > 0, f, NF - 1), 0)

    def b_down_map(w, f, e, lv):
        return (e[w], 0, 0)

    grid_spec = pltpu.PrefetchScalarGridSpec(
        num_scalar_prefetch=2,
        grid=(N_ITEMS, NF),
        in_specs=[
            pl.BlockSpec((1, 1, R_MAX), lambda w, f, e, lv: (w, 0, 0), memory_space=pltpu.SMEM),
            pl.BlockSpec((1, 1, R_MAX), lambda w, f, e, lv: (jnp.minimum(w + 1, N_ITEMS - 1), 0, 0),
                         memory_space=pltpu.SMEM),
            pl.BlockSpec((1, 1, R_MAX), lambda w, f, e, lv: (w, 0, 0), memory_space=pltpu.SMEM),
            pl.BlockSpec(memory_space=pl.ANY),
            pl.BlockSpec((None, D_MODEL, FC), w_in_map),
            pl.BlockSpec((None, 1, FC), w_in_map),
            pl.BlockSpec((None, D_MODEL, FC), w_in_map),
            pl.BlockSpec((None, 1, FC), w_in_map),
            pl.BlockSpec((None, FC, D_MODEL), w_down_map),
            pl.BlockSpec((None, 1, D_MODEL), b_down_map),
        ],
        out_specs=pl.BlockSpec(memory_space=pl.ANY),
        scratch_shapes=[
            pltpu.VMEM((2, R_MAX * SUBLANES, LANES), U32),
            pltpu.VMEM((R_MAX, D_MODEL), F32),
            pltpu.VMEM((GROUP * RT, D_MODEL), BF16),
            pltpu.VMEM((D_MODEL, FC), BF16),
            pltpu.VMEM((D_MODEL, FC), BF16),
            pltpu.VMEM((FC, D_MODEL), BF16),
            pltpu.SemaphoreType.DMA((2,)),
            pltpu.SemaphoreType.DMA(()),
        ],
    )
    return pl.pallas_call(
        _expert_kernel,
        grid_spec=grid_spec,
        out_shape=jax.ShapeDtypeStruct((SLOT_ROWS * SUBLANES, LANES), U32),
        compiler_params=_cparams(("arbitrary", "arbitrary"), 60),
        name="experts",
    )(item_e, item_live, item_tok, item_tok, item_dst, h2p, wg, bg, wu, bu, wd, bd)


def _combine_kernel(meta_ref, x1_ref, y0_ref, y1_ref, y2_ref, y3_ref, out_ref):
    gates = [meta_ref[:, 2 * TOP_K + k:2 * TOP_K + k + 1] for k in range(TOP_K)]
    for j in range(SUBLANES):
        lo_cols = slice(j * LANES, (j + 1) * LANES)
        hi_cols = slice(HALF + j * LANES, HALF + (j + 1) * LANES)
        lo_acc = x1_ref[:, lo_cols]
        hi_acc = x1_ref[:, hi_cols]
        for k, y_ref in enumerate((y0_ref, y1_ref, y2_ref, y3_ref)):
            lo, hi = _unpack_halves(_load_row_tiles(y_ref, j, TM_CB))
            lo_acc = lo_acc + gates[k] * lo
            hi_acc = hi_acc + gates[k] * hi
        out_ref[:, lo_cols] = lo_acc
        out_ref[:, hi_cols] = hi_acc


def _combine(meta, x1, y_slots):
    tm = TM_CB
    nblk = N_TOK // tm
    slot_specs = [pl.BlockSpec((tm * SUBLANES, LANES), lambda i, k=k: (k * nblk + i, 0)) for k in range(TOP_K)]
    return pl.pallas_call(
        _combine_kernel,
        grid=(nblk,),
        in_specs=[
            pl.BlockSpec((tm, LANES), lambda i: (i, 0)),
            pl.BlockSpec((tm, D_MODEL), lambda i: (i, 0)),
        ] + slot_specs,
        out_specs=pl.BlockSpec((tm, D_MODEL), lambda i: (i, 0)),
        out_shape=jax.ShapeDtypeStruct((N_TOK, D_MODEL), F32),
        compiler_params=_cparams(("arbitrary",), 32),
        name="combine",
    )(meta, x1, y_slots, y_slots, y_slots, y_slots)


def _dft_tables():
    r = DFT_RADIX
    k = np.arange(r)
    e64 = np.exp(2j * np.pi * np.outer(k, k) / r)
    e4096 = np.exp(2j * np.pi * np.outer(k, k) / SEQ)
    s = np.arange(SEQ)
    f1 = e64[:, s % r]
    g = e64[:, s // r] * e4096[:, s % r] / math.sqrt(SEQ)
    c = np.arange(FOURIER_GROUP_DIM)
    ang = 2.0 * np.pi * np.outer(c, c) / FOURIER_GROUP_DIM
    scale = 1.0 / math.sqrt(FOURIER_GROUP_DIM)
    as32 = lambda a: jnp.asarray(a.astype(np.float32))
    return (as32(f1.real), as32(f1.imag), as32(g.real), as32(g.imag),
            jnp.asarray((np.cos(ang) * scale).astype(np.float32)).astype(BF16),
            jnp.asarray((np.sin(ang) * scale).astype(np.float32)).astype(BF16))


def _head_selectors():
    i = np.arange(LANES)[:, None]
    j = np.arange(LANES)[None, :]
    out = np.zeros((4, LANES, LANES), np.float32)
    for src in range(2):
        for dst in range(2):
            out[2 * src + dst] = (i - HEAD_DIM * src == j - HEAD_DIM * dst) & (j // HEAD_DIM == dst)
    return jnp.asarray(out).astype(BF16)


def kernel(x, norm1_g, w_in, b_branch_gate, q_norm_g, k_norm_g, attn_sink, w_fourier_out, w_attn_out, w_o,
           norm2_g, w_router, b_router, w_gate_e, b_gate_e, w_up_e, b_up_e, w_down_e, b_down_e):
    b, s, d = x.shape
    assert (b, s, d) == (BATCH, SEQ, D_MODEL) and norm1_g.shape[0] == 1
    x2 = x.reshape(N_TOK, D_MODEL)

    heads_per_tile = TN_IN // HEAD_DIM
    ones_bd = jnp.asarray((np.arange(TN_IN)[:, None] // HEAD_DIM == np.arange(TN_IN)[None, :] // HEAD_DIM)
                          .astype(np.float32)).astype(BF16)
    u_f, q, kv, gates = _in_proj(
        x2, norm1_g[0].reshape(1, D_MODEL), w_in[0].astype(BF16), b_branch_gate[0].reshape(1, 2 * D_MODEL),
        jnp.tile(q_norm_g[0], heads_per_tile).reshape(1, TN_IN),
        jnp.tile(k_norm_g[0], heads_per_tile).reshape(1, TN_IN), ones_bd)
    y_f = _fourier(u_f, *_dft_tables())
    attn = _attention(attn_sink[0], q, kv, _head_selectors())

    wr = jnp.zeros((D_MODEL, LANES), BF16).at[:, :N_EXPERTS].set(w_router[0].astype(BF16))
    br = jnp.full((1, LANES), NEG_INF, F32).at[0, :N_EXPERTS].set(b_router[0])
    ltri = jnp.asarray(np.tril(np.ones((TM_MG, TM_MG), np.float32), -1)).astype(BF16)
    x1, h2p, meta, cnt = _merge(y_f, attn, gates, x2, w_fourier_out[0].astype(BF16), w_attn_out[0].astype(BF16),
                                w_o[0].astype(BF16), norm2_g[0].reshape(1, D_MODEL), wr, br, ltri)

    tables = _routing_tables(cnt[0, :N_EXPERTS].astype(jnp.int32), meta[:, 0:TOP_K].astype(jnp.int32),
                             meta[:, TOP_K:2 * TOP_K].astype(jnp.int32))
    y_slots = _experts(*tables, h2p,
                       w_gate_e[0], b_gate_e[0].reshape(N_EXPERTS, 1, D_EXPERT),
                       w_up_e[0], b_up_e[0].reshape(N_EXPERTS, 1, D_EXPERT),
                       w_down_e[0], b_down_e[0].reshape(N_EXPERTS, 1, D_MODEL))
    out = _combine(meta, x1, y_slots)
    return out.reshape(BATCH, SEQ, D_MODEL)


def _routing_tables(counts, top_idx, rank):
    padded = ((counts + RT - 1) // RT) * RT
    pend = jnp.cumsum(padded)
    pstart = pend - padded
    dest = pstart[top_idx] + rank
    n_assign = TOP_K * N_TOK
    row_assign = jnp.full((ROWS,), -1, jnp.int32).at[dest.reshape(n_assign)].set(
        jnp.arange(n_assign, dtype=jnp.int32), unique_indices=True)

    items_per_e = (padded + R_MAX - 1) // R_MAX
    it_end = jnp.cumsum(items_per_e)
    it_start = it_end - items_per_e
    total_items = it_end[-1]
    wi = jnp.arange(N_ITEMS, dtype=jnp.int32)
    e_w = jnp.minimum(jnp.searchsorted(it_end, wi, side="right"), N_EXPERTS - 1).astype(jnp.int32)
    j_w = wi - it_start[e_w]
    live = wi < total_items
    rows_w = jnp.clip(padded[e_w] - j_w * R_MAX, 0, R_MAX)
    e_last = e_w[jnp.maximum(total_items - 1, 0)]
    item_e = jnp.where(live, e_w, e_last).astype(jnp.int32)
    item_row0 = jnp.where(live, pstart[e_w] + j_w * R_MAX, 0).astype(jnp.int32)
    item_nsub = jnp.where(live, rows_w // RT, 0).astype(jnp.int32)
    item_live = live.astype(jnp.int32)
    r = jnp.arange(R_MAX, dtype=jnp.int32)
    src = jnp.minimum(item_row0[:, None] + r[None, :], ROWS - 1)
    item_rows = jnp.where(r[None, :] < (item_nsub * RT)[:, None], row_assign[src], -1)
    item_tok = (jnp.maximum(item_rows, 0) // TOP_K).reshape(N_ITEMS, 1, R_MAX)
    item_dst = jnp.where(item_rows >= 0, (item_rows % TOP_K) * N_TOK + item_rows // TOP_K,
                         TOP_K * N_TOK + r[None, :]).reshape(N_ITEMS, 1, R_MAX)
    return item_e, item_live, item_tok, item_dst
```

```python
import functools
import math

import jax
import jax.numpy as jnp
import numpy as np
from jax import lax
from jax.experimental import pallas as pl
from jax.experimental.pallas import tpu as pltpu

D_MODEL = 2048
BATCH = 4
SEQ = 4096
N_TOK = BATCH * SEQ
N_Q_HEADS = 16
N_KV_HEADS = 4
HEAD_DIM = 64
Q_WIDTH = N_Q_HEADS * HEAD_DIM
KV_WIDTH = N_KV_HEADS * HEAD_DIM
WINDOW = 128
N_FOURIER_GROUPS = 4
FOURIER_GROUP_DIM = 256
FOURIER_WIDTH = N_FOURIER_GROUPS * FOURIER_GROUP_DIM
IN_WIDTH = FOURIER_WIDTH + Q_WIDTH + 2 * KV_WIDTH + 2 * D_MODEL
N_EXPERTS = 32
TOP_K = 4
D_EXPERT = D_MODEL
SWIGLU_LIMIT = 7.0
SWIGLU_ALPHA = 1.702
RMS_EPS = 1e-6
NEG_INF = -1e30

F32 = jnp.float32
BF16 = jnp.bfloat16
U32 = jnp.uint32

V7X_VMEM_BYTES = 64 * 1024 * 1024
LANES = 128
SUBLANES = 8
MIB = 1024 * 1024

TM_IN = 512
TN_IN = 512
NORM_ROWS = 64
TT_F = 512
DFT_RADIX = 64
TQ = 512
QB = 128
TM_MG = 256
RT = 256
R_MAX = 9 * RT
FC = 256
NF = D_EXPERT // FC
ROWS = TOP_K * N_TOK + N_EXPERTS * RT
N_ITEMS = (TOP_K * N_TOK + N_EXPERTS * (RT - 1) + N_EXPERTS * (R_MAX - RT)) // R_MAX + 1
TM_CB = 256
HALF = D_MODEL // 2
N_SUB = R_MAX // RT
GROUP = 3
OUT_CHUNK = 256
HALF_STEPS = NF // 2
ROWS_PER_STEP = RT // HALF_STEPS
SLOT_ROWS = TOP_K * N_TOK + R_MAX


def _cparams(sem, vmem_mib):
    return pltpu.CompilerParams(dimension_semantics=sem, vmem_limit_bytes=vmem_mib * MIB)


def _dot(a, b):
    return jnp.dot(a, b, preferred_element_type=F32)


def _pack_halves(lo, hi):
    lo_bits = pltpu.bitcast(lo.astype(BF16).astype(F32), U32)
    hi_bits = pltpu.bitcast(hi.astype(BF16).astype(F32), U32)
    return (hi_bits & jnp.uint32(0xFFFF0000)) | (lo_bits >> jnp.uint32(16))


def _unpack_halves(words):
    lo = pltpu.bitcast(words << jnp.uint32(16), F32)
    hi = pltpu.bitcast(words & jnp.uint32(0xFFFF0000), F32)
    return lo, hi


def _store_row_tiles(tiles_ref, words):
    n_rows = words.shape[0]
    for j in range(SUBLANES):
        tiles_ref[pl.ds(j, n_rows, stride=SUBLANES), :] = words[:, j * LANES:(j + 1) * LANES]


def _load_row_tiles(tiles_ref, j, n_rows):
    return tiles_ref[pl.ds(j, n_rows, stride=SUBLANES), :]


def _in_proj_kernel(x_ref, g1_ref, w_ref, b_ref, qg_ref, kg_ref, ones_ref,
                    u_ref, q_ref, kv_ref, gate_ref, h_scr):
    def body(c, carry):
        rows = pl.ds(pl.multiple_of(c * NORM_ROWS, NORM_ROWS), NORM_ROWS)
        x = x_ref[rows, :]
        ms = jnp.mean(x * x, axis=-1, keepdims=True)
        h_scr[rows, :] = (x * lax.rsqrt(ms + RMS_EPS) * g1_ref[...]).astype(BF16)
        return carry
    lax.fori_loop(0, TM_IN // NORM_ROWS, body, 0)

    def head_norm(a, gain):
        sq = a * a
        hi = sq.astype(BF16)
        lo = (sq - hi.astype(F32)).astype(BF16)
        ssq = _dot(hi, ones_ref[...]) + _dot(lo, ones_ref[...])
        return a * lax.rsqrt(ssq * (1.0 / HEAD_DIM) + RMS_EPS) * gain

    h = h_scr[...]
    for c in range(IN_WIDTH // TN_IN):
        col0 = c * TN_IN
        acc = _dot(h, w_ref[:, col0:col0 + TN_IN])
        if col0 < FOURIER_WIDTH:
            u_ref[:, col0:col0 + TN_IN] = acc.astype(BF16)
        elif col0 < FOURIER_WIDTH + Q_WIDTH:
            o = col0 - FOURIER_WIDTH
            q_ref[:, o:o + TN_IN] = (head_norm(acc, qg_ref[...]) * (HEAD_DIM ** -0.5)).astype(BF16)
        elif col0 < FOURIER_WIDTH + Q_WIDTH + 2 * KV_WIDTH:
            lane = lax.broadcasted_iota(jnp.int32, acc.shape, 1)
            kv_ref[...] = jnp.where(lane < KV_WIDTH, head_norm(acc, kg_ref[...]), acc).astype(BF16)
        else:
            o = col0 - (FOURIER_WIDTH + Q_WIDTH + 2 * KV_WIDTH)
            z = acc + b_ref[:, o:o + TN_IN]
            gate_ref[:, o:o + TN_IN] = (1.0 / (1.0 + jnp.exp(-z))).astype(BF16)


def _in_proj(x2, g1, w_bf, bias, qg_t, kg_t, ones_bd):
    const = lambda i: (0, 0)
    row = lambda i: (i, 0)
    return pl.pallas_call(
        _in_proj_kernel,
        grid=(N_TOK // TM_IN,),
        in_specs=[
            pl.BlockSpec((TM_IN, D_MODEL), row),
            pl.BlockSpec((1, D_MODEL), const),
            pl.BlockSpec((D_MODEL, IN_WIDTH), const, pipeline_mode=pl.Buffered(1)),
            pl.BlockSpec((1, 2 * D_MODEL), const),
            pl.BlockSpec((1, TN_IN), const),
            pl.BlockSpec((1, TN_IN), const),
            pl.BlockSpec((TN_IN, TN_IN), const),
        ],
        out_specs=[
            pl.BlockSpec((TM_IN, FOURIER_WIDTH), row),
            pl.BlockSpec((TM_IN, Q_WIDTH), row),
            pl.BlockSpec((TM_IN, 2 * KV_WIDTH), row),
            pl.BlockSpec((TM_IN, 2 * D_MODEL), row),
        ],
        out_shape=[
            jax.ShapeDtypeStruct((N_TOK, FOURIER_WIDTH), BF16),
            jax.ShapeDtypeStruct((N_TOK, Q_WIDTH), BF16),
            jax.ShapeDtypeStruct((N_TOK, 2 * KV_WIDTH), BF16),
            jax.ShapeDtypeStruct((N_TOK, 2 * D_MODEL), BF16),
        ],
        scratch_shapes=[pltpu.VMEM((TM_IN, D_MODEL), BF16)],
        compiler_params=_cparams(("arbitrary",), 60),
        name="in_proj",
    )(x2, g1, w_bf, bias, qg_t, kg_t, ones_bd)


def _fourier_kernel(u_ref, f1r_ref, f1i_ref, gr_ref, gi_ref, cc_ref, sc_ref, y_ref, cs_scr, ss_scr):
    b = pl.program_id(1)

    @pl.when(b == 0)
    def _():
        gr = gr_ref[...]
        gi = gi_ref[...]
        for a in range(TT_F // DFT_RADIX):
            f1r = f1r_ref[a:a + 1, :]
            f1i = f1i_ref[a:a + 1, :]
            rows = slice(a * DFT_RADIX, (a + 1) * DFT_RADIX)
            cs_scr[rows, :] = (f1r * gr - f1i * gi).astype(BF16)
            ss_scr[rows, :] = (f1r * gi + f1i * gr).astype(BF16)

    u = u_ref[...]
    a_seq = _dot(cs_scr[...], u)
    b_seq = _dot(ss_scr[...], u)
    for g in range(N_FOURIER_GROUPS):
        cols = slice(g * FOURIER_GROUP_DIM, (g + 1) * FOURIER_GROUP_DIM)
        y = _dot(a_seq[:, cols].astype(BF16), cc_ref[...]) - _dot(b_seq[:, cols].astype(BF16), sc_ref[...])
        y_ref[:, cols] = y.astype(BF16)


def _fourier(u, f1r, f1i, gr, gi, cc, sc):
    nt = SEQ // TT_F
    f1_rows = TT_F // DFT_RADIX
    return pl.pallas_call(
        _fourier_kernel,
        grid=(nt, BATCH),
        in_specs=[
            pl.BlockSpec((SEQ, FOURIER_WIDTH), lambda t, b: (b, 0)),
            pl.BlockSpec((f1_rows, SEQ), lambda t, b: (t, 0)),
            pl.BlockSpec((f1_rows, SEQ), lambda t, b: (t, 0)),
            pl.BlockSpec((DFT_RADIX, SEQ), lambda t, b: (0, 0)),
            pl.BlockSpec((DFT_RADIX, SEQ), lambda t, b: (0, 0)),
            pl.BlockSpec((FOURIER_GROUP_DIM, FOURIER_GROUP_DIM), lambda t, b: (0, 0)),
            pl.BlockSpec((FOURIER_GROUP_DIM, FOURIER_GROUP_DIM), lambda t, b: (0, 0)),
        ],
        out_specs=pl.BlockSpec((TT_F, FOURIER_WIDTH), lambda t, b: (b * nt + t, 0)),
        out_shape=jax.ShapeDtypeStruct((N_TOK, FOURIER_WIDTH), BF16),
        scratch_shapes=[pltpu.VMEM((TT_F, SEQ), BF16), pltpu.VMEM((TT_F, SEQ), BF16)],
        compiler_params=_cparams(("arbitrary", "arbitrary"), 56),
        name="fourier",
    )(u, f1r, f1i, gr, gi, cc, sc)


def _alibi_slope(h):
    return float(2.0 ** (-8.0 * (h + 1) / N_Q_HEADS))


def _attention_kernel(sink_ref, q_ref, prev_ref, cur_ref, next_ref, sel_ref, o_ref, kz_scr, vz_scr):
    t = pl.program_id(1)
    band = jnp.concatenate([prev_ref[...], cur_ref[...], next_ref[...]], axis=0)
    for g in range(N_KV_HEADS):
        c, half = divmod(g, 2)
        kcol = band[:, c * LANES:(c + 1) * LANES]
        vcol = band[:, KV_WIDTH + c * LANES:KV_WIDTH + (c + 1) * LANES]
        for d in range(2):
            sel = sel_ref[half * 2 + d]
            kz_scr[2 * g + d] = jnp.transpose(_dot(kcol, sel)).astype(BF16)
            vz_scr[2 * g + d] = _dot(vcol, sel).astype(BF16)

    row = lax.broadcasted_iota(jnp.int32, (2 * QB, 3 * QB), 0)
    col = lax.broadcasted_iota(jnp.int32, (2 * QB, 3 * QB), 1)
    absrel = jnp.abs((row & (QB - 1)) - col + QB)
    absrel_f = absrel.astype(F32)
    top = lax.broadcasted_iota(jnp.int32, (2 * QB, 1), 0) < QB
    for i in range(TQ // QB):
        q_rows = slice(i * QB, (i + 1) * QB)
        k_rows = slice(i * QB, (i + 3) * QB)
        kpos = t * TQ + (i - 1) * QB + col
        mask = (absrel <= WINDOW) & (kpos >= 0) & (kpos < SEQ)
        for g in range(N_KV_HEADS):
            cols = [slice((2 * g + p) * LANES, (2 * g + p + 1) * LANES) for p in range(2)]
            qq = jnp.concatenate([q_ref[q_rows, cols[0]], q_ref[q_rows, cols[1]]], axis=0)
            out = jnp.zeros((2 * QB, LANES), F32)
            for d in range(2):
                h_top, h_bot = 4 * g + d, 4 * g + 2 + d
                slope = jnp.where(top, _alibi_slope(h_top), _alibi_slope(h_bot))
                sink = jnp.where(top, sink_ref[h_top], sink_ref[h_bot])
                s = _dot(qq, kz_scr[2 * g + d, :, k_rows])
                s = jnp.where(mask, s - slope * absrel_f, NEG_INF)
                m = jnp.maximum(jnp.max(s, axis=-1, keepdims=True), sink)
                pr = jnp.exp(s - m)
                den = jnp.sum(pr, axis=-1, keepdims=True) + jnp.exp(sink - m)
                out = out + _dot(pr.astype(BF16), vz_scr[2 * g + d, k_rows, :]) / den
            o_ref[q_rows, cols[0]] = out[:QB].astype(BF16)
            o_ref[q_rows, cols[1]] = out[QB:].astype(BF16)


def _attention(sink, q, kv, sel):
    nt = SEQ // TQ
    nb = SEQ // QB
    r = TQ // QB
    grid_spec = pltpu.PrefetchScalarGridSpec(
        num_scalar_prefetch=1,
        grid=(BATCH, nt),
        in_specs=[
            pl.BlockSpec((TQ, Q_WIDTH), lambda b, t, s: (b * nt + t, 0)),
            pl.BlockSpec((QB, 2 * KV_WIDTH), lambda b, t, s: (b * nb + jnp.maximum(t * r - 1, 0), 0)),
            pl.BlockSpec((TQ, 2 * KV_WIDTH), lambda b, t, s: (b * nt + t, 0)),
            pl.BlockSpec((QB, 2 * KV_WIDTH), lambda b, t, s: (b * nb + jnp.minimum(t * r + r, nb - 1), 0)),
            pl.BlockSpec((4, LANES, LANES), lambda b, t, s: (0, 0, 0)),
        ],
        out_specs=pl.BlockSpec((TQ, Q_WIDTH), lambda b, t, s: (b * nt + t, 0)),
        scratch_shapes=[pltpu.VMEM((2 * N_KV_HEADS, LANES, TQ + 2 * QB), BF16),
                        pltpu.VMEM((2 * N_KV_HEADS, TQ + 2 * QB, LANES), BF16)],
    )
    return pl.pallas_call(
        _attention_kernel,
        grid_spec=grid_spec,
        out_shape=jax.ShapeDtypeStruct((N_TOK, Q_WIDTH), BF16),
        compiler_params=_cparams(("arbitrary", "arbitrary"), 32),
        name="attention",
    )(sink, q, kv, kv, kv, sel)


def _merge_kernel(yf_ref, at_ref, gf_ref, ga_ref, x_ref, wfo_ref, wao_ref, wo_ref, g2_ref, wr_ref, br_ref,
                  ltri_ref, x1_ref, h2p_ref, meta_ref, cnt_ref, base_scr):
    i = pl.program_id(0)

    @pl.when(i == 0)
    def _():
        base_scr[...] = jnp.zeros_like(base_scr)

    yf = _dot(yf_ref[...], wfo_ref[...])
    ya = _dot(at_ref[...], wao_ref[...])
    merged = gf_ref[...].astype(F32) * yf + ga_ref[...].astype(F32) * ya
    x1 = x_ref[...] + _dot(merged.astype(BF16), wo_ref[...])
    x1_ref[...] = x1
    ms = jnp.mean(x1 * x1, axis=-1, keepdims=True)
    h2 = x1 * lax.rsqrt(ms + RMS_EPS) * g2_ref[...]
    _store_row_tiles(h2p_ref, _pack_halves(h2[:, :HALF], h2[:, HALF:]))
    logits = _dot(h2.astype(BF16), wr_ref[...]) + br_ref[...]

    lane = lax.broadcasted_iota(jnp.int32, logits.shape, 1)
    lane_f = lane.astype(F32)
    vals = logits
    top_v, top_i, onehots = [], [], []
    for _ in range(TOP_K):
        m = jnp.max(vals, axis=-1, keepdims=True)
        idx = jnp.min(jnp.where(vals == m, lane_f, float(LANES)), axis=-1, keepdims=True)
        oh = lane_f == idx
        top_v.append(m)
        top_i.append(idx)
        onehots.append(oh)
        vals = jnp.where(oh, -jnp.inf, vals)
    exps = [jnp.exp(v - top_v[0]) for v in top_v]
    den = exps[0] + exps[1] + exps[2] + exps[3]
    gates = [e / den for e in exps]

    cnt = sum(oh.astype(F32) for oh in onehots)
    prefix = _dot(ltri_ref[...], cnt.astype(BF16))
    tot = base_scr[...] + prefix
    ranks = [jnp.sum(jnp.where(oh, tot, 0.0), axis=-1, keepdims=True) for oh in onehots]
    base_scr[...] = base_scr[...] + jnp.sum(cnt, axis=0, keepdims=True)
    cnt_ref[...] = jnp.broadcast_to(base_scr[...], cnt_ref.shape)

    meta = jnp.zeros(logits.shape, F32)
    for k in range(TOP_K):
        meta = jnp.where(lane == k, top_i[k], meta)
        meta = jnp.where(lane == TOP_K + k, ranks[k], meta)
        meta = jnp.where(lane == 2 * TOP_K + k, gates[k], meta)
    meta_ref[...] = meta


def _merge(yf, attn, gates, x2, wfo, wao, wo, g2, wr, br, ltri):
    tm = TM_MG
    const = lambda i: (0, 0)
    resident = functools.partial(pl.BlockSpec, index_map=const, pipeline_mode=pl.Buffered(1))
    return pl.pallas_call(
        _merge_kernel,
        grid=(N_TOK // tm,),
        in_specs=[
            pl.BlockSpec((tm, FOURIER_WIDTH), lambda i: (i, 0)),
            pl.BlockSpec((tm, Q_WIDTH), lambda i: (i, 0)),
            pl.BlockSpec((tm, D_MODEL), lambda i: (i, 0)),
            pl.BlockSpec((tm, D_MODEL), lambda i: (i, 1)),
            pl.BlockSpec((tm, D_MODEL), lambda i: (i, 0)),
            resident((FOURIER_WIDTH, D_MODEL)),
            resident((Q_WIDTH, D_MODEL)),
            resident((D_MODEL, D_MODEL)),
            pl.BlockSpec((1, D_MODEL), const),
            pl.BlockSpec((D_MODEL, LANES), const),
            pl.BlockSpec((1, LANES), const),
            pl.BlockSpec((tm, tm), const),
        ],
        out_specs=[
            pl.BlockSpec((tm, D_MODEL), lambda i: (i, 0)),
            pl.BlockSpec((tm * SUBLANES, LANES), lambda i: (i, 0)),
            pl.BlockSpec((tm, LANES), lambda i: (i, 0)),
            pl.BlockSpec((8, LANES), const),
        ],
        out_shape=[
            jax.ShapeDtypeStruct((N_TOK, D_MODEL), F32),
            jax.ShapeDtypeStruct((N_TOK * SUBLANES, LANES), U32),
            jax.ShapeDtypeStruct((N_TOK, LANES), F32),
            jax.ShapeDtypeStruct((8, LANES), F32),
        ],
        scratch_shapes=[pltpu.VMEM((1, LANES), F32)],
        compiler_params=_cparams(("arbitrary",), 56),
        name="merge",
    )(yf, attn, gates, gates, x2, wfo, wao, wo, g2, wr, br, ltri)


def _expert_kernel(e_ref, live_ref, tok_cur_ref, tok_nxt_ref, dst_prev_ref,
                   h2p_hbm, wg_ref, bg_ref, wu_ref, bu_ref, wd_ref, bd_ref,
                   yslots_hbm, xu_scr, acc_scr, xs_scr, wgb, wub, wdb, gsem, ssem):
    del e_ref
    w = pl.program_id(0)
    f = pl.program_id(1)
    live = live_ref[w] > 0
    prev_live = (w > 0) & (live_ref[jnp.maximum(w - 1, 0)] > 0)
    cur = w & 1
    nxt = 1 - cur
    tile_rows = RT * SUBLANES

    def sub_rows(s):
        return pl.ds(pl.multiple_of(s * RT, RT), RT)

    def row_tile(row):
        return pl.ds(pl.multiple_of(row * SUBLANES, SUBLANES), SUBLANES)

    def gather_copy(tok_ref, slot, row):
        tok = tok_ref[0, 0, row]
        return pltpu.make_async_copy(h2p_hbm.at[row_tile(tok), :], xu_scr.at[slot, row_tile(row), :], gsem.at[slot])

    def scatter_copy(row):
        dst = dst_prev_ref[0, 0, row]
        return pltpu.make_async_copy(xu_scr.at[nxt, row_tile(row), :], yslots_hbm.at[row_tile(dst), :], ssem)

    def for_all_rows(fn):
        def body(row, carry):
            fn(row)
            return carry
        lax.fori_loop(0, R_MAX, body, 0, unroll=8)

    def scatter_prev(s):
        base = s * RT + f * ROWS_PER_STEP
        for g in range(ROWS_PER_STEP):
            scatter_copy(base + g).start(priority=g % 2)

    def gather_next(s):
        base = s * RT + (f - HALF_STEPS) * ROWS_PER_STEP
        for g in range(ROWS_PER_STEP):
            gather_copy(tok_nxt_ref, nxt, base + g).start(priority=g % 2)

    def cast_weights():
        wgb[...] = wg_ref[...].astype(BF16)
        wub[...] = wu_ref[...].astype(BF16)
        wdb[...] = wd_ref[...].astype(BF16)

    def stage_inputs(first_sub, n_sub, stage_row0):
        n_rows = n_sub * RT
        tiles = xu_scr.at[cur, pl.ds(pl.multiple_of(first_sub * tile_rows, tile_rows), n_sub * tile_rows), :]
        stage = xs_scr.at[pl.ds(stage_row0, n_rows), :]
        for j in range(SUBLANES):
            lo, hi = _unpack_halves(_load_row_tiles(tiles, j, n_rows))
            stage[:, j * LANES:(j + 1) * LANES] = lo.astype(BF16)
            stage[:, HALF + j * LANES:HALF + (j + 1) * LANES] = hi.astype(BF16)

    def activations(n_sub, stage_row0):
        x = xs_scr[pl.ds(stage_row0, n_sub * RT), :]
        g = _dot(x, wgb[...]) + bg_ref[...]
        u = _dot(x, wub[...]) + bu_ref[...]
        g = jnp.minimum(g, SWIGLU_LIMIT)
        u = jnp.clip(u, -SWIGLU_LIMIT, SWIGLU_LIMIT)
        act = (u + 1.0) * (g * (1.0 / (1.0 + jnp.exp(-SWIGLU_ALPHA * g))))
        return act.astype(BF16)

    def accumulate(j):
        n_rows = GROUP * RT
        rows = pl.ds(pl.multiple_of(j * n_rows, n_rows), n_rows)
        stage_inputs(j * GROUP, GROUP, 0)
        c = _dot(activations(GROUP, 0), wdb[...])
        acc_scr[rows, :] = jnp.where(f == 0, c, acc_scr[rows, :] + c)

    def finish(j):
        n_rows = GROUP * RT
        rows = pl.ds(pl.multiple_of(j * n_rows, n_rows), n_rows)
        tiles = xu_scr.at[cur, pl.ds(pl.multiple_of(j * GROUP * tile_rows, tile_rows), GROUP * tile_rows), :]
        stage_inputs(j * GROUP, GROUP, 0)
        act = activations(GROUP, 0)
        for c in range(HALF // OUT_CHUNK):
            lo_cols = slice(c * OUT_CHUNK, (c + 1) * OUT_CHUNK)
            hi_cols = slice(HALF + c * OUT_CHUNK, HALF + (c + 1) * OUT_CHUNK)
            y_lo = acc_scr[rows, lo_cols] + _dot(act, wdb[:, lo_cols]) + bd_ref[:, lo_cols]
            y_hi = acc_scr[rows, hi_cols] + _dot(act, wdb[:, hi_cols]) + bd_ref[:, hi_cols]
            words = _pack_halves(y_lo, y_hi)
            for jj in range(OUT_CHUNK // LANES):
                j_tile = c * (OUT_CHUNK // LANES) + jj
                tiles[pl.ds(j_tile, n_rows, stride=SUBLANES), :] = words[:, jj * LANES:(jj + 1) * LANES]

    def for_groups(compute, row_copies):
        def group(j, carry):
            compute(j)
            for i in range(GROUP):
                row_copies(j * GROUP + i)
            return carry
        lax.fori_loop(0, N_SUB // GROUP, group, 0)

    first_tile = pl.ds(0, SUBLANES)
    had_prev = (w == 0) | prev_live

    @pl.when(f == 0)
    def _():
        @pl.when(w == 0)
        def _():
            def zero(s, carry):
                acc_scr[sub_rows(s), :] = jnp.zeros((RT, D_MODEL), F32)
                xu_scr[1, pl.ds(pl.multiple_of(s * tile_rows, tile_rows), tile_rows), :] = (
                    jnp.zeros((tile_rows, LANES), U32))
                return carry
            lax.fori_loop(0, N_SUB, zero, 0)
            for_all_rows(lambda row: gather_copy(tok_cur_ref, cur, row).start())

        @pl.when(had_prev)
        def _():
            row_in = pltpu.make_async_copy(h2p_hbm.at[first_tile, :], xu_scr.at[cur, first_tile, :], gsem.at[cur])
            for_all_rows(lambda row: row_in.wait())

    @pl.when((f == HALF_STEPS) & had_prev)
    def _():
        row_out = pltpu.make_async_copy(xu_scr.at[nxt, first_tile, :], yslots_hbm.at[first_tile, :], ssem)
        for_all_rows(lambda row: row_out.wait())

    @pl.when(live & (f < HALF_STEPS))
    def _():
        cast_weights()
        for_groups(accumulate, scatter_prev)

    @pl.when(live & (f >= HALF_STEPS) & (f < NF - 1))
    def _():
        cast_weights()
        for_groups(accumulate, gather_next)

    @pl.when(live & (f == NF - 1))
    def _():
        cast_weights()
        for_groups(finish, gather_next)

    @pl.when(jnp.logical_not(live) & prev_live & (f < HALF_STEPS))
    def _():
        for_groups(lambda j: None, scatter_prev)


def _experts(item_e, item_live, item_tok, item_dst, h2p, wg, bg, wu, bu, wd, bd):
    def w_in_map(w, f, e, lv):
        return (e[w], 0, jnp.where(lv[w] > 0, f, NF - 1))

    def w_down_map(w, f, e, lv):
        return (e[w], jnp.where(lv[w] > 0, f, NF - 1), 0)

    def b_down_map(w, f, e, lv):
        return (e[w], 0, 0)

    grid_spec = pltpu.PrefetchScalarGridSpec(
        num_scalar_prefetch=2,
        grid=(N_ITEMS, NF),
        in_specs=[
            pl.BlockSpec((1, 1, R_MAX), lambda w, f, e, lv: (w, 0, 0), memory_space=pltpu.SMEM),
            pl.BlockSpec((1, 1, R_MAX), lambda w, f, e, lv: (jnp.minimum(w + 1, N_ITEMS - 1), 0, 0),
                         memory_space=pltpu.SMEM),
            pl.BlockSpec((1, 1, R_MAX), lambda w, f, e, lv: (w, 0, 0), memory_space=pltpu.SMEM),
            pl.BlockSpec(memory_space=pl.ANY),
            pl.BlockSpec((None, D_MODEL, FC), w_in_map),
            pl.BlockSpec((None, 1, FC), w_in_map),
            pl.BlockSpec((None, D_MODEL, FC), w_in_map),
            pl.BlockSpec((None, 1, FC), w_in_map),
            pl.BlockSpec((None, FC, D_MODEL), w_down_map),
            pl.BlockSpec((None, 1, D_MODEL), b_down_map),
        ],
        out_specs=pl.BlockSpec(memory_space=pl.ANY),
        scratch_shapes=[
            pltpu.VMEM((2, R_MAX * SUBLANES, LANES), U32),
            pltpu.VMEM((R_MAX, D_MODEL), F32),
            pltpu.VMEM((GROUP * RT, D_MODEL), BF16),
            pltpu.VMEM((D_MODEL, FC), BF16),
            pltpu.VMEM((D_MODEL, FC), BF16),
            pltpu.VMEM((FC, D_MODEL), BF16),
            pltpu.SemaphoreType.DMA((2,)),
            pltpu.SemaphoreType.DMA(()),
        ],
    )
    return pl.pallas_call(
        _expert_kernel,
        grid_spec=grid_spec,
        out_shape=jax.ShapeDtypeStruct((SLOT_ROWS * SUBLANES, LANES), U32),
        compiler_params=_cparams(("arbitrary", "arbitrary"), 60),
        name="experts",
    )(item_e, item_live, item_tok, item_tok, item_dst, h2p, wg, bg, wu, bu, wd, bd)


def _combine_kernel(meta_ref, x1_ref, y0_ref, y1_ref, y2_ref, y3_ref, out_ref):
    gates = [meta_ref[:, 2 * TOP_K + k:2 * TOP_K + k + 1] for k in range(TOP_K)]
    for j in range(SUBLANES):
        lo_cols = slice(j * LANES, (j + 1) * LANES)
        hi_cols = slice(HALF + j * LANES, HALF + (j + 1) * LANES)
        lo_acc = x1_ref[:, lo_cols]
        hi_acc = x1_ref[:, hi_cols]
        for k, y_ref in enumerate((y0_ref, y1_ref, y2_ref, y3_ref)):
            lo, hi = _unpack_halves(_load_row_tiles(y_ref, j, TM_CB))
            lo_acc = lo_acc + gates[k] * lo
            hi_acc = hi_acc + gates[k] * hi
        out_ref[:, lo_cols] = lo_acc
        out_ref[:, hi_cols] = hi_acc


def _combine(meta, x1, y_slots):
    tm = TM_CB
    nblk = N_TOK // tm
    slot_specs = [pl.BlockSpec((tm * SUBLANES, LANES), lambda i, k=k: (k * nblk + i, 0)) for k in range(TOP_K)]
    return pl.pallas_call(
        _combine_kernel,
        grid=(nblk,),
        in_specs=[
            pl.BlockSpec((tm, LANES), lambda i: (i, 0)),
            pl.BlockSpec((tm, D_MODEL), lambda i: (i, 0)),
        ] + slot_specs,
        out_specs=pl.BlockSpec((tm, D_MODEL), lambda i: (i, 0)),
        out_shape=jax.ShapeDtypeStruct((N_TOK, D_MODEL), F32),
        compiler_params=_cparams(("arbitrary",), 32),
        name="combine",
    )(meta, x1, y_slots, y_slots, y_slots, y_slots)


def _dft_tables():
    r = DFT_RADIX
    k = np.arange(r)
    e64 = np.exp(2j * np.pi * np.outer(k, k) / r)
    e4096 = np.exp(2j * np.pi * np.outer(k, k) / SEQ)
    s = np.arange(SEQ)
    f1 = e64[:, s % r]
    g = e64[:, s // r] * e4096[:, s % r] / math.sqrt(SEQ)
    c = np.arange(FOURIER_GROUP_DIM)
    ang = 2.0 * np.pi * np.outer(c, c) / FOURIER_GROUP_DIM
    scale = 1.0 / math.sqrt(FOURIER_GROUP_DIM)
    as32 = lambda a: jnp.asarray(a.astype(np.float32))
    return (as32(f1.real), as32(f1.imag), as32(g.real), as32(g.imag),
            jnp.asarray((np.cos(ang) * scale).astype(np.float32)).astype(BF16),
            jnp.asarray((np.sin(ang) * scale).astype(np.float32)).astype(BF16))


def _head_selectors():
    i = np.arange(LANES)[:, None]
    j = np.arange(LANES)[None, :]
    out = np.zeros((4, LANES, LANES), np.float32)
    for src in range(2):
        for dst in range(2):
            out[2 * src + dst] = (i - HEAD_DIM * src == j - HEAD_DIM * dst) & (j // HEAD_DIM == dst)
    return jnp.asarray(out).astype(BF16)


def kernel(x, norm1_g, w_in, b_branch_gate, q_norm_g, k_norm_g, attn_sink, w_fourier_out, w_attn_out, w_o,
           norm2_g, w_router, b_router, w_gate_e, b_gate_e, w_up_e, b_up_e, w_down_e, b_down_e):
    b, s, d = x.shape
    assert (b, s, d) == (BATCH, SEQ, D_MODEL) and norm1_g.shape[0] == 1
    x2 = x.reshape(N_TOK, D_MODEL)

    heads_per_tile = TN_IN // HEAD_DIM
    ones_bd = jnp.asarray((np.arange(TN_IN)[:, None] // HEAD_DIM == np.arange(TN_IN)[None, :] // HEAD_DIM)
                          .astype(np.float32)).astype(BF16)
    u_f, q, kv, gates = _in_proj(
        x2, norm1_g[0].reshape(1, D_MODEL), w_in[0].astype(BF16), b_branch_gate[0].reshape(1, 2 * D_MODEL),
        jnp.tile(q_norm_g[0], heads_per_tile).reshape(1, TN_IN),
        jnp.tile(k_norm_g[0], heads_per_tile).reshape(1, TN_IN), ones_bd)
    y_f = _fourier(u_f, *_dft_tables())
    attn = _attention(attn_sink[0], q, kv, _head_selectors())

    wr = jnp.zeros((D_MODEL, LANES), BF16).at[:, :N_EXPERTS].set(w_router[0].astype(BF16))
    br = jnp.full((1, LANES), NEG_INF, F32).at[0, :N_EXPERTS].set(b_router[0])
    ltri = jnp.asarray(np.tril(np.ones((TM_MG, TM_MG), np.float32), -1)).astype(BF16)
    x1, h2p, meta, cnt = _merge(y_f, attn, gates, x2, w_fourier_out[0].astype(BF16), w_attn_out[0].astype(BF16),
                                w_o[0].astype(BF16), norm2_g[0].reshape(1, D_MODEL), wr, br, ltri)

    tables = _routing_tables(cnt[0, :N_EXPERTS].astype(jnp.int32), meta[:, 0:TOP_K].astype(jnp.int32),
                             meta[:, TOP_K:2 * TOP_K].astype(jnp.int32))
    y_slots = _experts(*tables, h2p,
                       w_gate_e[0], b_gate_e[0].reshape(N_EXPERTS, 1, D_EXPERT),
                       w_up_e[0], b_up_e[0].reshape(N_EXPERTS, 1, D_EXPERT),
                       w_down_e[0], b_down_e[0].reshape(N_EXPERTS, 1, D_MODEL))
    out = _combine(meta, x1, y_slots)
    return out.reshape(BATCH, SEQ, D_MODEL)


def _routing_tables(counts, top_idx, rank):
    padded = ((counts + RT - 1) // RT) * RT
    pend = jnp.cumsum(padded)
    pstart = pend - padded
    dest = pstart[top_idx] + rank
    n_assign = TOP_K * N_TOK
    row_assign = jnp.full((ROWS,), -1, jnp.int32).at[dest.reshape(n_assign)].set(
        jnp.arange(n_assign, dtype=jnp.int32), unique_indices=True)

    items_per_e = (padded + R_MAX - 1) // R_MAX
    it_end = jnp.cumsum(items_per_e)
    it_start = it_end - items_per_e
    total_items = it_end[-1]
    wi = jnp.arange(N_ITEMS, dtype=jnp.int32)
    e_w = jnp.minimum(jnp.searchsorted(it_end, wi, side="right"), N_EXPERTS - 1).astype(jnp.int32)
    j_w = wi - it_start[e_w]
    live = wi < total_items
    rows_w = jnp.clip(padded[e_w] - j_w * R_MAX, 0, R_MAX)
    e_last = e_w[jnp.maximum(total_items - 1, 0)]
    item_e = jnp.where(live, e_w, e_last).astype(jnp.int32)
    item_row0 = jnp.where(live, pstart[e_w] + j_w * R_MAX, 0).astype(jnp.int32)
    item_nsub = jnp.where(live, rows_w // RT, 0).astype(jnp.int32)
    item_live = live.astype(jnp.int32)
    r = jnp.arange(R_MAX, dtype=jnp.int32)
    src = jnp.minimum(item_row0[:, None] + r[None, :], ROWS - 1)
    item_rows = jnp.where(r[None, :] < (item_nsub * RT)[:, None], row_assign[src], -1)
    item_tok = (jnp.maximum(item_rows, 0) // TOP_K).reshape(N_ITEMS, 1, R_MAX)
    item_dst = jnp.where(item_rows >= 0, (item_rows % TOP_K) * N_TOK + item_rows // TOP_K,
                         TOP_K * N_TOK + r[None, :])
    item_dst = jnp.concatenate([TOP_K * N_TOK + r[None, :], item_dst], axis=0).reshape(N_ITEMS + 1, 1, R_MAX)
    return item_e, item_live, item_tok, item_dst
```

```python
import functools
import math

import jax
import jax.numpy as jnp
import numpy as np
from jax import lax
from jax.experimental import pallas as pl
from jax.experimental.pallas import tpu as pltpu

D_MODEL = 2048
BATCH = 4
SEQ = 4096
N_TOK = BATCH * SEQ
N_Q_HEADS = 16
N_KV_HEADS = 4
HEAD_DIM = 64
Q_WIDTH = N_Q_HEADS * HEAD_DIM
KV_WIDTH = N_KV_HEADS * HEAD_DIM
WINDOW = 128
N_FOURIER_GROUPS = 4
FOURIER_GROUP_DIM = 256
FOURIER_WIDTH = N_FOURIER_GROUPS * FOURIER_GROUP_DIM
IN_WIDTH = FOURIER_WIDTH + Q_WIDTH + 2 * KV_WIDTH + 2 * D_MODEL
N_EXPERTS = 32
TOP_K = 4
D_EXPERT = D_MODEL
SWIGLU_LIMIT = 7.0
SWIGLU_ALPHA = 1.702
RMS_EPS = 1e-6
NEG_INF = -1e30

F32 = jnp.float32
BF16 = jnp.bfloat16
U32 = jnp.uint32

V7X_VMEM_BYTES = 64 * 1024 * 1024
LANES = 128
SUBLANES = 8
MIB = 1024 * 1024

TM_IN = 512
TN_IN = 512
NORM_ROWS = 64
TT_F = 512
DFT_RADIX = 64
TQ = 512
QB = 128
TM_MG = 256
RT = 256
R_MAX = 9 * RT
FC = 256
NF = D_EXPERT // FC
ROWS = TOP_K * N_TOK + N_EXPERTS * RT
N_ITEMS = (TOP_K * N_TOK + N_EXPERTS * (RT - 1) + N_EXPERTS * (R_MAX - RT)) // R_MAX + 1
TM_CB = 256
HALF = D_MODEL // 2
N_SUB = R_MAX // RT
GROUP = 3
OUT_CHUNK = 256
GATHER_PER_STEP = RT // NF
SLOT_ROWS = TOP_K * N_TOK + R_MAX


def _cparams(sem, vmem_mib):
    return pltpu.CompilerParams(dimension_semantics=sem, vmem_limit_bytes=vmem_mib * MIB)


def _dot(a, b):
    return jnp.dot(a, b, preferred_element_type=F32)


def _pack_halves(lo, hi):
    lo_bits = pltpu.bitcast(lo.astype(BF16).astype(F32), U32)
    hi_bits = pltpu.bitcast(hi.astype(BF16).astype(F32), U32)
    return (hi_bits & jnp.uint32(0xFFFF0000)) | (lo_bits >> jnp.uint32(16))


def _unpack_halves(words):
    lo = pltpu.bitcast(words << jnp.uint32(16), F32)
    hi = pltpu.bitcast(words & jnp.uint32(0xFFFF0000), F32)
    return lo, hi


def _store_row_tiles(tiles_ref, words):
    n_rows = words.shape[0]
    for j in range(SUBLANES):
        tiles_ref[pl.ds(j, n_rows, stride=SUBLANES), :] = words[:, j * LANES:(j + 1) * LANES]


def _load_row_tiles(tiles_ref, j, n_rows):
    return tiles_ref[pl.ds(j, n_rows, stride=SUBLANES), :]


def _in_proj_kernel(x_ref, g1_ref, w_ref, b_ref, qg_ref, kg_ref, ones_ref,
                    u_ref, q_ref, kv_ref, gate_ref, h_scr):
    def body(c, carry):
        rows = pl.ds(pl.multiple_of(c * NORM_ROWS, NORM_ROWS), NORM_ROWS)
        x = x_ref[rows, :]
        ms = jnp.mean(x * x, axis=-1, keepdims=True)
        h_scr[rows, :] = (x * lax.rsqrt(ms + RMS_EPS) * g1_ref[...]).astype(BF16)
        return carry
    lax.fori_loop(0, TM_IN // NORM_ROWS, body, 0)

    def head_norm(a, gain):
        sq = a * a
        hi = sq.astype(BF16)
        lo = (sq - hi.astype(F32)).astype(BF16)
        ssq = _dot(hi, ones_ref[...]) + _dot(lo, ones_ref[...])
        return a * lax.rsqrt(ssq * (1.0 / HEAD_DIM) + RMS_EPS) * gain

    h = h_scr[...]
    for c in range(IN_WIDTH // TN_IN):
        col0 = c * TN_IN
        acc = _dot(h, w_ref[:, col0:col0 + TN_IN])
        if col0 < FOURIER_WIDTH:
            u_ref[:, col0:col0 + TN_IN] = acc.astype(BF16)
        elif col0 < FOURIER_WIDTH + Q_WIDTH:
            o = col0 - FOURIER_WIDTH
            q_ref[:, o:o + TN_IN] = (head_norm(acc, qg_ref[...]) * (HEAD_DIM ** -0.5)).astype(BF16)
        elif col0 < FOURIER_WIDTH + Q_WIDTH + 2 * KV_WIDTH:
            lane = lax.broadcasted_iota(jnp.int32, acc.shape, 1)
            kv_ref[...] = jnp.where(lane < KV_WIDTH, head_norm(acc, kg_ref[...]), acc).astype(BF16)
        else:
            o = col0 - (FOURIER_WIDTH + Q_WIDTH + 2 * KV_WIDTH)
            z = acc + b_ref[:, o:o + TN_IN]
            gate_ref[:, o:o + TN_IN] = (1.0 / (1.0 + jnp.exp(-z))).astype(BF16)


def _in_proj(x2, g1, w_bf, bias, qg_t, kg_t, ones_bd):
    const = lambda i: (0, 0)
    row = lambda i: (i, 0)
    return pl.pallas_call(
        _in_proj_kernel,
        grid=(N_TOK // TM_IN,),
        in_specs=[
            pl.BlockSpec((TM_IN, D_MODEL), row),
            pl.BlockSpec((1, D_MODEL), const),
            pl.BlockSpec((D_MODEL, IN_WIDTH), const, pipeline_mode=pl.Buffered(1)),
            pl.BlockSpec((1, 2 * D_MODEL), const),
            pl.BlockSpec((1, TN_IN), const),
            pl.BlockSpec((1, TN_IN), const),
            pl.BlockSpec((TN_IN, TN_IN), const),
        ],
        out_specs=[
            pl.BlockSpec((TM_IN, FOURIER_WIDTH), row),
            pl.BlockSpec((TM_IN, Q_WIDTH), row),
            pl.BlockSpec((TM_IN, 2 * KV_WIDTH), row),
            pl.BlockSpec((TM_IN, 2 * D_MODEL), row),
        ],
        out_shape=[
            jax.ShapeDtypeStruct((N_TOK, FOURIER_WIDTH), BF16),
            jax.ShapeDtypeStruct((N_TOK, Q_WIDTH), BF16),
            jax.ShapeDtypeStruct((N_TOK, 2 * KV_WIDTH), BF16),
            jax.ShapeDtypeStruct((N_TOK, 2 * D_MODEL), BF16),
        ],
        scratch_shapes=[pltpu.VMEM((TM_IN, D_MODEL), BF16)],
        compiler_params=_cparams(("arbitrary",), 60),
        name="in_proj",
    )(x2, g1, w_bf, bias, qg_t, kg_t, ones_bd)


def _fourier_kernel(u_ref, f1r_ref, f1i_ref, gr_ref, gi_ref, cc_ref, sc_ref, y_ref, cs_scr, ss_scr):
    b = pl.program_id(1)

    @pl.when(b == 0)
    def _():
        gr = gr_ref[...]
        gi = gi_ref[...]
        for a in range(TT_F // DFT_RADIX):
            f1r = f1r_ref[a:a + 1, :]
            f1i = f1i_ref[a:a + 1, :]
            rows = slice(a * DFT_RADIX, (a + 1) * DFT_RADIX)
            cs_scr[rows, :] = (f1r * gr - f1i * gi).astype(BF16)
            ss_scr[rows, :] = (f1r * gi + f1i * gr).astype(BF16)

    u = u_ref[...]
    a_seq = _dot(cs_scr[...], u)
    b_seq = _dot(ss_scr[...], u)
    for g in range(N_FOURIER_GROUPS):
        cols = slice(g * FOURIER_GROUP_DIM, (g + 1) * FOURIER_GROUP_DIM)
        y = _dot(a_seq[:, cols].astype(BF16), cc_ref[...]) - _dot(b_seq[:, cols].astype(BF16), sc_ref[...])
        y_ref[:, cols] = y.astype(BF16)


def _fourier(u, f1r, f1i, gr, gi, cc, sc):
    nt = SEQ // TT_F
    f1_rows = TT_F // DFT_RADIX
    return pl.pallas_call(
        _fourier_kernel,
        grid=(nt, BATCH),
        in_specs=[
            pl.BlockSpec((SEQ, FOURIER_WIDTH), lambda t, b: (b, 0)),
            pl.BlockSpec((f1_rows, SEQ), lambda t, b: (t, 0)),
            pl.BlockSpec((f1_rows, SEQ), lambda t, b: (t, 0)),
            pl.BlockSpec((DFT_RADIX, SEQ), lambda t, b: (0, 0)),
            pl.BlockSpec((DFT_RADIX, SEQ), lambda t, b: (0, 0)),
            pl.BlockSpec((FOURIER_GROUP_DIM, FOURIER_GROUP_DIM), lambda t, b: (0, 0)),
            pl.BlockSpec((FOURIER_GROUP_DIM, FOURIER_GROUP_DIM), lambda t, b: (0, 0)),
        ],
        out_specs=pl.BlockSpec((TT_F, FOURIER_WIDTH), lambda t, b: (b * nt + t, 0)),
        out_shape=jax.ShapeDtypeStruct((N_TOK, FOURIER_WIDTH), BF16),
        scratch_shapes=[pltpu.VMEM((TT_F, SEQ), BF16), pltpu.VMEM((TT_F, SEQ), BF16)],
        compiler_params=_cparams(("arbitrary", "arbitrary"), 56),
        name="fourier",
    )(u, f1r, f1i, gr, gi, cc, sc)


def _alibi_slope(h):
    return float(2.0 ** (-8.0 * (h + 1) / N_Q_HEADS))


def _attention_kernel(sink_ref, q_ref, prev_ref, cur_ref, next_ref, sel_ref, o_ref, kz_scr, vz_scr):
    t = pl.program_id(1)
    band = jnp.concatenate([prev_ref[...], cur_ref[...], next_ref[...]], axis=0)
    for g in range(N_KV_HEADS):
        c, half = divmod(g, 2)
        kcol = band[:, c * LANES:(c + 1) * LANES]
        vcol = band[:, KV_WIDTH + c * LANES:KV_WIDTH + (c + 1) * LANES]
        for d in range(2):
            sel = sel_ref[half * 2 + d]
            kz_scr[2 * g + d] = jnp.transpose(_dot(kcol, sel)).astype(BF16)
            vz_scr[2 * g + d] = _dot(vcol, sel).astype(BF16)

    row = lax.broadcasted_iota(jnp.int32, (2 * QB, 3 * QB), 0)
    col = lax.broadcasted_iota(jnp.int32, (2 * QB, 3 * QB), 1)
    absrel = jnp.abs((row & (QB - 1)) - col + QB)
    absrel_f = absrel.astype(F32)
    top = lax.broadcasted_iota(jnp.int32, (2 * QB, 1), 0) < QB
    for i in range(TQ // QB):
        q_rows = slice(i * QB, (i + 1) * QB)
        k_rows = slice(i * QB, (i + 3) * QB)
        kpos = t * TQ + (i - 1) * QB + col
        mask = (absrel <= WINDOW) & (kpos >= 0) & (kpos < SEQ)
        for g in range(N_KV_HEADS):
            cols = [slice((2 * g + p) * LANES, (2 * g + p + 1) * LANES) for p in range(2)]
            qq = jnp.concatenate([q_ref[q_rows, cols[0]], q_ref[q_rows, cols[1]]], axis=0)
            out = jnp.zeros((2 * QB, LANES), F32)
            for d in range(2):
                h_top, h_bot = 4 * g + d, 4 * g + 2 + d
                slope = jnp.where(top, _alibi_slope(h_top), _alibi_slope(h_bot))
                sink = jnp.where(top, sink_ref[h_top], sink_ref[h_bot])
                s = _dot(qq, kz_scr[2 * g + d, :, k_rows])
                s = jnp.where(mask, s - slope * absrel_f, NEG_INF)
                m = jnp.maximum(jnp.max(s, axis=-1, keepdims=True), sink)
                pr = jnp.exp(s - m)
                den = jnp.sum(pr, axis=-1, keepdims=True) + jnp.exp(sink - m)
                out = out + _dot(pr.astype(BF16), vz_scr[2 * g + d, k_rows, :]) / den
            o_ref[q_rows, cols[0]] = out[:QB].astype(BF16)
            o_ref[q_rows, cols[1]] = out[QB:].astype(BF16)


def _attention(sink, q, kv, sel):
    nt = SEQ // TQ
    nb = SEQ // QB
    r = TQ // QB
    grid_spec = pltpu.PrefetchScalarGridSpec(
        num_scalar_prefetch=1,
        grid=(BATCH, nt),
        in_specs=[
            pl.BlockSpec((TQ, Q_WIDTH), lambda b, t, s: (b * nt + t, 0)),
            pl.BlockSpec((QB, 2 * KV_WIDTH), lambda b, t, s: (b * nb + jnp.maximum(t * r - 1, 0), 0)),
            pl.BlockSpec((TQ, 2 * KV_WIDTH), lambda b, t, s: (b * nt + t, 0)),
            pl.BlockSpec((QB, 2 * KV_WIDTH), lambda b, t, s: (b * nb + jnp.minimum(t * r + r, nb - 1), 0)),
            pl.BlockSpec((4, LANES, LANES), lambda b, t, s: (0, 0, 0)),
        ],
        out_specs=pl.BlockSpec((TQ, Q_WIDTH), lambda b, t, s: (b * nt + t, 0)),
        scratch_shapes=[pltpu.VMEM((2 * N_KV_HEADS, LANES, TQ + 2 * QB), BF16),
                        pltpu.VMEM((2 * N_KV_HEADS, TQ + 2 * QB, LANES), BF16)],
    )
    return pl.pallas_call(
        _attention_kernel,
        grid_spec=grid_spec,
        out_shape=jax.ShapeDtypeStruct((N_TOK, Q_WIDTH), BF16),
        compiler_params=_cparams(("arbitrary", "arbitrary"), 32),
        name="attention",
    )(sink, q, kv, kv, kv, sel)


def _merge_kernel(yf_ref, at_ref, gf_ref, ga_ref, x_ref, wfo_ref, wao_ref, wo_ref, g2_ref, wr_ref, br_ref,
                  ltri_ref, x1_ref, h2p_ref, meta_ref, cnt_ref, base_scr):
    i = pl.program_id(0)

    @pl.when(i == 0)
    def _():
        base_scr[...] = jnp.zeros_like(base_scr)

    yf = _dot(yf_ref[...], wfo_ref[...])
    ya = _dot(at_ref[...], wao_ref[...])
    merged = gf_ref[...].astype(F32) * yf + ga_ref[...].astype(F32) * ya
    x1 = x_ref[...] + _dot(merged.astype(BF16), wo_ref[...])
    x1_ref[...] = x1
    ms = jnp.mean(x1 * x1, axis=-1, keepdims=True)
    h2 = x1 * lax.rsqrt(ms + RMS_EPS) * g2_ref[...]
    _store_row_tiles(h2p_ref, _pack_halves(h2[:, :HALF], h2[:, HALF:]))
    logits = _dot(h2.astype(BF16), wr_ref[...]) + br_ref[...]

    lane = lax.broadcasted_iota(jnp.int32, logits.shape, 1)
    lane_f = lane.astype(F32)
    vals = logits
    top_v, top_i, onehots = [], [], []
    for _ in range(TOP_K):
        m = jnp.max(vals, axis=-1, keepdims=True)
        idx = jnp.min(jnp.where(vals == m, lane_f, float(LANES)), axis=-1, keepdims=True)
        oh = lane_f == idx
        top_v.append(m)
        top_i.append(idx)
        onehots.append(oh)
        vals = jnp.where(oh, -jnp.inf, vals)
    exps = [jnp.exp(v - top_v[0]) for v in top_v]
    den = exps[0] + exps[1] + exps[2] + exps[3]
    gates = [e / den for e in exps]

    cnt = sum(oh.astype(F32) for oh in onehots)
    prefix = _dot(ltri_ref[...], cnt.astype(BF16))
    tot = base_scr[...] + prefix
    ranks = [jnp.sum(jnp.where(oh, tot, 0.0), axis=-1, keepdims=True) for oh in onehots]
    base_scr[...] = base_scr[...] + jnp.sum(cnt, axis=0, keepdims=True)
    cnt_ref[...] = jnp.broadcast_to(base_scr[...], cnt_ref.shape)

    meta = jnp.zeros(logits.shape, F32)
    for k in range(TOP_K):
        meta = jnp.where(lane == k, top_i[k], meta)
        meta = jnp.where(lane == TOP_K + k, ranks[k], meta)
        meta = jnp.where(lane == 2 * TOP_K + k, gates[k], meta)
    meta_ref[...] = meta


def _merge(yf, attn, gates, x2, wfo, wao, wo, g2, wr, br, ltri):
    tm = TM_MG
    const = lambda i: (0, 0)
    resident = functools.partial(pl.BlockSpec, index_map=const, pipeline_mode=pl.Buffered(1))
    return pl.pallas_call(
        _merge_kernel,
        grid=(N_TOK // tm,),
        in_specs=[
            pl.BlockSpec((tm, FOURIER_WIDTH), lambda i: (i, 0)),
            pl.BlockSpec((tm, Q_WIDTH), lambda i: (i, 0)),
            pl.BlockSpec((tm, D_MODEL), lambda i: (i, 0)),
            pl.BlockSpec((tm, D_MODEL), lambda i: (i, 1)),
            pl.BlockSpec((tm, D_MODEL), lambda i: (i, 0)),
            resident((FOURIER_WIDTH, D_MODEL)),
            resident((Q_WIDTH, D_MODEL)),
            resident((D_MODEL, D_MODEL)),
            pl.BlockSpec((1, D_MODEL), const),
            pl.BlockSpec((D_MODEL, LANES), const),
            pl.BlockSpec((1, LANES), const),
            pl.BlockSpec((tm, tm), const),
        ],
        out_specs=[
            pl.BlockSpec((tm, D_MODEL), lambda i: (i, 0)),
            pl.BlockSpec((tm * SUBLANES, LANES), lambda i: (i, 0)),
            pl.BlockSpec((tm, LANES), lambda i: (i, 0)),
            pl.BlockSpec((8, LANES), const),
        ],
        out_shape=[
            jax.ShapeDtypeStruct((N_TOK, D_MODEL), F32),
            jax.ShapeDtypeStruct((N_TOK * SUBLANES, LANES), U32),
            jax.ShapeDtypeStruct((N_TOK, LANES), F32),
            jax.ShapeDtypeStruct((8, LANES), F32),
        ],
        scratch_shapes=[pltpu.VMEM((1, LANES), F32)],
        compiler_params=_cparams(("arbitrary",), 56),
        name="merge",
    )(yf, attn, gates, gates, x2, wfo, wao, wo, g2, wr, br, ltri)


def _expert_kernel(e_ref, live_ref, tok_cur_ref, tok_nxt_ref, dst_cur_ref,
                   h2p_hbm, wg_ref, bg_ref, wu_ref, bu_ref, wd_ref, bd_ref,
                   yslots_hbm, xu_scr, acc_scr, xs_scr, wgb, wub, wdb, gsem, ssem):
    del e_ref
    w = pl.program_id(0)
    f = pl.program_id(1)
    live = live_ref[w] > 0
    prev_live = (w > 0) & (live_ref[jnp.maximum(w - 1, 0)] > 0)
    cur = w & 1
    nxt = 1 - cur
    tile_rows = RT * SUBLANES

    def sub_rows(s):
        return pl.ds(pl.multiple_of(s * RT, RT), RT)

    def row_tile(row):
        return pl.ds(pl.multiple_of(row * SUBLANES, SUBLANES), SUBLANES)

    def gather_copy(tok_ref, slot, row):
        tok = tok_ref[0, 0, row]
        return pltpu.make_async_copy(h2p_hbm.at[row_tile(tok), :], xu_scr.at[slot, row_tile(row), :], gsem.at[slot])

    def scatter_copy(row):
        dst = dst_cur_ref[0, 0, row]
        return pltpu.make_async_copy(xu_scr.at[cur, row_tile(row), :], yslots_hbm.at[row_tile(dst), :], ssem)

    def for_all_rows(fn):
        def body(row, carry):
            fn(row)
            return carry
        lax.fori_loop(0, R_MAX, body, 0, unroll=8)

    def prefetch_next(s):
        base = s * RT + f * GATHER_PER_STEP
        for g in range(GATHER_PER_STEP):
            gather_copy(tok_nxt_ref, nxt, base + g).start(priority=g % 2)

    def cast_weights():
        wgb[...] = wg_ref[...].astype(BF16)
        wub[...] = wu_ref[...].astype(BF16)
        wdb[...] = wd_ref[...].astype(BF16)

    def stage_inputs(first_sub, n_sub, stage_row0):
        n_rows = n_sub * RT
        tiles = xu_scr.at[cur, pl.ds(pl.multiple_of(first_sub * tile_rows, tile_rows), n_sub * tile_rows), :]
        stage = xs_scr.at[pl.ds(stage_row0, n_rows), :]
        for j in range(SUBLANES):
            lo, hi = _unpack_halves(_load_row_tiles(tiles, j, n_rows))
            stage[:, j * LANES:(j + 1) * LANES] = lo.astype(BF16)
            stage[:, HALF + j * LANES:HALF + (j + 1) * LANES] = hi.astype(BF16)

    def activations(n_sub, stage_row0):
        x = xs_scr[pl.ds(stage_row0, n_sub * RT), :]
        g = _dot(x, wgb[...]) + bg_ref[...]
        u = _dot(x, wub[...]) + bu_ref[...]
        g = jnp.minimum(g, SWIGLU_LIMIT)
        u = jnp.clip(u, -SWIGLU_LIMIT, SWIGLU_LIMIT)
        act = (u + 1.0) * (g * (1.0 / (1.0 + jnp.exp(-SWIGLU_ALPHA * g))))
        return act.astype(BF16)

    def accumulate(j):
        n_rows = GROUP * RT
        rows = pl.ds(j * n_rows, n_rows)
        stage_inputs(j * GROUP, GROUP, 0)
        c = _dot(activations(GROUP, 0), wdb[...])
        acc_scr[rows, :] = jnp.where(f == 0, c, acc_scr[rows, :] + c)

    def finish(s):
        act = activations(1, (s % GROUP) * RT)
        rows = sub_rows(s)
        tiles = xu_scr.at[cur, pl.ds(s * tile_rows, tile_rows), :]
        for c in range(HALF // OUT_CHUNK):
            lo_cols = slice(c * OUT_CHUNK, (c + 1) * OUT_CHUNK)
            hi_cols = slice(HALF + c * OUT_CHUNK, HALF + (c + 1) * OUT_CHUNK)
            y_lo = acc_scr[rows, lo_cols] + _dot(act, wdb[:, lo_cols]) + bd_ref[:, lo_cols]
            y_hi = acc_scr[rows, hi_cols] + _dot(act, wdb[:, hi_cols]) + bd_ref[:, hi_cols]
            words = _pack_halves(y_lo, y_hi)
            for jj in range(OUT_CHUNK // LANES):
                j_tile = c * (OUT_CHUNK // LANES) + jj
                tiles[pl.ds(j_tile, RT, stride=SUBLANES), :] = words[:, jj * LANES:(jj + 1) * LANES]

    first_tile = pl.ds(0, SUBLANES)

    @pl.when(f == 0)
    def _():
        @pl.when(w == 0)
        def _():
            def zero(s, carry):
                acc_scr[sub_rows(s), :] = jnp.zeros((RT, D_MODEL), F32)
                xu_scr[1, pl.ds(pl.multiple_of(s * tile_rows, tile_rows), tile_rows), :] = (
                    jnp.zeros((tile_rows, LANES), U32))
                return carry
            lax.fori_loop(0, N_SUB, zero, 0)
            pad_rows = pl.ds(TOP_K * N_TOK * SUBLANES, R_MAX * SUBLANES)
            fill = pltpu.make_async_copy(xu_scr.at[1], yslots_hbm.at[pad_rows, :], ssem)
            fill.start()
            fill.wait()
            for_all_rows(lambda row: gather_copy(tok_cur_ref, cur, row).start())

        @pl.when((w == 0) | prev_live)
        def _():
            row_in = pltpu.make_async_copy(h2p_hbm.at[first_tile, :], xu_scr.at[cur, first_tile, :], gsem.at[cur])
            for_all_rows(lambda row: row_in.wait())

        @pl.when(prev_live)
        def _():
            row_out = pltpu.make_async_copy(xu_scr.at[cur, first_tile, :], yslots_hbm.at[first_tile, :], ssem)
            for_all_rows(lambda row: row_out.wait())

    @pl.when(live & (f < NF - 1))
    def _():
        cast_weights()
        for j in range(N_SUB // GROUP):
            accumulate(j)
            for i in range(GROUP):
                prefetch_next(j * GROUP + i)

    @pl.when(live & (f == NF - 1))
    def _():
        cast_weights()
        stage_inputs(0, 1, 0)
        for s in range(N_SUB):
            finish(s)
            if s + 1 < N_SUB:
                stage_inputs(s + 1, 1, ((s + 1) % GROUP) * RT)
            for r in range(RT):
                scatter_copy(s * RT + r).start(priority=r % 2)
            prefetch_next(s)


def _experts(item_e, item_live, item_tok, item_dst, h2p, wg, bg, wu, bu, wd, bd):
    def w_in_map(w, f, e, lv):
        return (e[w], 0, jnp.where(lv[w] > 0, f, NF - 1))

    def w_down_map(w, f, e, lv):
        return (e[w], jnp.where(lv[w] > 0, f, NF - 1), 0)

    def b_down_map(w, f, e, lv):
        return (e[w], 0, 0)

    grid_spec = pltpu.PrefetchScalarGridSpec(
        num_scalar_prefetch=2,
        grid=(jnp.sum(item_live) + 1, NF),
        in_specs=[
            pl.BlockSpec((1, 1, R_MAX), lambda w, f, e, lv: (w, 0, 0), memory_space=pltpu.SMEM),
            pl.BlockSpec((1, 1, R_MAX), lambda w, f, e, lv: (jnp.minimum(w + 1, N_ITEMS - 1), 0, 0),
                         memory_space=pltpu.SMEM),
            pl.BlockSpec((1, 1, R_MAX), lambda w, f, e, lv: (w, 0, 0), memory_space=pltpu.SMEM),
            pl.BlockSpec(memory_space=pl.ANY),
            pl.BlockSpec((None, D_MODEL, FC), w_in_map),
            pl.BlockSpec((None, 1, FC), w_in_map),
            pl.BlockSpec((None, D_MODEL, FC), w_in_map),
            pl.BlockSpec((None, 1, FC), w_in_map),
            pl.BlockSpec((None, FC, D_MODEL), w_down_map),
            pl.BlockSpec((None, 1, D_MODEL), b_down_map),
        ],
        out_specs=pl.BlockSpec(memory_space=pl.ANY),
        scratch_shapes=[
            pltpu.VMEM((2, R_MAX * SUBLANES, LANES), U32),
            pltpu.VMEM((R_MAX, D_MODEL), F32),
            pltpu.VMEM((GROUP * RT, D_MODEL), BF16),
            pltpu.VMEM((D_MODEL, FC), BF16),
            pltpu.VMEM((D_MODEL, FC), BF16),
            pltpu.VMEM((FC, D_MODEL), BF16),
            pltpu.SemaphoreType.DMA((2,)),
            pltpu.SemaphoreType.DMA(()),
        ],
    )
    return pl.pallas_call(
        _expert_kernel,
        grid_spec=grid_spec,
        out_shape=jax.ShapeDtypeStruct((SLOT_ROWS * SUBLANES, LANES), U32),
        compiler_params=_cparams(("arbitrary", "arbitrary"), 60),
        name="experts",
    )(item_e, item_live, item_tok, item_tok, item_dst, h2p, wg, bg, wu, bu, wd, bd)


def _combine_kernel(meta_ref, x1_ref, y0_ref, y1_ref, y2_ref, y3_ref, out_ref):
    gates = [meta_ref[:, 2 * TOP_K + k:2 * TOP_K + k + 1] for k in range(TOP_K)]
    for j in range(SUBLANES):
        lo_cols = slice(j * LANES, (j + 1) * LANES)
        hi_cols = slice(HALF + j * LANES, HALF + (j + 1) * LANES)
        lo_acc = x1_ref[:, lo_cols]
        hi_acc = x1_ref[:, hi_cols]
        for k, y_ref in enumerate((y0_ref, y1_ref, y2_ref, y3_ref)):
            lo, hi = _unpack_halves(_load_row_tiles(y_ref, j, TM_CB))
            lo_acc = lo_acc + gates[k] * lo
            hi_acc = hi_acc + gates[k] * hi
        out_ref[:, lo_cols] = lo_acc
        out_ref[:, hi_cols] = hi_acc


def _combine(meta, x1, y_slots):
    tm = TM_CB
    nblk = N_TOK // tm
    slot_specs = [pl.BlockSpec((tm * SUBLANES, LANES), lambda i, k=k: (k * nblk + i, 0)) for k in range(TOP_K)]
    return pl.pallas_call(
        _combine_kernel,
        grid=(nblk,),
        in_specs=[
            pl.BlockSpec((tm, LANES), lambda i: (i, 0)),
            pl.BlockSpec((tm, D_MODEL), lambda i: (i, 0)),
        ] + slot_specs,
        out_specs=pl.BlockSpec((tm, D_MODEL), lambda i: (i, 0)),
        out_shape=jax.ShapeDtypeStruct((N_TOK, D_MODEL), F32),
        compiler_params=_cparams(("arbitrary",), 32),
        name="combine",
    )(meta, x1, y_slots, y_slots, y_slots, y_slots)


def _dft_tables():
    r = DFT_RADIX
    k = np.arange(r)
    e64 = np.exp(2j * np.pi * np.outer(k, k) / r)
    e4096 = np.exp(2j * np.pi * np.outer(k, k) / SEQ)
    s = np.arange(SEQ)
    f1 = e64[:, s % r]
    g = e64[:, s // r] * e4096[:, s % r] / math.sqrt(SEQ)
    c = np.arange(FOURIER_GROUP_DIM)
    ang = 2.0 * np.pi * np.outer(c, c) / FOURIER_GROUP_DIM
    scale = 1.0 / math.sqrt(FOURIER_GROUP_DIM)
    as32 = lambda a: jnp.asarray(a.astype(np.float32))
    return (as32(f1.real), as32(f1.imag), as32(g.real), as32(g.imag),
            jnp.asarray((np.cos(ang) * scale).astype(np.float32)).astype(BF16),
            jnp.asarray((np.sin(ang) * scale).astype(np.float32)).astype(BF16))


def _head_selectors():
    i = np.arange(LANES)[:, None]
    j = np.arange(LANES)[None, :]
    out = np.zeros((4, LANES, LANES), np.float32)
    for src in range(2):
        for dst in range(2):
            out[2 * src + dst] = (i - HEAD_DIM * src == j - HEAD_DIM * dst) & (j // HEAD_DIM == dst)
    return jnp.asarray(out).astype(BF16)


def kernel(x, norm1_g, w_in, b_branch_gate, q_norm_g, k_norm_g, attn_sink, w_fourier_out, w_attn_out, w_o,
           norm2_g, w_router, b_router, w_gate_e, b_gate_e, w_up_e, b_up_e, w_down_e, b_down_e):
    b, s, d = x.shape
    assert (b, s, d) == (BATCH, SEQ, D_MODEL) and norm1_g.shape[0] == 1
    x2 = x.reshape(N_TOK, D_MODEL)

    heads_per_tile = TN_IN // HEAD_DIM
    ones_bd = jnp.asarray((np.arange(TN_IN)[:, None] // HEAD_DIM == np.arange(TN_IN)[None, :] // HEAD_DIM)
                          .astype(np.float32)).astype(BF16)
    u_f, q, kv, gates = _in_proj(
        x2, norm1_g[0].reshape(1, D_MODEL), w_in[0].astype(BF16), b_branch_gate[0].reshape(1, 2 * D_MODEL),
        jnp.tile(q_norm_g[0], heads_per_tile).reshape(1, TN_IN),
        jnp.tile(k_norm_g[0], heads_per_tile).reshape(1, TN_IN), ones_bd)
    y_f = _fourier(u_f, *_dft_tables())
    attn = _attention(attn_sink[0], q, kv, _head_selectors())

    wr = jnp.zeros((D_MODEL, LANES), BF16).at[:, :N_EXPERTS].set(w_router[0].astype(BF16))
    br = jnp.full((1, LANES), NEG_INF, F32).at[0, :N_EXPERTS].set(b_router[0])
    ltri = jnp.asarray(np.tril(np.ones((TM_MG, TM_MG), np.float32), -1)).astype(BF16)
    x1, h2p, meta, cnt = _merge(y_f, attn, gates, x2, w_fourier_out[0].astype(BF16), w_attn_out[0].astype(BF16),
                                w_o[0].astype(BF16), norm2_g[0].reshape(1, D_MODEL), wr, br, ltri)

    tables = _routing_tables(cnt[0, :N_EXPERTS].astype(jnp.int32), meta[:, 0:TOP_K].astype(jnp.int32),
                             meta[:, TOP_K:2 * TOP_K].astype(jnp.int32))
    y_slots = _experts(*tables, h2p,
                       w_gate_e[0], b_gate_e[0].reshape(N_EXPERTS, 1, D_EXPERT),
                       w_up_e[0], b_up_e[0].reshape(N_EXPERTS, 1, D_EXPERT),
                       w_down_e[0], b_down_e[0].reshape(N_EXPERTS, 1, D_MODEL))
    out = _combine(meta, x1, y_slots)
    return out.reshape(BATCH, SEQ, D_MODEL)


def _routing_tables(counts, top_idx, rank):
    padded = ((counts + RT - 1) // RT) * RT
    pend = jnp.cumsum(padded)
    pstart = pend - padded
    dest = pstart[top_idx] + rank
    n_assign = TOP_K * N_TOK
    row_assign = jnp.full((ROWS,), -1, jnp.int32).at[dest.reshape(n_assign)].set(
        jnp.arange(n_assign, dtype=jnp.int32), unique_indices=True)

    items_per_e = (padded + R_MAX - 1) // R_MAX
    it_end = jnp.cumsum(items_per_e)
    it_start = it_end - items_per_e
    total_items = it_end[-1]
    wi = jnp.arange(N_ITEMS, dtype=jnp.int32)
    e_w = jnp.minimum(jnp.searchsorted(it_end, wi, side="right"), N_EXPERTS - 1).astype(jnp.int32)
    j_w = wi - it_start[e_w]
    live = wi < total_items
    rows_w = jnp.clip(padded[e_w] - j_w * R_MAX, 0, R_MAX)
    e_last = e_w[jnp.maximum(total_items - 1, 0)]
    item_e = jnp.where(live, e_w, e_last).astype(jnp.int32)
    item_row0 = jnp.where(live, pstart[e_w] + j_w * R_MAX, 0).astype(jnp.int32)
    item_nsub = jnp.where(live, rows_w // RT, 0).astype(jnp.int32)
    item_live = live.astype(jnp.int32)
    r = jnp.arange(R_MAX, dtype=jnp.int32)
    src = jnp.minimum(item_row0[:, None] + r[None, :], ROWS - 1)
    item_rows = jnp.where(r[None, :] < (item_nsub * RT)[:, None], row_assign[src], -1)
    item_tok = (jnp.maximum(item_rows, 0) // TOP_K).reshape(N_ITEMS, 1, R_MAX)
    item_dst = jnp.where(item_rows >= 0, (item_rows % TOP_K) * N_TOK + item_rows // TOP_K,
                         TOP_K * N_TOK + r[None, :]).reshape(N_ITEMS, 1, R_MAX)
    return item_e, item_live, item_tok, item_dst
```

```python
import functools
import math

import jax
import jax.numpy as jnp
import numpy as np
from jax import lax
from jax.experimental import pallas as pl
from jax.experimental.pallas import tpu as pltpu

D_MODEL = 2048
BATCH = 4
SEQ = 4096
N_TOK = BATCH * SEQ
N_Q_HEADS = 16
N_KV_HEADS = 4
HEAD_DIM = 64
Q_WIDTH = N_Q_HEADS * HEAD_DIM
KV_WIDTH = N_KV_HEADS * HEAD_DIM
WINDOW = 128
N_FOURIER_GROUPS = 4
FOURIER_GROUP_DIM = 256
FOURIER_WIDTH = N_FOURIER_GROUPS * FOURIER_GROUP_DIM
IN_WIDTH = FOURIER_WIDTH + Q_WIDTH + 2 * KV_WIDTH + 2 * D_MODEL
N_EXPERTS = 32
TOP_K = 4
D_EXPERT = D_MODEL
SWIGLU_LIMIT = 7.0
SWIGLU_ALPHA = 1.702
RMS_EPS = 1e-6
NEG_INF = -1e30

F32 = jnp.float32
BF16 = jnp.bfloat16
U32 = jnp.uint32

V7X_VMEM_BYTES = 64 * 1024 * 1024
LANES = 128
SUBLANES = 8
MIB = 1024 * 1024

TM_IN = 512
TN_IN = 512
NORM_ROWS = 64
TT_F = 512
F_DIRECT = SEQ // TT_F // 2 + 1
DFT_RADIX = 64
TQ = 512
QB = 128
TM_MG = 256
RT = 256
R_MAX = 9 * RT
FC = 256
NF = D_EXPERT // FC
ROWS = TOP_K * N_TOK + N_EXPERTS * RT
N_ITEMS = (TOP_K * N_TOK + N_EXPERTS * (RT - 1) + N_EXPERTS * (R_MAX - RT)) // R_MAX + 1
TM_CB = 256
HALF = D_MODEL // 2
N_SUB = R_MAX // RT
GROUP = 3
OUT_CHUNK = 256
GATHER_PER_STEP = RT // NF
SLOT_ROWS = TOP_K * N_TOK + R_MAX


def _cparams(sem, vmem_mib):
    return pltpu.CompilerParams(dimension_semantics=sem, vmem_limit_bytes=vmem_mib * MIB)


def _dot(a, b):
    return jnp.dot(a, b, preferred_element_type=F32)


def _pack_halves(lo, hi):
    lo_bits = pltpu.bitcast(lo.astype(BF16).astype(F32), U32)
    hi_bits = pltpu.bitcast(hi.astype(BF16).astype(F32), U32)
    return (hi_bits & jnp.uint32(0xFFFF0000)) | (lo_bits >> jnp.uint32(16))


def _unpack_halves(words):
    lo = pltpu.bitcast(words << jnp.uint32(16), F32)
    hi = pltpu.bitcast(words & jnp.uint32(0xFFFF0000), F32)
    return lo, hi


def _store_row_tiles(tiles_ref, words):
    n_rows = words.shape[0]
    for j in range(SUBLANES):
        tiles_ref[pl.ds(j, n_rows, stride=SUBLANES), :] = words[:, j * LANES:(j + 1) * LANES]


def _load_row_tiles(tiles_ref, j, n_rows):
    return tiles_ref[pl.ds(j, n_rows, stride=SUBLANES), :]


def _in_proj_kernel(x_ref, g1_ref, w_ref, b_ref, qg_ref, kg_ref, ones_ref,
                    u_ref, q_ref, kv_ref, gate_ref, h_scr):
    def body(c, carry):
        rows = pl.ds(pl.multiple_of(c * NORM_ROWS, NORM_ROWS), NORM_ROWS)
        x = x_ref[rows, :]
        ms = jnp.mean(x * x, axis=-1, keepdims=True)
        h_scr[rows, :] = (x * lax.rsqrt(ms + RMS_EPS) * g1_ref[...]).astype(BF16)
        return carry
    lax.fori_loop(0, TM_IN // NORM_ROWS, body, 0)

    def head_norm(a, gain):
        sq = a * a
        hi = sq.astype(BF16)
        lo = (sq - hi.astype(F32)).astype(BF16)
        ssq = _dot(hi, ones_ref[...]) + _dot(lo, ones_ref[...])
        return a * lax.rsqrt(ssq * (1.0 / HEAD_DIM) + RMS_EPS) * gain

    h = h_scr[...]
    for c in range(IN_WIDTH // TN_IN):
        col0 = c * TN_IN
        acc = _dot(h, w_ref[:, col0:col0 + TN_IN])
        if col0 < FOURIER_WIDTH:
            u_ref[:, col0:col0 + TN_IN] = acc.astype(BF16)
        elif col0 < FOURIER_WIDTH + Q_WIDTH:
            o = col0 - FOURIER_WIDTH
            q_ref[:, o:o + TN_IN] = (head_norm(acc, qg_ref[...]) * (HEAD_DIM ** -0.5)).astype(BF16)
        elif col0 < FOURIER_WIDTH + Q_WIDTH + 2 * KV_WIDTH:
            lane = lax.broadcasted_iota(jnp.int32, acc.shape, 1)
            kv_ref[...] = jnp.where(lane < KV_WIDTH, head_norm(acc, kg_ref[...]), acc).astype(BF16)
        else:
            o = col0 - (FOURIER_WIDTH + Q_WIDTH + 2 * KV_WIDTH)
            z = acc + b_ref[:, o:o + TN_IN]
            gate_ref[:, o:o + TN_IN] = (1.0 / (1.0 + jnp.exp(-z))).astype(BF16)


def _in_proj(x2, g1, w_bf, bias, qg_t, kg_t, ones_bd):
    const = lambda i: (0, 0)
    row = lambda i: (i, 0)
    return pl.pallas_call(
        _in_proj_kernel,
        grid=(N_TOK // TM_IN,),
        in_specs=[
            pl.BlockSpec((TM_IN, D_MODEL), row),
            pl.BlockSpec((1, D_MODEL), const),
            pl.BlockSpec((D_MODEL, IN_WIDTH), const, pipeline_mode=pl.Buffered(1)),
            pl.BlockSpec((1, 2 * D_MODEL), const),
            pl.BlockSpec((1, TN_IN), const),
            pl.BlockSpec((1, TN_IN), const),
            pl.BlockSpec((TN_IN, TN_IN), const),
        ],
        out_specs=[
            pl.BlockSpec((TM_IN, FOURIER_WIDTH), row),
            pl.BlockSpec((TM_IN, Q_WIDTH), row),
            pl.BlockSpec((TM_IN, 2 * KV_WIDTH), row),
            pl.BlockSpec((TM_IN, 2 * D_MODEL), row),
        ],
        out_shape=[
            jax.ShapeDtypeStruct((N_TOK, FOURIER_WIDTH), BF16),
            jax.ShapeDtypeStruct((N_TOK, Q_WIDTH), BF16),
            jax.ShapeDtypeStruct((N_TOK, 2 * KV_WIDTH), BF16),
            jax.ShapeDtypeStruct((N_TOK, 2 * D_MODEL), BF16),
        ],
        scratch_shapes=[pltpu.VMEM((TM_IN, D_MODEL), BF16)],
        compiler_params=_cparams(("arbitrary",), 60),
        name="in_proj",
    )(x2, g1, w_bf, bias, qg_t, kg_t, ones_bd)


def _fourier_kernel(u_ref, f1r_ref, f1i_ref, gr_ref, gi_ref, cc_ref, sc_ref, rev_ref, y_ref, cs_scr, ss_scr, pq_scr):
    t = pl.program_id(0)
    b = pl.program_id(1)
    nt = SEQ // TT_F

    @pl.when(t < F_DIRECT)
    def _():
        @pl.when(b == 0)
        def _():
            gr = gr_ref[...]
            gi = gi_ref[...]
            for a in range(TT_F // DFT_RADIX):
                f1r = f1r_ref[a:a + 1, :]
                f1i = f1i_ref[a:a + 1, :]
                rows = slice(a * DFT_RADIX, (a + 1) * DFT_RADIX)
                cs_scr[rows, :] = (f1r * gr - f1i * gi).astype(BF16)
                ss_scr[rows, :] = (f1r * gi + f1i * gr).astype(BF16)

        u = u_ref[...]
        a_seq = _dot(cs_scr[...], u)
        b_seq = _dot(ss_scr[...], u)
        for g in range(N_FOURIER_GROUPS):
            cols = slice(g * FOURIER_GROUP_DIM, (g + 1) * FOURIER_GROUP_DIM)
            p = _dot(a_seq[:, cols].astype(BF16), cc_ref[...])
            q = _dot(b_seq[:, cols].astype(BF16), sc_ref[...])
            y_ref[:, cols] = (p - q).astype(BF16)

            @pl.when(t < nt - F_DIRECT + 1)
            def _():
                pq_scr[b, t, :, cols] = (p + q).astype(BF16)

    @pl.when(t >= F_DIRECT)
    def _():
        src = jnp.concatenate([pq_scr[b, nt - 1 - t], pq_scr[b, nt - t]], axis=0)
        y_ref[...] = _dot(rev_ref[...], src).astype(BF16)


def _fourier(u, f1r, f1i, gr, gi, cc, sc, rev):
    nt = SEQ // TT_F
    f1_rows = TT_F // DFT_RADIX
    const = lambda t, b: (0, 0)
    direct = lambda t: jnp.minimum(t, F_DIRECT - 1)
    return pl.pallas_call(
        _fourier_kernel,
        grid=(nt, BATCH),
        in_specs=[
            pl.BlockSpec((SEQ, FOURIER_WIDTH), lambda t, b: (jnp.where(t < F_DIRECT, b, BATCH - 1), 0)),
            pl.BlockSpec((f1_rows, SEQ), lambda t, b: (direct(t), 0)),
            pl.BlockSpec((f1_rows, SEQ), lambda t, b: (direct(t), 0)),
            pl.BlockSpec((DFT_RADIX, SEQ), const),
            pl.BlockSpec((DFT_RADIX, SEQ), const),
            pl.BlockSpec((FOURIER_GROUP_DIM, FOURIER_GROUP_DIM), const),
            pl.BlockSpec((FOURIER_GROUP_DIM, FOURIER_GROUP_DIM), const),
            pl.BlockSpec((TT_F, 2 * TT_F), const),
        ],
        out_specs=pl.BlockSpec((TT_F, FOURIER_WIDTH), lambda t, b: (b * nt + t, 0)),
        out_shape=jax.ShapeDtypeStruct((N_TOK, FOURIER_WIDTH), BF16),
        scratch_shapes=[pltpu.VMEM((TT_F, SEQ), BF16), pltpu.VMEM((TT_F, SEQ), BF16),
                        pltpu.VMEM((BATCH, nt - F_DIRECT + 1, TT_F, FOURIER_WIDTH), BF16)],
        compiler_params=_cparams(("arbitrary", "arbitrary"), 60),
        name="fourier",
    )(u, f1r, f1i, gr, gi, cc, sc, rev)


def _mirror_rows():
    rev = np.zeros((TT_F, 2 * TT_F), np.float32)
    rev[0, TT_F] = 1.0
    rev[np.arange(1, TT_F), TT_F - np.arange(1, TT_F)] = 1.0
    return jnp.asarray(rev).astype(BF16)


def _alibi_slope(h):
    return float(2.0 ** (-8.0 * (h + 1) / N_Q_HEADS))


def _attention_kernel(sink_ref, q_ref, prev_ref, cur_ref, next_ref, sel_ref, o_ref, kz_scr, vz_scr):
    t = pl.program_id(1)
    band = jnp.concatenate([prev_ref[...], cur_ref[...], next_ref[...]], axis=0)
    for g in range(N_KV_HEADS):
        c, half = divmod(g, 2)
        kcol = band[:, c * LANES:(c + 1) * LANES]
        vcol = band[:, KV_WIDTH + c * LANES:KV_WIDTH + (c + 1) * LANES]
        for d in range(2):
            sel = sel_ref[half * 2 + d]
            kz_scr[2 * g + d] = jnp.transpose(_dot(kcol, sel)).astype(BF16)
            vz_scr[2 * g + d] = _dot(vcol, sel).astype(BF16)

    row = lax.broadcasted_iota(jnp.int32, (2 * QB, 3 * QB), 0)
    col = lax.broadcasted_iota(jnp.int32, (2 * QB, 3 * QB), 1)
    absrel = jnp.abs((row & (QB - 1)) - col + QB)
    absrel_f = absrel.astype(F32)
    top = lax.broadcasted_iota(jnp.int32, (2 * QB, 1), 0) < QB
    for i in range(TQ // QB):
        q_rows = slice(i * QB, (i + 1) * QB)
        k_rows = slice(i * QB, (i + 3) * QB)
        kpos = t * TQ + (i - 1) * QB + col
        mask = (absrel <= WINDOW) & (kpos >= 0) & (kpos < SEQ)
        for g in range(N_KV_HEADS):
            cols = [slice((2 * g + p) * LANES, (2 * g + p + 1) * LANES) for p in range(2)]
            qq = jnp.concatenate([q_ref[q_rows, cols[0]], q_ref[q_rows, cols[1]]], axis=0)
            out = jnp.zeros((2 * QB, LANES), F32)
            for d in range(2):
                h_top, h_bot = 4 * g + d, 4 * g + 2 + d
                slope = jnp.where(top, _alibi_slope(h_top), _alibi_slope(h_bot))
                sink = jnp.where(top, sink_ref[h_top], sink_ref[h_bot])
                s = _dot(qq, kz_scr[2 * g + d, :, k_rows])
                s = jnp.where(mask, s - slope * absrel_f, NEG_INF)
                m = jnp.maximum(jnp.max(s, axis=-1, keepdims=True), sink)
                pr = jnp.exp(s - m)
                den = jnp.sum(pr, axis=-1, keepdims=True) + jnp.exp(sink - m)
                out = out + _dot(pr.astype(BF16), vz_scr[2 * g + d, k_rows, :]) / den
            o_ref[q_rows, cols[0]] = out[:QB].astype(BF16)
            o_ref[q_rows, cols[1]] = out[QB:].astype(BF16)


def _attention(sink, q, kv, sel):
    nt = SEQ // TQ
    nb = SEQ // QB
    r = TQ // QB
    grid_spec = pltpu.PrefetchScalarGridSpec(
        num_scalar_prefetch=1,
        grid=(BATCH, nt),
        in_specs=[
            pl.BlockSpec((TQ, Q_WIDTH), lambda b, t, s: (b * nt + t, 0)),
            pl.BlockSpec((QB, 2 * KV_WIDTH), lambda b, t, s: (b * nb + jnp.maximum(t * r - 1, 0), 0)),
            pl.BlockSpec((TQ, 2 * KV_WIDTH), lambda b, t, s: (b * nt + t, 0)),
            pl.BlockSpec((QB, 2 * KV_WIDTH), lambda b, t, s: (b * nb + jnp.minimum(t * r + r, nb - 1), 0)),
            pl.BlockSpec((4, LANES, LANES), lambda b, t, s: (0, 0, 0)),
        ],
        out_specs=pl.BlockSpec((TQ, Q_WIDTH), lambda b, t, s: (b * nt + t, 0)),
        scratch_shapes=[pltpu.VMEM((2 * N_KV_HEADS, LANES, TQ + 2 * QB), BF16),
                        pltpu.VMEM((2 * N_KV_HEADS, TQ + 2 * QB, LANES), BF16)],
    )
    return pl.pallas_call(
        _attention_kernel,
        grid_spec=grid_spec,
        out_shape=jax.ShapeDtypeStruct((N_TOK, Q_WIDTH), BF16),
        compiler_params=_cparams(("arbitrary", "arbitrary"), 32),
        name="attention",
    )(sink, q, kv, kv, kv, sel)


def _merge_kernel(yf_ref, at_ref, gf_ref, ga_ref, x_ref, wfo_ref, wao_ref, wo_ref, g2_ref, wr_ref, br_ref,
                  ltri_ref, x1_ref, h2p_ref, meta_ref, cnt_ref, base_scr):
    i = pl.program_id(0)

    @pl.when(i == 0)
    def _():
        base_scr[...] = jnp.zeros_like(base_scr)

    yf = _dot(yf_ref[...], wfo_ref[...])
    ya = _dot(at_ref[...], wao_ref[...])
    merged = gf_ref[...].astype(F32) * yf + ga_ref[...].astype(F32) * ya
    x1 = x_ref[...] + _dot(merged.astype(BF16), wo_ref[...])
    x1_ref[...] = x1
    ms = jnp.mean(x1 * x1, axis=-1, keepdims=True)
    h2 = x1 * lax.rsqrt(ms + RMS_EPS) * g2_ref[...]
    _store_row_tiles(h2p_ref, _pack_halves(h2[:, :HALF], h2[:, HALF:]))
    logits = _dot(h2.astype(BF16), wr_ref[...]) + br_ref[...]

    lane = lax.broadcasted_iota(jnp.int32, logits.shape, 1)
    lane_f = lane.astype(F32)
    vals = logits
    top_v, top_i, onehots = [], [], []
    for _ in range(TOP_K):
        m = jnp.max(vals, axis=-1, keepdims=True)
        idx = jnp.min(jnp.where(vals == m, lane_f, float(LANES)), axis=-1, keepdims=True)
        oh = lane_f == idx
        top_v.append(m)
        top_i.append(idx)
        onehots.append(oh)
        vals = jnp.where(oh, -jnp.inf, vals)
    exps = [jnp.exp(v - top_v[0]) for v in top_v]
    den = exps[0] + exps[1] + exps[2] + exps[3]
    gates = [e / den for e in exps]

    cnt = sum(oh.astype(F32) for oh in onehots)
    prefix = _dot(ltri_ref[...], cnt.astype(BF16))
    tot = base_scr[...] + prefix
    ranks = [jnp.sum(jnp.where(oh, tot, 0.0), axis=-1, keepdims=True) for oh in onehots]
    base_scr[...] = base_scr[...] + jnp.sum(cnt, axis=0, keepdims=True)
    cnt_ref[...] = jnp.broadcast_to(base_scr[...], cnt_ref.shape)

    meta = jnp.zeros(logits.shape, F32)
    for k in range(TOP_K):
        meta = jnp.where(lane == k, top_i[k], meta)
        meta = jnp.where(lane == TOP_K + k, ranks[k], meta)
        meta = jnp.where(lane == 2 * TOP_K + k, gates[k], meta)
    meta_ref[...] = meta


def _merge(yf, attn, gates, x2, wfo, wao, wo, g2, wr, br, ltri):
    tm = TM_MG
    const = lambda i: (0, 0)
    resident = functools.partial(pl.BlockSpec, index_map=const, pipeline_mode=pl.Buffered(1))
    return pl.pallas_call(
        _merge_kernel,
        grid=(N_TOK // tm,),
        in_specs=[
            pl.BlockSpec((tm, FOURIER_WIDTH), lambda i: (i, 0)),
            pl.BlockSpec((tm, Q_WIDTH), lambda i: (i, 0)),
            pl.BlockSpec((tm, D_MODEL), lambda i: (i, 0)),
            pl.BlockSpec((tm, D_MODEL), lambda i: (i, 1)),
            pl.BlockSpec((tm, D_MODEL), lambda i: (i, 0)),
            resident((FOURIER_WIDTH, D_MODEL)),
            resident((Q_WIDTH, D_MODEL)),
            resident((D_MODEL, D_MODEL)),
            pl.BlockSpec((1, D_MODEL), const),
            pl.BlockSpec((D_MODEL, LANES), const),
            pl.BlockSpec((1, LANES), const),
            pl.BlockSpec((tm, tm), const),
        ],
        out_specs=[
            pl.BlockSpec((tm, D_MODEL), lambda i: (i, 0)),
            pl.BlockSpec((tm * SUBLANES, LANES), lambda i: (i, 0)),
            pl.BlockSpec((tm, LANES), lambda i: (i, 0)),
            pl.BlockSpec((8, LANES), const),
        ],
        out_shape=[
            jax.ShapeDtypeStruct((N_TOK, D_MODEL), F32),
            jax.ShapeDtypeStruct((N_TOK * SUBLANES, LANES), U32),
            jax.ShapeDtypeStruct((N_TOK, LANES), F32),
            jax.ShapeDtypeStruct((8, LANES), F32),
        ],
        scratch_shapes=[pltpu.VMEM((1, LANES), F32)],
        compiler_params=_cparams(("arbitrary",), 56),
        name="merge",
    )(yf, attn, gates, gates, x2, wfo, wao, wo, g2, wr, br, ltri)


def _expert_kernel(e_ref, live_ref, tok_cur_ref, tok_nxt_ref, dst_cur_ref,
                   h2p_hbm, wg_ref, bg_ref, wu_ref, bu_ref, wd_ref, bd_ref,
                   yslots_hbm, xu_scr, acc_scr, xs_scr, wgb, wub, wdb, gsem, ssem):
    del e_ref
    w = pl.program_id(0)
    f = pl.program_id(1)
    live = live_ref[w] > 0
    prev_live = (w > 0) & (live_ref[jnp.maximum(w - 1, 0)] > 0)
    cur = w & 1
    nxt = 1 - cur
    tile_rows = RT * SUBLANES

    def sub_rows(s):
        return pl.ds(pl.multiple_of(s * RT, RT), RT)

    def row_tile(row):
        return pl.ds(pl.multiple_of(row * SUBLANES, SUBLANES), SUBLANES)

    def gather_copy(tok_ref, slot, row):
        tok = tok_ref[0, 0, row]
        return pltpu.make_async_copy(h2p_hbm.at[row_tile(tok), :], xu_scr.at[slot, row_tile(row), :], gsem.at[slot])

    def scatter_copy(row):
        dst = dst_cur_ref[0, 0, row]
        return pltpu.make_async_copy(xu_scr.at[cur, row_tile(row), :], yslots_hbm.at[row_tile(dst), :], ssem)

    def for_all_rows(fn):
        def body(row, carry):
            fn(row)
            return carry
        lax.fori_loop(0, R_MAX, body, 0, unroll=8)

    def prefetch_next(s):
        base = s * RT + f * GATHER_PER_STEP
        for g in range(GATHER_PER_STEP):
            gather_copy(tok_nxt_ref, nxt, base + g).start(priority=g % 2)

    def cast_weights():
        wgb[...] = wg_ref[...].astype(BF16)
        wub[...] = wu_ref[...].astype(BF16)
        wdb[...] = wd_ref[...].astype(BF16)

    def stage_inputs(first_sub, n_sub, stage_row0):
        n_rows = n_sub * RT
        tiles = xu_scr.at[cur, pl.ds(pl.multiple_of(first_sub * tile_rows, tile_rows), n_sub * tile_rows), :]
        stage = xs_scr.at[pl.ds(stage_row0, n_rows), :]
        for j in range(SUBLANES):
            lo, hi = _unpack_halves(_load_row_tiles(tiles, j, n_rows))
            stage[:, j * LANES:(j + 1) * LANES] = lo.astype(BF16)
            stage[:, HALF + j * LANES:HALF + (j + 1) * LANES] = hi.astype(BF16)

    def activations(n_sub, stage_row0):
        x = xs_scr[pl.ds(stage_row0, n_sub * RT), :]
        g = _dot(x, wgb[...]) + bg_ref[...]
        u = _dot(x, wub[...]) + bu_ref[...]
        g = jnp.minimum(g, SWIGLU_LIMIT)
        u = jnp.clip(u, -SWIGLU_LIMIT, SWIGLU_LIMIT)
        act = (u + 1.0) * (g * (1.0 / (1.0 + jnp.exp(-SWIGLU_ALPHA * g))))
        return act.astype(BF16)

    def accumulate(j):
        n_rows = GROUP * RT
        rows = pl.ds(j * n_rows, n_rows)
        stage_inputs(j * GROUP, GROUP, 0)
        c = _dot(activations(GROUP, 0), wdb[...])
        acc_scr[rows, :] = jnp.where(f == 0, c, acc_scr[rows, :] + c)

    def finish(s):
        act = activations(1, (s % GROUP) * RT)
        rows = sub_rows(s)
        tiles = xu_scr.at[cur, pl.ds(s * tile_rows, tile_rows), :]
        for c in range(HALF // OUT_CHUNK):
            lo_cols = slice(c * OUT_CHUNK, (c + 1) * OUT_CHUNK)
            hi_cols = slice(HALF + c * OUT_CHUNK, HALF + (c + 1) * OUT_CHUNK)
            y_lo = acc_scr[rows, lo_cols] + _dot(act, wdb[:, lo_cols]) + bd_ref[:, lo_cols]
            y_hi = acc_scr[rows, hi_cols] + _dot(act, wdb[:, hi_cols]) + bd_ref[:, hi_cols]
            words = _pack_halves(y_lo, y_hi)
            for jj in range(OUT_CHUNK // LANES):
                j_tile = c * (OUT_CHUNK // LANES) + jj
                tiles[pl.ds(j_tile, RT, stride=SUBLANES), :] = words[:, jj * LANES:(jj + 1) * LANES]

    first_tile = pl.ds(0, SUBLANES)

    @pl.when(f == 0)
    def _():
        @pl.when(w == 0)
        def _():
            def zero(s, carry):
                acc_scr[sub_rows(s), :] = jnp.zeros((RT, D_MODEL), F32)
                xu_scr[1, pl.ds(pl.multiple_of(s * tile_rows, tile_rows), tile_rows), :] = (
                    jnp.zeros((tile_rows, LANES), U32))
                return carry
            lax.fori_loop(0, N_SUB, zero, 0)
            pad_rows = pl.ds(TOP_K * N_TOK * SUBLANES, R_MAX * SUBLANES)
            fill = pltpu.make_async_copy(xu_scr.at[1], yslots_hbm.at[pad_rows, :], ssem)
            fill.start()
            fill.wait()
            for_all_rows(lambda row: gather_copy(tok_cur_ref, cur, row).start())

        @pl.when((w == 0) | prev_live)
        def _():
            row_in = pltpu.make_async_copy(h2p_hbm.at[first_tile, :], xu_scr.at[cur, first_tile, :], gsem.at[cur])
            for_all_rows(lambda row: row_in.wait())

        @pl.when(prev_live)
        def _():
            row_out = pltpu.make_async_copy(xu_scr.at[cur, first_tile, :], yslots_hbm.at[first_tile, :], ssem)
            for_all_rows(lambda row: row_out.wait())

    @pl.when(live & (f < NF - 1))
    def _():
        cast_weights()
        for j in range(N_SUB // GROUP):
            accumulate(j)
            for i in range(GROUP):
                prefetch_next(j * GROUP + i)

    @pl.when(live & (f == NF - 1))
    def _():
        cast_weights()
        stage_inputs(0, 1, 0)
        for s in range(N_SUB):
            finish(s)
            if s + 1 < N_SUB:
                stage_inputs(s + 1, 1, ((s + 1) % GROUP) * RT)
            for r in range(RT):
                scatter_copy(s * RT + r).start(priority=r % 2)
            prefetch_next(s)


def _experts(item_e, item_live, item_tok, item_dst, h2p, wg, bg, wu, bu, wd, bd):
    def w_in_map(w, f, e, lv):
        return (e[w], 0, jnp.where(lv[w] > 0, f, NF - 1))

    def w_down_map(w, f, e, lv):
        return (e[w], jnp.where(lv[w] > 0, f, NF - 1), 0)

    def b_down_map(w, f, e, lv):
        return (e[w], 0, 0)

    grid_spec = pltpu.PrefetchScalarGridSpec(
        num_scalar_prefetch=2,
        grid=(jnp.sum(item_live) + 1, NF),
        in_specs=[
            pl.BlockSpec((1, 1, R_MAX), lambda w, f, e, lv: (w, 0, 0), memory_space=pltpu.SMEM),
            pl.BlockSpec((1, 1, R_MAX), lambda w, f, e, lv: (jnp.minimum(w + 1, N_ITEMS - 1), 0, 0),
                         memory_space=pltpu.SMEM),
            pl.BlockSpec((1, 1, R_MAX), lambda w, f, e, lv: (w, 0, 0), memory_space=pltpu.SMEM),
            pl.BlockSpec(memory_space=pl.ANY),
            pl.BlockSpec((None, D_MODEL, FC), w_in_map),
            pl.BlockSpec((None, 1, FC), w_in_map),
            pl.BlockSpec((None, D_MODEL, FC), w_in_map),
            pl.BlockSpec((None, 1, FC), w_in_map),
            pl.BlockSpec((None, FC, D_MODEL), w_down_map),
            pl.BlockSpec((None, 1, D_MODEL), b_down_map),
        ],
        out_specs=pl.BlockSpec(memory_space=pl.ANY),
        scratch_shapes=[
            pltpu.VMEM((2, R_MAX * SUBLANES, LANES), U32),
            pltpu.VMEM((R_MAX, D_MODEL), F32),
            pltpu.VMEM((GROUP * RT, D_MODEL), BF16),
            pltpu.VMEM((D_MODEL, FC), BF16),
            pltpu.VMEM((D_MODEL, FC), BF16),
            pltpu.VMEM((FC, D_MODEL), BF16),
            pltpu.SemaphoreType.DMA((2,)),
            pltpu.SemaphoreType.DMA(()),
        ],
    )
    return pl.pallas_call(
        _expert_kernel,
        grid_spec=grid_spec,
        out_shape=jax.ShapeDtypeStruct((SLOT_ROWS * SUBLANES, LANES), U32),
        compiler_params=_cparams(("arbitrary", "arbitrary"), 60),
        name="experts",
    )(item_e, item_live, item_tok, item_tok, item_dst, h2p, wg, bg, wu, bu, wd, bd)


def _combine_kernel(meta_ref, x1_ref, y0_ref, y1_ref, y2_ref, y3_ref, out_ref):
    gates = [meta_ref[:, 2 * TOP_K + k:2 * TOP_K + k + 1] for k in range(TOP_K)]
    for j in range(SUBLANES):
        lo_cols = slice(j * LANES, (j + 1) * LANES)
        hi_cols = slice(HALF + j * LANES, HALF + (j + 1) * LANES)
        lo_acc = x1_ref[:, lo_cols]
        hi_acc = x1_ref[:, hi_cols]
        for k, y_ref in enumerate((y0_ref, y1_ref, y2_ref, y3_ref)):
            lo, hi = _unpack_halves(_load_row_tiles(y_ref, j, TM_CB))
            lo_acc = lo_acc + gates[k] * lo
            hi_acc = hi_acc + gates[k] * hi
        out_ref[:, lo_cols] = lo_acc
        out_ref[:, hi_cols] = hi_acc


def _combine(meta, x1, y_slots):
    tm = TM_CB
    nblk = N_TOK // tm
    slot_specs = [pl.BlockSpec((tm * SUBLANES, LANES), lambda i, k=k: (k * nblk + i, 0)) for k in range(TOP_K)]
    return pl.pallas_call(
        _combine_kernel,
        grid=(nblk,),
        in_specs=[
            pl.BlockSpec((tm, LANES), lambda i: (i, 0)),
            pl.BlockSpec((tm, D_MODEL), lambda i: (i, 0)),
        ] + slot_specs,
        out_specs=pl.BlockSpec((tm, D_MODEL), lambda i: (i, 0)),
        out_shape=jax.ShapeDtypeStruct((N_TOK, D_MODEL), F32),
        compiler_params=_cparams(("arbitrary",), 32),
        name="combine",
    )(meta, x1, y_slots, y_slots, y_slots, y_slots)


def _dft_tables():
    r = DFT_RADIX
    k = np.arange(r)
    e64 = np.exp(2j * np.pi * np.outer(k, k) / r)
    e4096 = np.exp(2j * np.pi * np.outer(k, k) / SEQ)
    s = np.arange(SEQ)
    f1 = e64[:, s % r]
    g = e64[:, s // r] * e4096[:, s % r] / math.sqrt(SEQ)
    c = np.arange(FOURIER_GROUP_DIM)
    ang = 2.0 * np.pi * np.outer(c, c) / FOURIER_GROUP_DIM
    scale = 1.0 / math.sqrt(FOURIER_GROUP_DIM)
    as32 = lambda a: jnp.asarray(a.astype(np.float32))
    return (as32(f1.real), as32(f1.imag), as32(g.real), as32(g.imag),
            jnp.asarray((np.cos(ang) * scale).astype(np.float32)).astype(BF16),
            jnp.asarray((np.sin(ang) * scale).astype(np.float32)).astype(BF16))


def _head_selectors():
    i = np.arange(LANES)[:, None]
    j = np.arange(LANES)[None, :]
    out = np.zeros((4, LANES, LANES), np.float32)
    for src in range(2):
        for dst in range(2):
            out[2 * src + dst] = (i - HEAD_DIM * src == j - HEAD_DIM * dst) & (j // HEAD_DIM == dst)
    return jnp.asarray(out).astype(BF16)


def kernel(x, norm1_g, w_in, b_branch_gate, q_norm_g, k_norm_g, attn_sink, w_fourier_out, w_attn_out, w_o,
           norm2_g, w_router, b_router, w_gate_e, b_gate_e, w_up_e, b_up_e, w_down_e, b_down_e):
    b, s, d = x.shape
    assert (b, s, d) == (BATCH, SEQ, D_MODEL) and norm1_g.shape[0] == 1
    x2 = x.reshape(N_TOK, D_MODEL)

    heads_per_tile = TN_IN // HEAD_DIM
    ones_bd = jnp.asarray((np.arange(TN_IN)[:, None] // HEAD_DIM == np.arange(TN_IN)[None, :] // HEAD_DIM)
                          .astype(np.float32)).astype(BF16)
    u_f, q, kv, gates = _in_proj(
        x2, norm1_g[0].reshape(1, D_MODEL), w_in[0].astype(BF16), b_branch_gate[0].reshape(1, 2 * D_MODEL),
        jnp.tile(q_norm_g[0], heads_per_tile).reshape(1, TN_IN),
        jnp.tile(k_norm_g[0], heads_per_tile).reshape(1, TN_IN), ones_bd)
    y_f = _fourier(u_f, *_dft_tables(), _mirror_rows())
    attn = _attention(attn_sink[0], q, kv, _head_selectors())

    wr = jnp.zeros((D_MODEL, LANES), BF16).at[:, :N_EXPERTS].set(w_router[0].astype(BF16))
    br = jnp.full((1, LANES), NEG_INF, F32).at[0, :N_EXPERTS].set(b_router[0])
    ltri = jnp.asarray(np.tril(np.ones((TM_MG, TM_MG), np.float32), -1)).astype(BF16)
    x1, h2p, meta, cnt = _merge(y_f, attn, gates, x2, w_fourier_out[0].astype(BF16), w_attn_out[0].astype(BF16),
                                w_o[0].astype(BF16), norm2_g[0].reshape(1, D_MODEL), wr, br, ltri)

    tables = _routing_tables(cnt[0, :N_EXPERTS].astype(jnp.int32), meta[:, 0:TOP_K].astype(jnp.int32),
                             meta[:, TOP_K:2 * TOP_K].astype(jnp.int32))
    y_slots = _experts(*tables, h2p,
                       w_gate_e[0], b_gate_e[0].reshape(N_EXPERTS, 1, D_EXPERT),
                       w_up_e[0], b_up_e[0].reshape(N_EXPERTS, 1, D_EXPERT),
                       w_down_e[0], b_down_e[0].reshape(N_EXPERTS, 1, D_MODEL))
    out = _combine(meta, x1, y_slots)
    return out.reshape(BATCH, SEQ, D_MODEL)


def _routing_tables(counts, top_idx, rank):
    padded = ((counts + RT - 1) // RT) * RT
    pend = jnp.cumsum(padded)
    pstart = pend - padded
    dest = pstart[top_idx] + rank
    n_assign = TOP_K * N_TOK
    row_assign = jnp.full((ROWS,), -1, jnp.int32).at[dest.reshape(n_assign)].set(
        jnp.arange(n_assign, dtype=jnp.int32), unique_indices=True)

    items_per_e = (padded + R_MAX - 1) // R_MAX
    it_end = jnp.cumsum(items_per_e)
    it_start = it_end - items_per_e
    total_items = it_end[-1]
    wi = jnp.arange(N_ITEMS, dtype=jnp.int32)
    e_w = jnp.minimum(jnp.searchsorted(it_end, wi, side="right"), N_EXPERTS - 1).astype(jnp.int32)
    j_w = wi - it_start[e_w]
    live = wi < total_items
    rows_w = jnp.clip(padded[e_w] - j_w * R_MAX, 0, R_MAX)
    e_last = e_w[jnp.maximum(total_items - 1, 0)]
    item_e = jnp.where(live, e_w, e_last).astype(jnp.int32)
    item_row0 = jnp.where(live, pstart[e_w] + j_w * R_MAX, 0).astype(jnp.int32)
    item_nsub = jnp.where(live, rows_w // RT, 0).astype(jnp.int32)
    item_live = live.astype(jnp.int32)
    r = jnp.arange(R_MAX, dtype=jnp.int32)
    src = jnp.minimum(item_row0[:, None] + r[None, :], ROWS - 1)
    item_rows = jnp.where(r[None, :] < (item_nsub * RT)[:, None], row_assign[src], -1)
    item_tok = (jnp.maximum(item_rows, 0) // TOP_K).reshape(N_ITEMS, 1, R_MAX)
    item_dst = jnp.where(item_rows >= 0, (item_rows % TOP_K) * N_TOK + item_rows // TOP_K,
                         TOP_K * N_TOK + r[None, :]).reshape(N_ITEMS, 1, R_MAX)
    return item_e, item_live, item_tok, item_dst
```

```python
import functools
import math

import jax
import jax.numpy as jnp
import numpy as np
from jax import lax
from jax.experimental import pallas as pl
from jax.experimental.pallas import tpu as pltpu

D_MODEL = 2048
BATCH = 4
SEQ = 4096
N_TOK = BATCH * SEQ
N_Q_HEADS = 16
N_KV_HEADS = 4
HEAD_DIM = 64
Q_WIDTH = N_Q_HEADS * HEAD_DIM
KV_WIDTH = N_KV_HEADS * HEAD_DIM
WINDOW = 128
N_FOURIER_GROUPS = 4
FOURIER_GROUP_DIM = 256
FOURIER_WIDTH = N_FOURIER_GROUPS * FOURIER_GROUP_DIM
IN_WIDTH = FOURIER_WIDTH + Q_WIDTH + 2 * KV_WIDTH + 2 * D_MODEL
N_EXPERTS = 32
TOP_K = 4
D_EXPERT = D_MODEL
SWIGLU_LIMIT = 7.0
SWIGLU_ALPHA = 1.702
RMS_EPS = 1e-6
NEG_INF = -1e30

F32 = jnp.float32
BF16 = jnp.bfloat16
U32 = jnp.uint32

V7X_VMEM_BYTES = 64 * 1024 * 1024
LANES = 128
SUBLANES = 8
MIB = 1024 * 1024

TM_IN = 512
TN_IN = 512
NORM_ROWS = 64
TT_F = 512
F_DIRECT = SEQ // TT_F // 2 + 1
DFT_RADIX = 64
TQ = 512
QB = 128
TM_MG = 512
MG_SLAB = 256
RT = 256
R_MAX = 9 * RT
FC = 256
NF = D_EXPERT // FC
ROWS = TOP_K * N_TOK + N_EXPERTS * RT
N_ITEMS = (TOP_K * N_TOK + N_EXPERTS * (RT - 1) + N_EXPERTS * (R_MAX - RT)) // R_MAX + 1
TM_CB = 256
HALF = D_MODEL // 2
N_SUB = R_MAX // RT
GROUP = 3
OUT_CHUNK = 256
GATHER_PER_STEP = RT // NF
SLOT_ROWS = TOP_K * N_TOK + R_MAX


def _cparams(sem, vmem_mib):
    return pltpu.CompilerParams(dimension_semantics=sem, vmem_limit_bytes=vmem_mib * MIB)


def _dot(a, b):
    return jnp.dot(a, b, preferred_element_type=F32)


def _pack_halves(lo, hi):
    lo_bits = pltpu.bitcast(lo.astype(BF16).astype(F32), U32)
    hi_bits = pltpu.bitcast(hi.astype(BF16).astype(F32), U32)
    return (hi_bits & jnp.uint32(0xFFFF0000)) | (lo_bits >> jnp.uint32(16))


def _unpack_halves(words):
    lo = pltpu.bitcast(words << jnp.uint32(16), F32)
    hi = pltpu.bitcast(words & jnp.uint32(0xFFFF0000), F32)
    return lo, hi


def _store_row_tiles(tiles_ref, words):
    n_rows = words.shape[0]
    for j in range(SUBLANES):
        tiles_ref[pl.ds(j, n_rows, stride=SUBLANES), :] = words[:, j * LANES:(j + 1) * LANES]


def _load_row_tiles(tiles_ref, j, n_rows):
    return tiles_ref[pl.ds(j, n_rows, stride=SUBLANES), :]


def _in_proj_kernel(x_ref, g1_ref, w_ref, b_ref, qg_ref, kg_ref, ones_ref,
                    u_ref, q_ref, kv_ref, gate_ref, h_scr):
    def body(c, carry):
        rows = pl.ds(pl.multiple_of(c * NORM_ROWS, NORM_ROWS), NORM_ROWS)
        x = x_ref[rows, :]
        ms = jnp.mean(x * x, axis=-1, keepdims=True)
        h_scr[rows, :] = (x * lax.rsqrt(ms + RMS_EPS) * g1_ref[...]).astype(BF16)
        return carry
    lax.fori_loop(0, TM_IN // NORM_ROWS, body, 0)

    def head_norm(a, gain):
        sq = a * a
        hi = sq.astype(BF16)
        lo = (sq - hi.astype(F32)).astype(BF16)
        ssq = _dot(hi, ones_ref[...]) + _dot(lo, ones_ref[...])
        return a * lax.rsqrt(ssq * (1.0 / HEAD_DIM) + RMS_EPS) * gain

    h = h_scr[...]
    for c in range(IN_WIDTH // TN_IN):
        col0 = c * TN_IN
        acc = _dot(h, w_ref[:, col0:col0 + TN_IN])
        if col0 < FOURIER_WIDTH:
            u_ref[:, col0:col0 + TN_IN] = acc.astype(BF16)
        elif col0 < FOURIER_WIDTH + Q_WIDTH:
            o = col0 - FOURIER_WIDTH
            q_ref[:, o:o + TN_IN] = (head_norm(acc, qg_ref[...]) * (HEAD_DIM ** -0.5)).astype(BF16)
        elif col0 < FOURIER_WIDTH + Q_WIDTH + 2 * KV_WIDTH:
            lane = lax.broadcasted_iota(jnp.int32, acc.shape, 1)
            kv_ref[...] = jnp.where(lane < KV_WIDTH, head_norm(acc, kg_ref[...]), acc).astype(BF16)
        else:
            o = col0 - (FOURIER_WIDTH + Q_WIDTH + 2 * KV_WIDTH)
            z = acc + b_ref[:, o:o + TN_IN]
            gate_ref[:, o:o + TN_IN] = (1.0 / (1.0 + jnp.exp(-z))).astype(BF16)


def _in_proj(x2, g1, w_bf, bias, qg_t, kg_t, ones_bd):
    const = lambda i: (0, 0)
    row = lambda i: (i, 0)
    return pl.pallas_call(
        _in_proj_kernel,
        grid=(N_TOK // TM_IN,),
        in_specs=[
            pl.BlockSpec((TM_IN, D_MODEL), row),
            pl.BlockSpec((1, D_MODEL), const),
            pl.BlockSpec((D_MODEL, IN_WIDTH), const, pipeline_mode=pl.Buffered(1)),
            pl.BlockSpec((1, 2 * D_MODEL), const),
            pl.BlockSpec((1, TN_IN), const),
            pl.BlockSpec((1, TN_IN), const),
            pl.BlockSpec((TN_IN, TN_IN), const),
        ],
        out_specs=[
            pl.BlockSpec((TM_IN, FOURIER_WIDTH), row),
            pl.BlockSpec((TM_IN, Q_WIDTH), row),
            pl.BlockSpec((TM_IN, 2 * KV_WIDTH), row),
            pl.BlockSpec((TM_IN, 2 * D_MODEL), row),
        ],
        out_shape=[
            jax.ShapeDtypeStruct((N_TOK, FOURIER_WIDTH), BF16),
            jax.ShapeDtypeStruct((N_TOK, Q_WIDTH), BF16),
            jax.ShapeDtypeStruct((N_TOK, 2 * KV_WIDTH), BF16),
            jax.ShapeDtypeStruct((N_TOK, 2 * D_MODEL), BF16),
        ],
        scratch_shapes=[pltpu.VMEM((TM_IN, D_MODEL), BF16)],
        compiler_params=_cparams(("arbitrary",), 60),
        name="in_proj",
    )(x2, g1, w_bf, bias, qg_t, kg_t, ones_bd)


def _fourier_kernel(u_ref, f1r_ref, f1i_ref, gr_ref, gi_ref, cc_ref, sc_ref, rev_ref, y_ref, cs_scr, ss_scr, pq_scr):
    t = pl.program_id(0)
    b = pl.program_id(1)
    nt = SEQ // TT_F

    @pl.when(t < F_DIRECT)
    def _():
        @pl.when(b == 0)
        def _():
            gr = gr_ref[...]
            gi = gi_ref[...]
            for a in range(TT_F // DFT_RADIX):
                f1r = f1r_ref[a:a + 1, :]
                f1i = f1i_ref[a:a + 1, :]
                rows = slice(a * DFT_RADIX, (a + 1) * DFT_RADIX)
                cs_scr[rows, :] = (f1r * gr - f1i * gi).astype(BF16)
                ss_scr[rows, :] = (f1r * gi + f1i * gr).astype(BF16)

        u = u_ref[...]
        a_seq = _dot(cs_scr[...], u)
        b_seq = _dot(ss_scr[...], u)
        for g in range(N_FOURIER_GROUPS):
            cols = slice(g * FOURIER_GROUP_DIM, (g + 1) * FOURIER_GROUP_DIM)
            p = _dot(a_seq[:, cols].astype(BF16), cc_ref[...])
            q = _dot(b_seq[:, cols].astype(BF16), sc_ref[...])
            y_ref[:, cols] = (p - q).astype(BF16)

            @pl.when(t < nt - F_DIRECT + 1)
            def _():
                pq_scr[b, t, :, cols] = (p + q).astype(BF16)

    @pl.when(t >= F_DIRECT)
    def _():
        src = jnp.concatenate([pq_scr[b, nt - 1 - t], pq_scr[b, nt - t]], axis=0)
        y_ref[...] = _dot(rev_ref[...], src).astype(BF16)


def _fourier(u, f1r, f1i, gr, gi, cc, sc, rev):
    nt = SEQ // TT_F
    f1_rows = TT_F // DFT_RADIX
    const = lambda t, b: (0, 0)
    direct = lambda t: jnp.minimum(t, F_DIRECT - 1)
    return pl.pallas_call(
        _fourier_kernel,
        grid=(nt, BATCH),
        in_specs=[
            pl.BlockSpec((SEQ, FOURIER_WIDTH), lambda t, b: (jnp.where(t < F_DIRECT, b, BATCH - 1), 0)),
            pl.BlockSpec((f1_rows, SEQ), lambda t, b: (direct(t), 0)),
            pl.BlockSpec((f1_rows, SEQ), lambda t, b: (direct(t), 0)),
            pl.BlockSpec((DFT_RADIX, SEQ), const),
            pl.BlockSpec((DFT_RADIX, SEQ), const),
            pl.BlockSpec((FOURIER_GROUP_DIM, FOURIER_GROUP_DIM), const),
            pl.BlockSpec((FOURIER_GROUP_DIM, FOURIER_GROUP_DIM), const),
            pl.BlockSpec((TT_F, 2 * TT_F), const),
        ],
        out_specs=pl.BlockSpec((TT_F, FOURIER_WIDTH), lambda t, b: (b * nt + t, 0)),
        out_shape=jax.ShapeDtypeStruct((N_TOK, FOURIER_WIDTH), BF16),
        scratch_shapes=[pltpu.VMEM((TT_F, SEQ), BF16), pltpu.VMEM((TT_F, SEQ), BF16),
                        pltpu.VMEM((BATCH, nt - F_DIRECT + 1, TT_F, FOURIER_WIDTH), BF16)],
        compiler_params=_cparams(("arbitrary", "arbitrary"), 60),
        name="fourier",
    )(u, f1r, f1i, gr, gi, cc, sc, rev)


def _mirror_rows():
    rev = np.zeros((TT_F, 2 * TT_F), np.float32)
    rev[0, TT_F] = 1.0
    rev[np.arange(1, TT_F), TT_F - np.arange(1, TT_F)] = 1.0
    return jnp.asarray(rev).astype(BF16)


def _alibi_slope(h):
    return float(2.0 ** (-8.0 * (h + 1) / N_Q_HEADS))


def _attention_kernel(sink_ref, q_ref, prev_ref, cur_ref, next_ref, sel_ref, o_ref, kz_scr, vz_scr):
    t = pl.program_id(1)
    band = jnp.concatenate([prev_ref[...], cur_ref[...], next_ref[...]], axis=0)
    for g in range(N_KV_HEADS):
        c, half = divmod(g, 2)
        kcol = band[:, c * LANES:(c + 1) * LANES]
        vcol = band[:, KV_WIDTH + c * LANES:KV_WIDTH + (c + 1) * LANES]
        for d in range(2):
            sel = sel_ref[half * 2 + d]
            kz_scr[2 * g + d] = jnp.transpose(_dot(kcol, sel)).astype(BF16)
            vz_scr[2 * g + d] = _dot(vcol, sel).astype(BF16)

    row = lax.broadcasted_iota(jnp.int32, (2 * QB, 3 * QB), 0)
    col = lax.broadcasted_iota(jnp.int32, (2 * QB, 3 * QB), 1)
    absrel = jnp.abs((row & (QB - 1)) - col + QB)
    absrel_f = absrel.astype(F32)
    top = lax.broadcasted_iota(jnp.int32, (2 * QB, 1), 0) < QB
    for i in range(TQ // QB):
        q_rows = slice(i * QB, (i + 1) * QB)
        k_rows = slice(i * QB, (i + 3) * QB)
        kpos = t * TQ + (i - 1) * QB + col
        mask = (absrel <= WINDOW) & (kpos >= 0) & (kpos < SEQ)
        for g in range(N_KV_HEADS):
            cols = [slice((2 * g + p) * LANES, (2 * g + p + 1) * LANES) for p in range(2)]
            qq = jnp.concatenate([q_ref[q_rows, cols[0]], q_ref[q_rows, cols[1]]], axis=0)
            out = jnp.zeros((2 * QB, LANES), F32)
            for d in range(2):
                h_top, h_bot = 4 * g + d, 4 * g + 2 + d
                slope = jnp.where(top, _alibi_slope(h_top), _alibi_slope(h_bot))
                sink = jnp.where(top, sink_ref[h_top], sink_ref[h_bot])
                s = _dot(qq, kz_scr[2 * g + d, :, k_rows])
                s = jnp.where(mask, s - slope * absrel_f, NEG_INF)
                m = jnp.maximum(jnp.max(s, axis=-1, keepdims=True), sink)
                pr = jnp.exp(s - m)
                den = jnp.sum(pr, axis=-1, keepdims=True) + jnp.exp(sink - m)
                out = out + _dot(pr.astype(BF16), vz_scr[2 * g + d, k_rows, :]) / den
            o_ref[q_rows, cols[0]] = out[:QB].astype(BF16)
            o_ref[q_rows, cols[1]] = out[QB:].astype(BF16)


def _attention(sink, q, kv, sel):
    nt = SEQ // TQ
    nb = SEQ // QB
    r = TQ // QB
    grid_spec = pltpu.PrefetchScalarGridSpec(
        num_scalar_prefetch=1,
        grid=(BATCH, nt),
        in_specs=[
            pl.BlockSpec((TQ, Q_WIDTH), lambda b, t, s: (b * nt + t, 0)),
            pl.BlockSpec((QB, 2 * KV_WIDTH), lambda b, t, s: (b * nb + jnp.maximum(t * r - 1, 0), 0)),
            pl.BlockSpec((TQ, 2 * KV_WIDTH), lambda b, t, s: (b * nt + t, 0)),
            pl.BlockSpec((QB, 2 * KV_WIDTH), lambda b, t, s: (b * nb + jnp.minimum(t * r + r, nb - 1), 0)),
            pl.BlockSpec((4, LANES, LANES), lambda b, t, s: (0, 0, 0)),
        ],
        out_specs=pl.BlockSpec((TQ, Q_WIDTH), lambda b, t, s: (b * nt + t, 0)),
        scratch_shapes=[pltpu.VMEM((2 * N_KV_HEADS, LANES, TQ + 2 * QB), BF16),
                        pltpu.VMEM((2 * N_KV_HEADS, TQ + 2 * QB, LANES), BF16)],
    )
    return pl.pallas_call(
        _attention_kernel,
        grid_spec=grid_spec,
        out_shape=jax.ShapeDtypeStruct((N_TOK, Q_WIDTH), BF16),
        compiler_params=_cparams(("arbitrary", "arbitrary"), 32),
        name="attention",
    )(sink, q, kv, kv, kv, sel)


def _merge_kernel(yf_ref, at_ref, gf_ref, ga_ref, x_ref, wfo_ref, wao_ref, wo_ref, g2_ref, wr_ref, br_ref,
                  ltri_ref, x1_ref, h2p_ref, meta_ref, cnt_ref, base_scr):
    i = pl.program_id(0)

    @pl.when(i == 0)
    def _():
        base_scr[...] = jnp.zeros_like(base_scr)

    base = base_scr[...]
    for s in range(TM_MG // MG_SLAB):
        rows = slice(s * MG_SLAB, (s + 1) * MG_SLAB)
        yf = _dot(yf_ref[rows, :], wfo_ref[...])
        ya = _dot(at_ref[rows, :], wao_ref[...])
        merged = gf_ref[rows, :].astype(F32) * yf + ga_ref[rows, :].astype(F32) * ya
        x1 = x_ref[rows, :] + _dot(merged.astype(BF16), wo_ref[...])
        x1_ref[rows, :] = x1
        ms = jnp.mean(x1 * x1, axis=-1, keepdims=True)
        h2 = x1 * lax.rsqrt(ms + RMS_EPS) * g2_ref[...]
        _store_row_tiles(h2p_ref.at[pl.ds(s * MG_SLAB * SUBLANES, MG_SLAB * SUBLANES), :],
                         _pack_halves(h2[:, :HALF], h2[:, HALF:]))
        logits = _dot(h2.astype(BF16), wr_ref[...]) + br_ref[...]

        lane = lax.broadcasted_iota(jnp.int32, logits.shape, 1)
        lane_f = lane.astype(F32)
        vals = logits
        top_v, top_i, onehots = [], [], []
        for _ in range(TOP_K):
            m = jnp.max(vals, axis=-1, keepdims=True)
            idx = jnp.min(jnp.where(vals == m, lane_f, float(LANES)), axis=-1, keepdims=True)
            oh = lane_f == idx
            top_v.append(m)
            top_i.append(idx)
            onehots.append(oh)
            vals = jnp.where(oh, -jnp.inf, vals)
        exps = [jnp.exp(v - top_v[0]) for v in top_v]
        den = exps[0] + exps[1] + exps[2] + exps[3]
        gates = [e / den for e in exps]

        cnt = sum(oh.astype(F32) for oh in onehots)
        prefix = _dot(ltri_ref[...], cnt.astype(BF16))
        tot = base + prefix
        ranks = [jnp.sum(jnp.where(oh, tot, 0.0), axis=-1, keepdims=True) for oh in onehots]
        base = base + jnp.sum(cnt, axis=0, keepdims=True)

        meta = jnp.zeros(logits.shape, F32)
        for k in range(TOP_K):
            meta = jnp.where(lane == k, top_i[k], meta)
            meta = jnp.where(lane == TOP_K + k, ranks[k], meta)
            meta = jnp.where(lane == 2 * TOP_K + k, gates[k], meta)
        meta_ref[rows, :] = meta
    base_scr[...] = base
    cnt_ref[...] = jnp.broadcast_to(base, cnt_ref.shape)


def _merge(yf, attn, gates, x2, wfo, wao, wo, g2, wr, br, ltri):
    tm = TM_MG
    const = lambda i: (0, 0)
    resident = functools.partial(pl.BlockSpec, index_map=const, pipeline_mode=pl.Buffered(1))
    return pl.pallas_call(
        _merge_kernel,
        grid=(N_TOK // tm,),
        in_specs=[
            pl.BlockSpec((tm, FOURIER_WIDTH), lambda i: (i, 0)),
            pl.BlockSpec((tm, Q_WIDTH), lambda i: (i, 0)),
            pl.BlockSpec((tm, D_MODEL), lambda i: (i, 0)),
            pl.BlockSpec((tm, D_MODEL), lambda i: (i, 1)),
            pl.BlockSpec((tm, D_MODEL), lambda i: (i, 0)),
            resident((FOURIER_WIDTH, D_MODEL)),
            resident((Q_WIDTH, D_MODEL)),
            resident((D_MODEL, D_MODEL)),
            pl.BlockSpec((1, D_MODEL), const),
            pl.BlockSpec((D_MODEL, LANES), const),
            pl.BlockSpec((1, LANES), const),
            pl.BlockSpec((MG_SLAB, MG_SLAB), const),
        ],
        out_specs=[
            pl.BlockSpec((tm, D_MODEL), lambda i: (i, 0)),
            pl.BlockSpec((tm * SUBLANES, LANES), lambda i: (i, 0)),
            pl.BlockSpec((tm, LANES), lambda i: (i, 0)),
            pl.BlockSpec((8, LANES), const),
        ],
        out_shape=[
            jax.ShapeDtypeStruct((N_TOK, D_MODEL), F32),
            jax.ShapeDtypeStruct((N_TOK * SUBLANES, LANES), U32),
            jax.ShapeDtypeStruct((N_TOK, LANES), F32),
            jax.ShapeDtypeStruct((8, LANES), F32),
        ],
        scratch_shapes=[pltpu.VMEM((1, LANES), F32)],
        compiler_params=_cparams(("arbitrary",), 56),
        name="merge",
    )(yf, attn, gates, gates, x2, wfo, wao, wo, g2, wr, br, ltri)


def _expert_kernel(e_ref, live_ref, tok_cur_ref, tok_nxt_ref, dst_cur_ref,
                   h2p_hbm, wg_ref, bg_ref, wu_ref, bu_ref, wd_ref, bd_ref,
                   yslots_hbm, xu_scr, acc_scr, xs_scr, wgb, wub, wdb, gsem, ssem):
    del e_ref
    w = pl.program_id(0)
    f = pl.program_id(1)
    live = live_ref[w] > 0
    prev_live = (w > 0) & (live_ref[jnp.maximum(w - 1, 0)] > 0)
    cur = w & 1
    nxt = 1 - cur
    tile_rows = RT * SUBLANES

    def sub_rows(s):
        return pl.ds(pl.multiple_of(s * RT, RT), RT)

    def row_tile(row):
        return pl.ds(pl.multiple_of(row * SUBLANES, SUBLANES), SUBLANES)

    def gather_copy(tok_ref, slot, row):
        tok = tok_ref[0, 0, row]
        return pltpu.make_async_copy(h2p_hbm.at[row_tile(tok), :], xu_scr.at[slot, row_tile(row), :], gsem.at[slot])

    def scatter_copy(row):
        dst = dst_cur_ref[0, 0, row]
        return pltpu.make_async_copy(xu_scr.at[cur, row_tile(row), :], yslots_hbm.at[row_tile(dst), :], ssem)

    def for_all_rows(fn):
        def body(row, carry):
            fn(row)
            return carry
        lax.fori_loop(0, R_MAX, body, 0, unroll=8)

    def prefetch_next(s):
        base = s * RT + f * GATHER_PER_STEP
        for g in range(GATHER_PER_STEP):
            gather_copy(tok_nxt_ref, nxt, base + g).start(priority=g % 2)

    def cast_weights():
        wgb[...] = wg_ref[...].astype(BF16)
        wub[...] = wu_ref[...].astype(BF16)
        wdb[...] = wd_ref[...].astype(BF16)

    def stage_inputs(first_sub, n_sub, stage_row0):
        n_rows = n_sub * RT
        tiles = xu_scr.at[cur, pl.ds(pl.multiple_of(first_sub * tile_rows, tile_rows), n_sub * tile_rows), :]
        stage = xs_scr.at[pl.ds(stage_row0, n_rows), :]
        for j in range(SUBLANES):
            lo, hi = _unpack_halves(_load_row_tiles(tiles, j, n_rows))
            stage[:, j * LANES:(j + 1) * LANES] = lo.astype(BF16)
            stage[:, HALF + j * LANES:HALF + (j + 1) * LANES] = hi.astype(BF16)

    def activations(n_sub, stage_row0):
        x = xs_scr[pl.ds(stage_row0, n_sub * RT), :]
        g = _dot(x, wgb[...]) + bg_ref[...]
        u = _dot(x, wub[...]) + bu_ref[...]
        g = jnp.minimum(g, SWIGLU_LIMIT)
        u = jnp.clip(u, -SWIGLU_LIMIT, SWIGLU_LIMIT)
        act = (u + 1.0) * (g * (1.0 / (1.0 + jnp.exp(-SWIGLU_ALPHA * g))))
        return act.astype(BF16)

    def accumulate(j):
        n_rows = GROUP * RT
        rows = pl.ds(j * n_rows, n_rows)
        stage_inputs(j * GROUP, GROUP, 0)
        c = _dot(activations(GROUP, 0), wdb[...])
        acc_scr[rows, :] = jnp.where(f == 0, c, acc_scr[rows, :] + c)

    def finish(s):
        act = activations(1, (s % GROUP) * RT)
        rows = sub_rows(s)
        tiles = xu_scr.at[cur, pl.ds(s * tile_rows, tile_rows), :]
        for c in range(HALF // OUT_CHUNK):
            lo_cols = slice(c * OUT_CHUNK, (c + 1) * OUT_CHUNK)
            hi_cols = slice(HALF + c * OUT_CHUNK, HALF + (c + 1) * OUT_CHUNK)
            y_lo = acc_scr[rows, lo_cols] + _dot(act, wdb[:, lo_cols]) + bd_ref[:, lo_cols]
            y_hi = acc_scr[rows, hi_cols] + _dot(act, wdb[:, hi_cols]) + bd_ref[:, hi_cols]
            words = _pack_halves(y_lo, y_hi)
            for jj in range(OUT_CHUNK // LANES):
                j_tile = c * (OUT_CHUNK // LANES) + jj
                tiles[pl.ds(j_tile, RT, stride=SUBLANES), :] = words[:, jj * LANES:(jj + 1) * LANES]

    first_tile = pl.ds(0, SUBLANES)

    @pl.when(f == 0)
    def _():
        @pl.when(w == 0)
        def _():
            def zero(s, carry):
                acc_scr[sub_rows(s), :] = jnp.zeros((RT, D_MODEL), F32)
                xu_scr[1, pl.ds(pl.multiple_of(s * tile_rows, tile_rows), tile_rows), :] = (
                    jnp.zeros((tile_rows, LANES), U32))
                return carry
            lax.fori_loop(0, N_SUB, zero, 0)
            pad_rows = pl.ds(TOP_K * N_TOK * SUBLANES, R_MAX * SUBLANES)
            fill = pltpu.make_async_copy(xu_scr.at[1], yslots_hbm.at[pad_rows, :], ssem)
            fill.start()
            fill.wait()
            for_all_rows(lambda row: gather_copy(tok_cur_ref, cur, row).start())

        @pl.when((w == 0) | prev_live)
        def _():
            row_in = pltpu.make_async_copy(h2p_hbm.at[first_tile, :], xu_scr.at[cur, first_tile, :], gsem.at[cur])
            for_all_rows(lambda row: row_in.wait())

        @pl.when(prev_live)
        def _():
            row_out = pltpu.make_async_copy(xu_scr.at[cur, first_tile, :], yslots_hbm.at[first_tile, :], ssem)
            for_all_rows(lambda row: row_out.wait())

    @pl.when(live & (f < NF - 1))
    def _():
        cast_weights()
        for j in range(N_SUB // GROUP):
            accumulate(j)
            for i in range(GROUP):
                prefetch_next(j * GROUP + i)

    @pl.when(live & (f == NF - 1))
    def _():
        cast_weights()
        stage_inputs(0, 1, 0)
        for s in range(N_SUB):
            finish(s)
            if s + 1 < N_SUB:
                stage_inputs(s + 1, 1, ((s + 1) % GROUP) * RT)
            for r in range(RT):
                scatter_copy(s * RT + r).start(priority=r % 2)
            prefetch_next(s)


def _experts(item_e, item_live, item_tok, item_dst, h2p, wg, bg, wu, bu, wd, bd):
    def w_in_map(w, f, e, lv):
        return (e[w], 0, jnp.where(lv[w] > 0, f, NF - 1))

    def w_down_map(w, f, e, lv):
        return (e[w], jnp.where(lv[w] > 0, f, NF - 1), 0)

    def b_down_map(w, f, e, lv):
        return (e[w], 0, 0)

    grid_spec = pltpu.PrefetchScalarGridSpec(
        num_scalar_prefetch=2,
        grid=(jnp.sum(item_live) + 1, NF),
        in_specs=[
            pl.BlockSpec((1, 1, R_MAX), lambda w, f, e, lv: (w, 0, 0), memory_space=pltpu.SMEM),
            pl.BlockSpec((1, 1, R_MAX), lambda w, f, e, lv: (jnp.minimum(w + 1, N_ITEMS - 1), 0, 0),
                         memory_space=pltpu.SMEM),
            pl.BlockSpec((1, 1, R_MAX), lambda w, f, e, lv: (w, 0, 0), memory_space=pltpu.SMEM),
            pl.BlockSpec(memory_space=pl.ANY),
            pl.BlockSpec((None, D_MODEL, FC), w_in_map),
            pl.BlockSpec((None, 1, FC), w_in_map),
            pl.BlockSpec((None, D_MODEL, FC), w_in_map),
            pl.BlockSpec((None, 1, FC), w_in_map),
            pl.BlockSpec((None, FC, D_MODEL), w_down_map),
            pl.BlockSpec((None, 1, D_MODEL), b_down_map),
        ],
        out_specs=pl.BlockSpec(memory_space=pl.ANY),
        scratch_shapes=[
            pltpu.VMEM((2, R_MAX * SUBLANES, LANES), U32),
            pltpu.VMEM((R_MAX, D_MODEL), F32),
            pltpu.VMEM((GROUP * RT, D_MODEL), BF16),
            pltpu.VMEM((D_MODEL, FC), BF16),
            pltpu.VMEM((D_MODEL, FC), BF16),
            pltpu.VMEM((FC, D_MODEL), BF16),
            pltpu.SemaphoreType.DMA((2,)),
            pltpu.SemaphoreType.DMA(()),
        ],
    )
    return pl.pallas_call(
        _expert_kernel,
        grid_spec=grid_spec,
        out_shape=jax.ShapeDtypeStruct((SLOT_ROWS * SUBLANES, LANES), U32),
        compiler_params=_cparams(("arbitrary", "arbitrary"), 60),
        name="experts",
    )(item_e, item_live, item_tok, item_tok, item_dst, h2p, wg, bg, wu, bu, wd, bd)


def _combine_kernel(meta_ref, x1_ref, y0_ref, y1_ref, y2_ref, y3_ref, out_ref):
    gates = [meta_ref[:, 2 * TOP_K + k:2 * TOP_K + k + 1] for k in range(TOP_K)]
    for j in range(SUBLANES):
        lo_cols = slice(j * LANES, (j + 1) * LANES)
        hi_cols = slice(HALF + j * LANES, HALF + (j + 1) * LANES)
        lo_acc = x1_ref[:, lo_cols]
        hi_acc = x1_ref[:, hi_cols]
        for k, y_ref in enumerate((y0_ref, y1_ref, y2_ref, y3_ref)):
            lo, hi = _unpack_halves(_load_row_tiles(y_ref, j, TM_CB))
            lo_acc = lo_acc + gates[k] * lo
            hi_acc = hi_acc + gates[k] * hi
        out_ref[:, lo_cols] = lo_acc
        out_ref[:, hi_cols] = hi_acc


def _combine(meta, x1, y_slots):
    tm = TM_CB
    nblk = N_TOK // tm
    slot_specs = [pl.BlockSpec((tm * SUBLANES, LANES), lambda i, k=k: (k * nblk + i, 0)) for k in range(TOP_K)]
    return pl.pallas_call(
        _combine_kernel,
        grid=(nblk,),
        in_specs=[
            pl.BlockSpec((tm, LANES), lambda i: (i, 0)),
            pl.BlockSpec((tm, D_MODEL), lambda i: (i, 0)),
        ] + slot_specs,
        out_specs=pl.BlockSpec((tm, D_MODEL), lambda i: (i, 0)),
        out_shape=jax.ShapeDtypeStruct((N_TOK, D_MODEL), F32),
        compiler_params=_cparams(("arbitrary",), 32),
        name="combine",
    )(meta, x1, y_slots, y_slots, y_slots, y_slots)


def _dft_tables():
    r = DFT_RADIX
    k = np.arange(r)
    e64 = np.exp(2j * np.pi * np.outer(k, k) / r)
    e4096 = np.exp(2j * np.pi * np.outer(k, k) / SEQ)
    s = np.arange(SEQ)
    f1 = e64[:, s % r]
    g = e64[:, s // r] * e4096[:, s % r] / math.sqrt(SEQ)
    c = np.arange(FOURIER_GROUP_DIM)
    ang = 2.0 * np.pi * np.outer(c, c) / FOURIER_GROUP_DIM
    scale = 1.0 / math.sqrt(FOURIER_GROUP_DIM)
    as32 = lambda a: jnp.asarray(a.astype(np.float32))
    return (as32(f1.real), as32(f1.imag), as32(g.real), as32(g.imag),
            jnp.asarray((np.cos(ang) * scale).astype(np.float32)).astype(BF16),
            jnp.asarray((np.sin(ang) * scale).astype(np.float32)).astype(BF16))


def _head_selectors():
    i = np.arange(LANES)[:, None]
    j = np.arange(LANES)[None, :]
    out = np.zeros((4, LANES, LANES), np.float32)
    for src in range(2):
        for dst in range(2):
            out[2 * src + dst] = (i - HEAD_DIM * src == j - HEAD_DIM * dst) & (j // HEAD_DIM == dst)
    return jnp.asarray(out).astype(BF16)


def kernel(x, norm1_g, w_in, b_branch_gate, q_norm_g, k_norm_g, attn_sink, w_fourier_out, w_attn_out, w_o,
           norm2_g, w_router, b_router, w_gate_e, b_gate_e, w_up_e, b_up_e, w_down_e, b_down_e):
    b, s, d = x.shape
    assert (b, s, d) == (BATCH, SEQ, D_MODEL) and norm1_g.shape[0] == 1
    x2 = x.reshape(N_TOK, D_MODEL)

    heads_per_tile = TN_IN // HEAD_DIM
    ones_bd = jnp.asarray((np.arange(TN_IN)[:, None] // HEAD_DIM == np.arange(TN_IN)[None, :] // HEAD_DIM)
                          .astype(np.float32)).astype(BF16)
    u_f, q, kv, gates = _in_proj(
        x2, norm1_g[0].reshape(1, D_MODEL), w_in[0].astype(BF16), b_branch_gate[0].reshape(1, 2 * D_MODEL),
        jnp.tile(q_norm_g[0], heads_per_tile).reshape(1, TN_IN),
        jnp.tile(k_norm_g[0], heads_per_tile).reshape(1, TN_IN), ones_bd)
    y_f = _fourier(u_f, *_dft_tables(), _mirror_rows())
    attn = _attention(attn_sink[0], q, kv, _head_selectors())

    wr = jnp.zeros((D_MODEL, LANES), BF16).at[:, :N_EXPERTS].set(w_router[0].astype(BF16))
    br = jnp.full((1, LANES), NEG_INF, F32).at[0, :N_EXPERTS].set(b_router[0])
    ltri = jnp.asarray(np.tril(np.ones((MG_SLAB, MG_SLAB), np.float32), -1)).astype(BF16)
    x1, h2p, meta, cnt = _merge(y_f, attn, gates, x2, w_fourier_out[0].astype(BF16), w_attn_out[0].astype(BF16),
                                w_o[0].astype(BF16), norm2_g[0].reshape(1, D_MODEL), wr, br, ltri)

    tables = _routing_tables(cnt[0, :N_EXPERTS].astype(jnp.int32), meta[:, 0:TOP_K].astype(jnp.int32),
                             meta[:, TOP_K:2 * TOP_K].astype(jnp.int32))
    y_slots = _experts(*tables, h2p,
                       w_gate_e[0], b_gate_e[0].reshape(N_EXPERTS, 1, D_EXPERT),
                       w_up_e[0], b_up_e[0].reshape(N_EXPERTS, 1, D_EXPERT),
                       w_down_e[0], b_down_e[0].reshape(N_EXPERTS, 1, D_MODEL))
    out = _combine(meta, x1, y_slots)
    return out.reshape(BATCH, SEQ, D_MODEL)


def _routing_tables(counts, top_idx, rank):
    padded = ((counts + RT - 1) // RT) * RT
    pend = jnp.cumsum(padded)
    pstart = pend - padded
    dest = pstart[top_idx] + rank
    n_assign = TOP_K * N_TOK
    row_assign = jnp.full((ROWS,), -1, jnp.int32).at[dest.reshape(n_assign)].set(
        jnp.arange(n_assign, dtype=jnp.int32), unique_indices=True)

    items_per_e = (padded + R_MAX - 1) // R_MAX
    it_end = jnp.cumsum(items_per_e)
    it_start = it_end - items_per_e
    total_items = it_end[-1]
    wi = jnp.arange(N_ITEMS, dtype=jnp.int32)
    e_w = jnp.minimum(jnp.searchsorted(it_end, wi, side="right"), N_EXPERTS - 1).astype(jnp.int32)
    j_w = wi - it_start[e_w]
    live = wi < total_items
    rows_w = jnp.clip(padded[e_w] - j_w * R_MAX, 0, R_MAX)
    e_last = e_w[jnp.maximum(total_items - 1, 0)]
    item_e = jnp.where(live, e_w, e_last).astype(jnp.int32)
    item_row0 = jnp.where(live, pstart[e_w] + j_w * R_MAX, 0).astype(jnp.int32)
    item_nsub = jnp.where(live, rows_w // RT, 0).astype(jnp.int32)
    item_live = live.astype(jnp.int32)
    r = jnp.arange(R_MAX, dtype=jnp.int32)
    src = jnp.minimum(item_row0[:, None] + r[None, :], ROWS - 1)
    item_rows = jnp.where(r[None, :] < (item_nsub * RT)[:, None], row_assign[src], -1)
    item_tok = (jnp.maximum(item_rows, 0) // TOP_K).reshape(N_ITEMS, 1, R_MAX)
    item_dst = jnp.where(item_rows >= 0, (item_rows % TOP_K) * N_TOK + item_rows // TOP_K,
                         TOP_K * N_TOK + r[None, :]).reshape(N_ITEMS, 1, R_MAX)
    return item_e, item_live, item_tok, item_dst
```

```python
import functools
import math

import jax
import jax.numpy as jnp
import numpy as np
from jax import lax
from jax.experimental import pallas as pl
from jax.experimental.pallas import tpu as pltpu

D_MODEL = 2048
BATCH = 4
SEQ = 4096
N_TOK = BATCH * SEQ
N_Q_HEADS = 16
N_KV_HEADS = 4
HEAD_DIM = 64
Q_WIDTH = N_Q_HEADS * HEAD_DIM
KV_WIDTH = N_KV_HEADS * HEAD_DIM
WINDOW = 128
N_FOURIER_GROUPS = 4
FOURIER_GROUP_DIM = 256
FOURIER_WIDTH = N_FOURIER_GROUPS * FOURIER_GROUP_DIM
IN_WIDTH = FOURIER_WIDTH + Q_WIDTH + 2 * KV_WIDTH + 2 * D_MODEL
N_EXPERTS = 32
TOP_K = 4
D_EXPERT = D_MODEL
SWIGLU_LIMIT = 7.0
SWIGLU_ALPHA = 1.702
RMS_EPS = 1e-6
NEG_INF = -1e30

F32 = jnp.float32
BF16 = jnp.bfloat16
U32 = jnp.uint32

V7X_VMEM_BYTES = 64 * 1024 * 1024
LANES = 128
SUBLANES = 8
MIB = 1024 * 1024

TM_IN = 512
TN_IN = 512
NORM_ROWS = 64
TT_F = 512
F_DIRECT = SEQ // TT_F // 2 + 1
DFT_RADIX = 64
TQ = 512
QB = 128
TM_MG = 512
MG_SLAB = 256
RT = 128
R_MAX = 17 * RT
FC = 256
NF = D_EXPERT // FC
ROWS = TOP_K * N_TOK + N_EXPERTS * RT
N_ITEMS = (TOP_K * N_TOK + N_EXPERTS * (RT - 1) + N_EXPERTS * (R_MAX - RT)) // R_MAX + 1
TM_CB = 256
HALF = D_MODEL // 2
ACC_ROWS = 768
ACC_GROUPS = tuple((r, min(ACC_ROWS, R_MAX - r)) for r in range(0, R_MAX, ACC_ROWS))
FIN_ROWS = 256
FIN_TILES = tuple((r, min(FIN_ROWS, R_MAX - r)) for r in range(0, R_MAX, FIN_ROWS))
OUT_CHUNK = 256
GATHER_PER_STEP = R_MAX // NF
SLOT_ROWS = TOP_K * N_TOK + R_MAX


def _cparams(sem, vmem_mib):
    return pltpu.CompilerParams(dimension_semantics=sem, vmem_limit_bytes=vmem_mib * MIB)


def _dot(a, b):
    return jnp.dot(a, b, preferred_element_type=F32)


def _pack_halves(lo, hi):
    lo_bits = pltpu.bitcast(lo.astype(BF16).astype(F32), U32)
    hi_bits = pltpu.bitcast(hi.astype(BF16).astype(F32), U32)
    return (hi_bits & jnp.uint32(0xFFFF0000)) | (lo_bits >> jnp.uint32(16))


def _unpack_halves(words):
    lo = pltpu.bitcast(words << jnp.uint32(16), F32)
    hi = pltpu.bitcast(words & jnp.uint32(0xFFFF0000), F32)
    return lo, hi


def _store_row_tiles(tiles_ref, words):
    n_rows = words.shape[0]
    for j in range(SUBLANES):
        tiles_ref[pl.ds(j, n_rows, stride=SUBLANES), :] = words[:, j * LANES:(j + 1) * LANES]


def _load_row_tiles(tiles_ref, j, n_rows):
    return tiles_ref[pl.ds(j, n_rows, stride=SUBLANES), :]


def _in_proj_kernel(x_ref, g1_ref, w_ref, b_ref, qg_ref, kg_ref, ones_ref,
                    u_ref, q_ref, kv_ref, gate_ref, h_scr):
    def body(c, carry):
        rows = pl.ds(pl.multiple_of(c * NORM_ROWS, NORM_ROWS), NORM_ROWS)
        x = x_ref[rows, :]
        ms = jnp.mean(x * x, axis=-1, keepdims=True)
        h_scr[rows, :] = (x * lax.rsqrt(ms + RMS_EPS) * g1_ref[...]).astype(BF16)
        return carry
    lax.fori_loop(0, TM_IN // NORM_ROWS, body, 0)

    def head_norm(a, gain):
        sq = a * a
        hi = sq.astype(BF16)
        lo = (sq - hi.astype(F32)).astype(BF16)
        ssq = _dot(hi, ones_ref[...]) + _dot(lo, ones_ref[...])
        return a * lax.rsqrt(ssq * (1.0 / HEAD_DIM) + RMS_EPS) * gain

    h = h_scr[...]
    for c in range(IN_WIDTH // TN_IN):
        col0 = c * TN_IN
        acc = _dot(h, w_ref[:, col0:col0 + TN_IN])
        if col0 < FOURIER_WIDTH:
            u_ref[:, col0:col0 + TN_IN] = acc.astype(BF16)
        elif col0 < FOURIER_WIDTH + Q_WIDTH:
            o = col0 - FOURIER_WIDTH
            q_ref[:, o:o + TN_IN] = (head_norm(acc, qg_ref[...]) * (HEAD_DIM ** -0.5)).astype(BF16)
        elif col0 < FOURIER_WIDTH + Q_WIDTH + 2 * KV_WIDTH:
            lane = lax.broadcasted_iota(jnp.int32, acc.shape, 1)
            kv_ref[...] = jnp.where(lane < KV_WIDTH, head_norm(acc, kg_ref[...]), acc).astype(BF16)
        else:
            o = col0 - (FOURIER_WIDTH + Q_WIDTH + 2 * KV_WIDTH)
            z = acc + b_ref[:, o:o + TN_IN]
            gate_ref[:, o:o + TN_IN] = (1.0 / (1.0 + jnp.exp(-z))).astype(BF16)


def _in_proj(x2, g1, w_bf, bias, qg_t, kg_t, ones_bd):
    const = lambda i: (0, 0)
    row = lambda i: (i, 0)
    return pl.pallas_call(
        _in_proj_kernel,
        grid=(N_TOK // TM_IN,),
        in_specs=[
            pl.BlockSpec((TM_IN, D_MODEL), row),
            pl.BlockSpec((1, D_MODEL), const),
            pl.BlockSpec((D_MODEL, IN_WIDTH), const, pipeline_mode=pl.Buffered(1)),
            pl.BlockSpec((1, 2 * D_MODEL), const),
            pl.BlockSpec((1, TN_IN), const),
            pl.BlockSpec((1, TN_IN), const),
            pl.BlockSpec((TN_IN, TN_IN), const),
        ],
        out_specs=[
            pl.BlockSpec((TM_IN, FOURIER_WIDTH), row),
            pl.BlockSpec((TM_IN, Q_WIDTH), row),
            pl.BlockSpec((TM_IN, 2 * KV_WIDTH), row),
            pl.BlockSpec((TM_IN, 2 * D_MODEL), row),
        ],
        out_shape=[
            jax.ShapeDtypeStruct((N_TOK, FOURIER_WIDTH), BF16),
            jax.ShapeDtypeStruct((N_TOK, Q_WIDTH), BF16),
            jax.ShapeDtypeStruct((N_TOK, 2 * KV_WIDTH), BF16),
            jax.ShapeDtypeStruct((N_TOK, 2 * D_MODEL), BF16),
        ],
        scratch_shapes=[pltpu.VMEM((TM_IN, D_MODEL), BF16)],
        compiler_params=_cparams(("arbitrary",), 60),
        name="in_proj",
    )(x2, g1, w_bf, bias, qg_t, kg_t, ones_bd)


def _fourier_kernel(u_ref, f1r_ref, f1i_ref, gr_ref, gi_ref, cc_ref, sc_ref, rev_ref, y_ref, cs_scr, ss_scr, pq_scr):
    t = pl.program_id(0)
    b = pl.program_id(1)
    nt = SEQ // TT_F

    @pl.when(t < F_DIRECT)
    def _():
        @pl.when(b == 0)
        def _():
            gr = gr_ref[...]
            gi = gi_ref[...]
            for a in range(TT_F // DFT_RADIX):
                f1r = f1r_ref[a:a + 1, :]
                f1i = f1i_ref[a:a + 1, :]
                rows = slice(a * DFT_RADIX, (a + 1) * DFT_RADIX)
                cs_scr[rows, :] = (f1r * gr - f1i * gi).astype(BF16)
                ss_scr[rows, :] = (f1r * gi + f1i * gr).astype(BF16)

        u = u_ref[...]
        a_seq = _dot(cs_scr[...], u)
        b_seq = _dot(ss_scr[...], u)
        for g in range(N_FOURIER_GROUPS):
            cols = slice(g * FOURIER_GROUP_DIM, (g + 1) * FOURIER_GROUP_DIM)
            p = _dot(a_seq[:, cols].astype(BF16), cc_ref[...])
            q = _dot(b_seq[:, cols].astype(BF16), sc_ref[...])
            y_ref[:, cols] = (p - q).astype(BF16)

            @pl.when(t < nt - F_DIRECT + 1)
            def _():
                pq_scr[b, t, :, cols] = (p + q).astype(BF16)

    @pl.when(t >= F_DIRECT)
    def _():
        src = jnp.concatenate([pq_scr[b, nt - 1 - t], pq_scr[b, nt - t]], axis=0)
        y_ref[...] = _dot(rev_ref[...], src).astype(BF16)


def _fourier(u, f1r, f1i, gr, gi, cc, sc, rev):
    nt = SEQ // TT_F
    f1_rows = TT_F // DFT_RADIX
    const = lambda t, b: (0, 0)
    direct = lambda t: jnp.minimum(t, F_DIRECT - 1)
    return pl.pallas_call(
        _fourier_kernel,
        grid=(nt, BATCH),
        in_specs=[
            pl.BlockSpec((SEQ, FOURIER_WIDTH), lambda t, b: (jnp.where(t < F_DIRECT, b, BATCH - 1), 0)),
            pl.BlockSpec((f1_rows, SEQ), lambda t, b: (direct(t), 0)),
            pl.BlockSpec((f1_rows, SEQ), lambda t, b: (direct(t), 0)),
            pl.BlockSpec((DFT_RADIX, SEQ), const),
            pl.BlockSpec((DFT_RADIX, SEQ), const),
            pl.BlockSpec((FOURIER_GROUP_DIM, FOURIER_GROUP_DIM), const),
            pl.BlockSpec((FOURIER_GROUP_DIM, FOURIER_GROUP_DIM), const),
            pl.BlockSpec((TT_F, 2 * TT_F), const),
        ],
        out_specs=pl.BlockSpec((TT_F, FOURIER_WIDTH), lambda t, b: (b * nt + t, 0)),
        out_shape=jax.ShapeDtypeStruct((N_TOK, FOURIER_WIDTH), BF16),
        scratch_shapes=[pltpu.VMEM((TT_F, SEQ), BF16), pltpu.VMEM((TT_F, SEQ), BF16),
                        pltpu.VMEM((BATCH, nt - F_DIRECT + 1, TT_F, FOURIER_WIDTH), BF16)],
        compiler_params=_cparams(("arbitrary", "arbitrary"), 60),
        name="fourier",
    )(u, f1r, f1i, gr, gi, cc, sc, rev)


def _mirror_rows():
    rev = np.zeros((TT_F, 2 * TT_F), np.float32)
    rev[0, TT_F] = 1.0
    rev[np.arange(1, TT_F), TT_F - np.arange(1, TT_F)] = 1.0
    return jnp.asarray(rev).astype(BF16)


def _alibi_slope(h):
    return float(2.0 ** (-8.0 * (h + 1) / N_Q_HEADS))


def _attention_kernel(sink_ref, q_ref, prev_ref, cur_ref, next_ref, sel_ref, o_ref, kz_scr, vz_scr):
    t = pl.program_id(1)
    band = jnp.concatenate([prev_ref[...], cur_ref[...], next_ref[...]], axis=0)
    for g in range(N_KV_HEADS):
        c, half = divmod(g, 2)
        kcol = band[:, c * LANES:(c + 1) * LANES]
        vcol = band[:, KV_WIDTH + c * LANES:KV_WIDTH + (c + 1) * LANES]
        for d in range(2):
            sel = sel_ref[half * 2 + d]
            kz_scr[2 * g + d] = jnp.transpose(_dot(kcol, sel)).astype(BF16)
            vz_scr[2 * g + d] = _dot(vcol, sel).astype(BF16)

    row = lax.broadcasted_iota(jnp.int32, (2 * QB, 3 * QB), 0)
    col = lax.broadcasted_iota(jnp.int32, (2 * QB, 3 * QB), 1)
    absrel = jnp.abs((row & (QB - 1)) - col + QB)
    absrel_f = absrel.astype(F32)
    top = lax.broadcasted_iota(jnp.int32, (2 * QB, 1), 0) < QB
    for i in range(TQ // QB):
        q_rows = slice(i * QB, (i + 1) * QB)
        k_rows = slice(i * QB, (i + 3) * QB)
        kpos = t * TQ + (i - 1) * QB + col
        mask = (absrel <= WINDOW) & (kpos >= 0) & (kpos < SEQ)
        for g in range(N_KV_HEADS):
            cols = [slice((2 * g + p) * LANES, (2 * g + p + 1) * LANES) for p in range(2)]
            qq = jnp.concatenate([q_ref[q_rows, cols[0]], q_ref[q_rows, cols[1]]], axis=0)
            out = jnp.zeros((2 * QB, LANES), F32)
            for d in range(2):
                h_top, h_bot = 4 * g + d, 4 * g + 2 + d
                slope = jnp.where(top, _alibi_slope(h_top), _alibi_slope(h_bot))
                sink = jnp.where(top, sink_ref[h_top], sink_ref[h_bot])
                s = _dot(qq, kz_scr[2 * g + d, :, k_rows])
                s = jnp.where(mask, s - slope * absrel_f, NEG_INF)
                m = jnp.maximum(jnp.max(s, axis=-1, keepdims=True), sink)
                pr = jnp.exp(s - m)
                den = jnp.sum(pr, axis=-1, keepdims=True) + jnp.exp(sink - m)
                out = out + _dot(pr.astype(BF16), vz_scr[2 * g + d, k_rows, :]) / den
            o_ref[q_rows, cols[0]] = out[:QB].astype(BF16)
            o_ref[q_rows, cols[1]] = out[QB:].astype(BF16)


def _attention(sink, q, kv, sel):
    nt = SEQ // TQ
    nb = SEQ // QB
    r = TQ // QB
    grid_spec = pltpu.PrefetchScalarGridSpec(
        num_scalar_prefetch=1,
        grid=(BATCH, nt),
        in_specs=[
            pl.BlockSpec((TQ, Q_WIDTH), lambda b, t, s: (b * nt + t, 0)),
            pl.BlockSpec((QB, 2 * KV_WIDTH), lambda b, t, s: (b * nb + jnp.maximum(t * r - 1, 0), 0)),
            pl.BlockSpec((TQ, 2 * KV_WIDTH), lambda b, t, s: (b * nt + t, 0)),
            pl.BlockSpec((QB, 2 * KV_WIDTH), lambda b, t, s: (b * nb + jnp.minimum(t * r + r, nb - 1), 0)),
            pl.BlockSpec((4, LANES, LANES), lambda b, t, s: (0, 0, 0)),
        ],
        out_specs=pl.BlockSpec((TQ, Q_WIDTH), lambda b, t, s: (b * nt + t, 0)),
        scratch_shapes=[pltpu.VMEM((2 * N_KV_HEADS, LANES, TQ + 2 * QB), BF16),
                        pltpu.VMEM((2 * N_KV_HEADS, TQ + 2 * QB, LANES), BF16)],
    )
    return pl.pallas_call(
        _attention_kernel,
        grid_spec=grid_spec,
        out_shape=jax.ShapeDtypeStruct((N_TOK, Q_WIDTH), BF16),
        compiler_params=_cparams(("arbitrary", "arbitrary"), 32),
        name="attention",
    )(sink, q, kv, kv, kv, sel)


def _merge_kernel(yf_ref, at_ref, gf_ref, ga_ref, x_ref, wfo_ref, wao_ref, wo_ref, g2_ref, wr_ref, br_ref,
                  ltri_ref, x1_ref, h2p_ref, meta_ref, cnt_ref, base_scr):
    i = pl.program_id(0)

    @pl.when(i == 0)
    def _():
        base_scr[...] = jnp.zeros_like(base_scr)

    base = base_scr[...]
    for s in range(TM_MG // MG_SLAB):
        rows = slice(s * MG_SLAB, (s + 1) * MG_SLAB)
        yf = _dot(yf_ref[rows, :], wfo_ref[...])
        ya = _dot(at_ref[rows, :], wao_ref[...])
        merged = gf_ref[rows, :].astype(F32) * yf + ga_ref[rows, :].astype(F32) * ya
        x1 = x_ref[rows, :] + _dot(merged.astype(BF16), wo_ref[...])
        x1_ref[rows, :] = x1
        ms = jnp.mean(x1 * x1, axis=-1, keepdims=True)
        h2 = x1 * lax.rsqrt(ms + RMS_EPS) * g2_ref[...]
        _store_row_tiles(h2p_ref.at[pl.ds(s * MG_SLAB * SUBLANES, MG_SLAB * SUBLANES), :],
                         _pack_halves(h2[:, :HALF], h2[:, HALF:]))
        logits = _dot(h2.astype(BF16), wr_ref[...]) + br_ref[...]

        lane = lax.broadcasted_iota(jnp.int32, logits.shape, 1)
        lane_f = lane.astype(F32)
        vals = logits
        top_v, top_i, onehots = [], [], []
        for _ in range(TOP_K):
            m = jnp.max(vals, axis=-1, keepdims=True)
            idx = jnp.min(jnp.where(vals == m, lane_f, float(LANES)), axis=-1, keepdims=True)
            oh = lane_f == idx
            top_v.append(m)
            top_i.append(idx)
            onehots.append(oh)
            vals = jnp.where(oh, -jnp.inf, vals)
        exps = [jnp.exp(v - top_v[0]) for v in top_v]
        den = exps[0] + exps[1] + exps[2] + exps[3]
        gates = [e / den for e in exps]

        cnt = sum(oh.astype(F32) for oh in onehots)
        prefix = _dot(ltri_ref[...], cnt.astype(BF16))
        tot = base + prefix
        ranks = [jnp.sum(jnp.where(oh, tot, 0.0), axis=-1, keepdims=True) for oh in onehots]
        base = base + jnp.sum(cnt, axis=0, keepdims=True)

        meta = jnp.zeros(logits.shape, F32)
        for k in range(TOP_K):
            meta = jnp.where(lane == k, top_i[k], meta)
            meta = jnp.where(lane == TOP_K + k, ranks[k], meta)
            meta = jnp.where(lane == 2 * TOP_K + k, gates[k], meta)
        meta_ref[rows, :] = meta
    base_scr[...] = base
    cnt_ref[...] = jnp.broadcast_to(base, cnt_ref.shape)


def _merge(yf, attn, gates, x2, wfo, wao, wo, g2, wr, br, ltri):
    tm = TM_MG
    const = lambda i: (0, 0)
    resident = functools.partial(pl.BlockSpec, index_map=const, pipeline_mode=pl.Buffered(1))
    return pl.pallas_call(
        _merge_kernel,
        grid=(N_TOK // tm,),
        in_specs=[
            pl.BlockSpec((tm, FOURIER_WIDTH), lambda i: (i, 0)),
            pl.BlockSpec((tm, Q_WIDTH), lambda i: (i, 0)),
            pl.BlockSpec((tm, D_MODEL), lambda i: (i, 0)),
            pl.BlockSpec((tm, D_MODEL), lambda i: (i, 1)),
            pl.BlockSpec((tm, D_MODEL), lambda i: (i, 0)),
            resident((FOURIER_WIDTH, D_MODEL)),
            resident((Q_WIDTH, D_MODEL)),
            resident((D_MODEL, D_MODEL)),
            pl.BlockSpec((1, D_MODEL), const),
            pl.BlockSpec((D_MODEL, LANES), const),
            pl.BlockSpec((1, LANES), const),
            pl.BlockSpec((MG_SLAB, MG_SLAB), const),
        ],
        out_specs=[
            pl.BlockSpec((tm, D_MODEL), lambda i: (i, 0)),
            pl.BlockSpec((tm * SUBLANES, LANES), lambda i: (i, 0)),
            pl.BlockSpec((tm, LANES), lambda i: (i, 0)),
            pl.BlockSpec((8, LANES), const),
        ],
        out_shape=[
            jax.ShapeDtypeStruct((N_TOK, D_MODEL), F32),
            jax.ShapeDtypeStruct((N_TOK * SUBLANES, LANES), U32),
            jax.ShapeDtypeStruct((N_TOK, LANES), F32),
            jax.ShapeDtypeStruct((8, LANES), F32),
        ],
        scratch_shapes=[pltpu.VMEM((1, LANES), F32)],
        compiler_params=_cparams(("arbitrary",), 56),
        name="merge",
    )(yf, attn, gates, gates, x2, wfo, wao, wo, g2, wr, br, ltri)


def _expert_kernel(e_ref, live_ref, tok_cur_ref, tok_nxt_ref, dst_cur_ref,
                   h2p_hbm, wg_ref, bg_ref, wu_ref, bu_ref, wd_ref, bd_ref,
                   yslots_hbm, xu_scr, acc_scr, xs_scr, wgb, wub, wdb, gsem, ssem):
    del e_ref
    w = pl.program_id(0)
    f = pl.program_id(1)
    live = live_ref[w] > 0
    prev_live = (w > 0) & (live_ref[jnp.maximum(w - 1, 0)] > 0)
    cur = w & 1
    nxt = 1 - cur
    def row_tile(row):
        return pl.ds(pl.multiple_of(row * SUBLANES, SUBLANES), SUBLANES)

    def gather_copy(tok_ref, slot, row):
        tok = tok_ref[0, 0, row]
        return pltpu.make_async_copy(h2p_hbm.at[row_tile(tok), :], xu_scr.at[slot, row_tile(row), :], gsem.at[slot])

    def scatter_copy(row):
        dst = dst_cur_ref[0, 0, row]
        return pltpu.make_async_copy(xu_scr.at[cur, row_tile(row), :], yslots_hbm.at[row_tile(dst), :], ssem)

    def for_all_rows(fn):
        def body(row, carry):
            fn(row)
            return carry
        lax.fori_loop(0, R_MAX, body, 0, unroll=8)

    def prefetch_next(part, n_parts):
        lo = f * GATHER_PER_STEP + part * GATHER_PER_STEP // n_parts
        for g in range((part + 1) * GATHER_PER_STEP // n_parts - part * GATHER_PER_STEP // n_parts):
            gather_copy(tok_nxt_ref, nxt, lo + g).start(priority=g % 2)

    def cast_weights():
        wgb[...] = wg_ref[...].astype(BF16)
        wub[...] = wu_ref[...].astype(BF16)
        wdb[...] = wd_ref[...].astype(BF16)

    def item_tiles(row0, n_rows):
        return xu_scr.at[cur, pl.ds(row0 * SUBLANES, n_rows * SUBLANES), :]

    def stage_inputs(row0, n_rows, stage_row0):
        tiles = item_tiles(row0, n_rows)
        stage = xs_scr.at[pl.ds(stage_row0, n_rows), :]
        for j in range(SUBLANES):
            lo, hi = _unpack_halves(_load_row_tiles(tiles, j, n_rows))
            stage[:, j * LANES:(j + 1) * LANES] = lo.astype(BF16)
            stage[:, HALF + j * LANES:HALF + (j + 1) * LANES] = hi.astype(BF16)

    def activations(n_rows, stage_row0):
        x = xs_scr[pl.ds(stage_row0, n_rows), :]
        g = _dot(x, wgb[...]) + bg_ref[...]
        u = _dot(x, wub[...]) + bu_ref[...]
        g = jnp.minimum(g, SWIGLU_LIMIT)
        u = jnp.clip(u, -SWIGLU_LIMIT, SWIGLU_LIMIT)
        act = (u + 1.0) * (g * (1.0 / (1.0 + jnp.exp(-SWIGLU_ALPHA * g))))
        return act.astype(BF16)

    def accumulate(row0, n_rows):
        rows = pl.ds(row0, n_rows)
        stage_inputs(row0, n_rows, 0)
        c = _dot(activations(n_rows, 0), wdb[...])
        acc_scr[rows, :] = jnp.where(f == 0, c, acc_scr[rows, :] + c)

    def finish(row0, n_rows, stage_row0):
        act = activations(n_rows, stage_row0)
        rows = pl.ds(row0, n_rows)
        tiles = item_tiles(row0, n_rows)
        for c in range(HALF // OUT_CHUNK):
            lo_cols = slice(c * OUT_CHUNK, (c + 1) * OUT_CHUNK)
            hi_cols = slice(HALF + c * OUT_CHUNK, HALF + (c + 1) * OUT_CHUNK)
            y_lo = acc_scr[rows, lo_cols] + _dot(act, wdb[:, lo_cols]) + bd_ref[:, lo_cols]
            y_hi = acc_scr[rows, hi_cols] + _dot(act, wdb[:, hi_cols]) + bd_ref[:, hi_cols]
            words = _pack_halves(y_lo, y_hi)
            for jj in range(OUT_CHUNK // LANES):
                j_tile = c * (OUT_CHUNK // LANES) + jj
                tiles[pl.ds(j_tile, n_rows, stride=SUBLANES), :] = words[:, jj * LANES:(jj + 1) * LANES]

    first_tile = pl.ds(0, SUBLANES)

    @pl.when(f == 0)
    def _():
        @pl.when(w == 0)
        def _():
            def zero(s, carry):
                acc_scr[pl.ds(pl.multiple_of(s * RT, RT), RT), :] = jnp.zeros((RT, D_MODEL), F32)
                xu_scr[1, pl.ds(pl.multiple_of(s * RT * SUBLANES, RT * SUBLANES), RT * SUBLANES), :] = (
                    jnp.zeros((RT * SUBLANES, LANES), U32))
                return carry
            lax.fori_loop(0, R_MAX // RT, zero, 0)
            pad_rows = pl.ds(TOP_K * N_TOK * SUBLANES, R_MAX * SUBLANES)
            fill = pltpu.make_async_copy(xu_scr.at[1], yslots_hbm.at[pad_rows, :], ssem)
            fill.start()
            fill.wait()
            for_all_rows(lambda row: gather_copy(tok_cur_ref, cur, row).start())

        @pl.when((w == 0) | prev_live)
        def _():
            row_in = pltpu.make_async_copy(h2p_hbm.at[first_tile, :], xu_scr.at[cur, first_tile, :], gsem.at[cur])
            for_all_rows(lambda row: row_in.wait())

        @pl.when(prev_live)
        def _():
            row_out = pltpu.make_async_copy(xu_scr.at[cur, first_tile, :], yslots_hbm.at[first_tile, :], ssem)
            for_all_rows(lambda row: row_out.wait())

    @pl.when(live & (f < NF - 1))
    def _():
        cast_weights()
        for part, (row0, n_rows) in enumerate(ACC_GROUPS):
            accumulate(row0, n_rows)
            prefetch_next(part, len(ACC_GROUPS))

    @pl.when(live & (f == NF - 1))
    def _():
        cast_weights()
        stage_row0 = [(part % 3) * FIN_ROWS for part in range(len(FIN_TILES))]
        stage_inputs(*FIN_TILES[0], stage_row0[0])
        for part, (row0, n_rows) in enumerate(FIN_TILES):
            finish(row0, n_rows, stage_row0[part])
            if part + 1 < len(FIN_TILES):
                stage_inputs(*FIN_TILES[part + 1], stage_row0[part + 1])
            for r in range(n_rows):
                scatter_copy(row0 + r).start(priority=r % 2)
            prefetch_next(part, len(FIN_TILES))


def _experts(item_e, item_live, item_tok, item_dst, h2p, wg, bg, wu, bu, wd, bd):
    def w_in_map(w, f, e, lv):
        return (e[w], 0, jnp.where(lv[w] > 0, f, NF - 1))

    def w_down_map(w, f, e, lv):
        return (e[w], jnp.where(lv[w] > 0, f, NF - 1), 0)

    def b_down_map(w, f, e, lv):
        return (e[w], 0, 0)

    grid_spec = pltpu.PrefetchScalarGridSpec(
        num_scalar_prefetch=2,
        grid=(jnp.sum(item_live) + 1, NF),
        in_specs=[
            pl.BlockSpec((1, 1, R_MAX), lambda w, f, e, lv: (w, 0, 0), memory_space=pltpu.SMEM),
            pl.BlockSpec((1, 1, R_MAX), lambda w, f, e, lv: (jnp.minimum(w + 1, N_ITEMS - 1), 0, 0),
                         memory_space=pltpu.SMEM),
            pl.BlockSpec((1, 1, R_MAX), lambda w, f, e, lv: (w, 0, 0), memory_space=pltpu.SMEM),
            pl.BlockSpec(memory_space=pl.ANY),
            pl.BlockSpec((None, D_MODEL, FC), w_in_map),
            pl.BlockSpec((None, 1, FC), w_in_map),
            pl.BlockSpec((None, D_MODEL, FC), w_in_map),
            pl.BlockSpec((None, 1, FC), w_in_map),
            pl.BlockSpec((None, FC, D_MODEL), w_down_map),
            pl.BlockSpec((None, 1, D_MODEL), b_down_map),
        ],
        out_specs=pl.BlockSpec(memory_space=pl.ANY),
        scratch_shapes=[
            pltpu.VMEM((2, R_MAX * SUBLANES, LANES), U32),
            pltpu.VMEM((R_MAX, D_MODEL), F32),
            pltpu.VMEM((ACC_ROWS, D_MODEL), BF16),
            pltpu.VMEM((D_MODEL, FC), BF16),
            pltpu.VMEM((D_MODEL, FC), BF16),
            pltpu.VMEM((FC, D_MODEL), BF16),
            pltpu.SemaphoreType.DMA((2,)),
            pltpu.SemaphoreType.DMA(()),
        ],
    )
    return pl.pallas_call(
        _expert_kernel,
        grid_spec=grid_spec,
        out_shape=jax.ShapeDtypeStruct((SLOT_ROWS * SUBLANES, LANES), U32),
        compiler_params=_cparams(("arbitrary", "arbitrary"), 60),
        name="experts",
    )(item_e, item_live, item_tok, item_tok, item_dst, h2p, wg, bg, wu, bu, wd, bd)


def _combine_kernel(meta_ref, x1_ref, y0_ref, y1_ref, y2_ref, y3_ref, out_ref):
    gates = [meta_ref[:, 2 * TOP_K + k:2 * TOP_K + k + 1] for k in range(TOP_K)]
    for j in range(SUBLANES):
        lo_cols = slice(j * LANES, (j + 1) * LANES)
        hi_cols = slice(HALF + j * LANES, HALF + (j + 1) * LANES)
        lo_acc = x1_ref[:, lo_cols]
        hi_acc = x1_ref[:, hi_cols]
        for k, y_ref in enumerate((y0_ref, y1_ref, y2_ref, y3_ref)):
            lo, hi = _unpack_halves(_load_row_tiles(y_ref, j, TM_CB))
            lo_acc = lo_acc + gates[k] * lo
            hi_acc = hi_acc + gates[k] * hi
        out_ref[:, lo_cols] = lo_acc
        out_ref[:, hi_cols] = hi_acc


def _combine(meta, x1, y_slots):
    tm = TM_CB
    nblk = N_TOK // tm
    slot_specs = [pl.BlockSpec((tm * SUBLANES, LANES), lambda i, k=k: (k * nblk + i, 0)) for k in range(TOP_K)]
    return pl.pallas_call(
        _combine_kernel,
        grid=(nblk,),
        in_specs=[
            pl.BlockSpec((tm, LANES), lambda i: (i, 0)),
            pl.BlockSpec((tm, D_MODEL), lambda i: (i, 0)),
        ] + slot_specs,
        out_specs=pl.BlockSpec((tm, D_MODEL), lambda i: (i, 0)),
        out_shape=jax.ShapeDtypeStruct((N_TOK, D_MODEL), F32),
        compiler_params=_cparams(("arbitrary",), 32),
        name="combine",
    )(meta, x1, y_slots, y_slots, y_slots, y_slots)


def _dft_tables():
    r = DFT_RADIX
    k = np.arange(r)
    e64 = np.exp(2j * np.pi * np.outer(k, k) / r)
    e4096 = np.exp(2j * np.pi * np.outer(k, k) / SEQ)
    s = np.arange(SEQ)
    f1 = e64[:, s % r]
    g = e64[:, s // r] * e4096[:, s % r] / math.sqrt(SEQ)
    c = np.arange(FOURIER_GROUP_DIM)
    ang = 2.0 * np.pi * np.outer(c, c) / FOURIER_GROUP_DIM
    scale = 1.0 / math.sqrt(FOURIER_GROUP_DIM)
    as32 = lambda a: jnp.asarray(a.astype(np.float32))
    return (as32(f1.real), as32(f1.imag), as32(g.real), as32(g.imag),
            jnp.asarray((np.cos(ang) * scale).astype(np.float32)).astype(BF16),
            jnp.asarray((np.sin(ang) * scale).astype(np.float32)).astype(BF16))


def _head_selectors():
    i = np.arange(LANES)[:, None]
    j = np.arange(LANES)[None, :]
    out = np.zeros((4, LANES, LANES), np.float32)
    for src in range(2):
        for dst in range(2):
            out[2 * src + dst] = (i - HEAD_DIM * src == j - HEAD_DIM * dst) & (j // HEAD_DIM == dst)
    return jnp.asarray(out).astype(BF16)


def kernel(x, norm1_g, w_in, b_branch_gate, q_norm_g, k_norm_g, attn_sink, w_fourier_out, w_attn_out, w_o,
           norm2_g, w_router, b_router, w_gate_e, b_gate_e, w_up_e, b_up_e, w_down_e, b_down_e):
    b, s, d = x.shape
    assert (b, s, d) == (BATCH, SEQ, D_MODEL) and norm1_g.shape[0] == 1
    x2 = x.reshape(N_TOK, D_MODEL)

    heads_per_tile = TN_IN // HEAD_DIM
    ones_bd = jnp.asarray((np.arange(TN_IN)[:, None] // HEAD_DIM == np.arange(TN_IN)[None, :] // HEAD_DIM)
                          .astype(np.float32)).astype(BF16)
    u_f, q, kv, gates = _in_proj(
        x2, norm1_g[0].reshape(1, D_MODEL), w_in[0].astype(BF16), b_branch_gate[0].reshape(1, 2 * D_MODEL),
        jnp.tile(q_norm_g[0], heads_per_tile).reshape(1, TN_IN),
        jnp.tile(k_norm_g[0], heads_per_tile).reshape(1, TN_IN), ones_bd)
    y_f = _fourier(u_f, *_dft_tables(), _mirror_rows())
    attn = _attention(attn_sink[0], q, kv, _head_selectors())

    wr = jnp.zeros((D_MODEL, LANES), BF16).at[:, :N_EXPERTS].set(w_router[0].astype(BF16))
    br = jnp.full((1, LANES), NEG_INF, F32).at[0, :N_EXPERTS].set(b_router[0])
    ltri = jnp.asarray(np.tril(np.ones((MG_SLAB, MG_SLAB), np.float32), -1)).astype(BF16)
    x1, h2p, meta, cnt = _merge(y_f, attn, gates, x2, w_fourier_out[0].astype(BF16), w_attn_out[0].astype(BF16),
                                w_o[0].astype(BF16), norm2_g[0].reshape(1, D_MODEL), wr, br, ltri)

    tables = _routing_tables(cnt[0, :N_EXPERTS].astype(jnp.int32), meta[:, 0:TOP_K].astype(jnp.int32),
                             meta[:, TOP_K:2 * TOP_K].astype(jnp.int32))
    y_slots = _experts(*tables, h2p,
                       w_gate_e[0], b_gate_e[0].reshape(N_EXPERTS, 1, D_EXPERT),
                       w_up_e[0], b_up_e[0].reshape(N_EXPERTS, 1, D_EXPERT),
                       w_down_e[0], b_down_e[0].reshape(N_EXPERTS, 1, D_MODEL))
    out = _combine(meta, x1, y_slots)
    return out.reshape(BATCH, SEQ, D_MODEL)


def _routing_tables(counts, top_idx, rank):
    padded = ((counts + RT - 1) // RT) * RT
    pend = jnp.cumsum(padded)
    pstart = pend - padded
    dest = pstart[top_idx] + rank
    n_assign = TOP_K * N_TOK
    row_assign = jnp.full((ROWS,), -1, jnp.int32).at[dest.reshape(n_assign)].set(
        jnp.arange(n_assign, dtype=jnp.int32), unique_indices=True)

    items_per_e = (padded + R_MAX - 1) // R_MAX
    it_end = jnp.cumsum(items_per_e)
    it_start = it_end - items_per_e
    total_items = it_end[-1]
    wi = jnp.arange(N_ITEMS, dtype=jnp.int32)
    e_w = jnp.minimum(jnp.searchsorted(it_end, wi, side="right"), N_EXPERTS - 1).astype(jnp.int32)
    j_w = wi - it_start[e_w]
    live = wi < total_items
    rows_w = jnp.clip(padded[e_w] - j_w * R_MAX, 0, R_MAX)
    e_last = e_w[jnp.maximum(total_items - 1, 0)]
    item_e = jnp.where(live, e_w, e_last).astype(jnp.int32)
    item_row0 = jnp.where(live, pstart[e_w] + j_w * R_MAX, 0).astype(jnp.int32)
    item_nsub = jnp.where(live, rows_w // RT, 0).astype(jnp.int32)
    item_live = live.astype(jnp.int32)
    r = jnp.arange(R_MAX, dtype=jnp.int32)
    src = jnp.minimum(item_row0[:, None] + r[None, :], ROWS - 1)
    item_rows = jnp.where(r[None, :] < (item_nsub * RT)[:, None], row_assign[src], -1)
    item_tok = (jnp.maximum(item_rows, 0) // TOP_K).reshape(N_ITEMS, 1, R_MAX)
    item_dst = jnp.where(item_rows >= 0, (item_rows % TOP_K) * N_TOK + item_rows // TOP_K,
                         TOP_K * N_TOK + r[None, :]).reshape(N_ITEMS, 1, R_MAX)
    return item_e, item_live, item_tok, item_dst
```

```python
import functools
import math

import jax
import jax.numpy as jnp
import numpy as np
from jax import lax
from jax.experimental import pallas as pl
from jax.experimental.pallas import tpu as pltpu

D_MODEL = 2048
BATCH = 4
SEQ = 4096
N_TOK = BATCH * SEQ
N_Q_HEADS = 16
N_KV_HEADS = 4
HEAD_DIM = 64
Q_WIDTH = N_Q_HEADS * HEAD_DIM
KV_WIDTH = N_KV_HEADS * HEAD_DIM
WINDOW = 128
N_FOURIER_GROUPS = 4
FOURIER_GROUP_DIM = 256
FOURIER_WIDTH = N_FOURIER_GROUPS * FOURIER_GROUP_DIM
IN_WIDTH = FOURIER_WIDTH + Q_WIDTH + 2 * KV_WIDTH + 2 * D_MODEL
N_EXPERTS = 32
TOP_K = 4
D_EXPERT = D_MODEL
SWIGLU_LIMIT = 7.0
SWIGLU_ALPHA = 1.702
RMS_EPS = 1e-6
NEG_INF = -1e30

F32 = jnp.float32
BF16 = jnp.bfloat16
U32 = jnp.uint32

V7X_VMEM_BYTES = 64 * 1024 * 1024
LANES = 128
SUBLANES = 8
MIB = 1024 * 1024

TM_IN = 512
TN_IN = 512
NORM_ROWS = 64
TT_F = 512
F_DIRECT = SEQ // TT_F // 2 + 1
DFT_RADIX = 64
TQ = 512
QB = 128
TM_MG = 512
MG_SLAB = 256
RT = 256
R_MAX = 9 * RT
FC = 256
NF = D_EXPERT // FC
ROWS = TOP_K * N_TOK + N_EXPERTS * RT
N_ITEMS = (TOP_K * N_TOK + N_EXPERTS * (RT - 1) + N_EXPERTS * (R_MAX - RT)) // R_MAX + 1
TM_CB = 256
HALF = D_MODEL // 2
N_SUB = R_MAX // RT
GROUP = 3
OUT_CHUNK = 256
GATHER_PER_STEP = RT // NF
SLOT_ROWS = TOP_K * N_TOK + R_MAX


def _cparams(sem, vmem_mib):
    return pltpu.CompilerParams(dimension_semantics=sem, vmem_limit_bytes=vmem_mib * MIB)


def _dot(a, b):
    return jnp.dot(a, b, preferred_element_type=F32)


def _pack_halves(lo, hi):
    lo_bits = pltpu.bitcast(lo.astype(BF16).astype(F32), U32)
    hi_bits = pltpu.bitcast(hi.astype(BF16).astype(F32), U32)
    return (hi_bits & jnp.uint32(0xFFFF0000)) | (lo_bits >> jnp.uint32(16))


def _unpack_halves(words):
    lo = pltpu.bitcast(words << jnp.uint32(16), F32)
    hi = pltpu.bitcast(words & jnp.uint32(0xFFFF0000), F32)
    return lo, hi


def _store_row_tiles(tiles_ref, words):
    n_rows = words.shape[0]
    for j in range(SUBLANES):
        tiles_ref[pl.ds(j, n_rows, stride=SUBLANES), :] = words[:, j * LANES:(j + 1) * LANES]


def _load_row_tiles(tiles_ref, j, n_rows):
    return tiles_ref[pl.ds(j, n_rows, stride=SUBLANES), :]


def _in_proj_kernel(x_ref, g1_ref, w_ref, b_ref, qg_ref, kg_ref, ones_ref,
                    u_ref, q_ref, kv_ref, gate_ref, h_scr):
    def body(c, carry):
        rows = pl.ds(pl.multiple_of(c * NORM_ROWS, NORM_ROWS), NORM_ROWS)
        x = x_ref[rows, :]
        ms = jnp.mean(x * x, axis=-1, keepdims=True)
        h_scr[rows, :] = (x * lax.rsqrt(ms + RMS_EPS) * g1_ref[...]).astype(BF16)
        return carry
    lax.fori_loop(0, TM_IN // NORM_ROWS, body, 0)

    def head_norm(a, gain):
        ssq = _dot((a * a).astype(BF16), ones_ref[...])
        return a * lax.rsqrt(ssq * (1.0 / HEAD_DIM) + RMS_EPS) * gain

    h = h_scr[...]
    for c in range(IN_WIDTH // TN_IN):
        col0 = c * TN_IN
        acc = _dot(h, w_ref[:, col0:col0 + TN_IN])
        if col0 < FOURIER_WIDTH:
            u_ref[:, col0:col0 + TN_IN] = acc.astype(BF16)
        elif col0 < FOURIER_WIDTH + Q_WIDTH:
            o = col0 - FOURIER_WIDTH
            q_ref[:, o:o + TN_IN] = (head_norm(acc, qg_ref[...]) * (HEAD_DIM ** -0.5)).astype(BF16)
        elif col0 < FOURIER_WIDTH + Q_WIDTH + 2 * KV_WIDTH:
            lane = lax.broadcasted_iota(jnp.int32, acc.shape, 1)
            kv_ref[...] = jnp.where(lane < KV_WIDTH, head_norm(acc, kg_ref[...]), acc).astype(BF16)
        else:
            o = col0 - (FOURIER_WIDTH + Q_WIDTH + 2 * KV_WIDTH)
            z = acc + b_ref[:, o:o + TN_IN]
            gate_ref[:, o:o + TN_IN] = (1.0 / (1.0 + jnp.exp(-z))).astype(BF16)


def _in_proj(x2, g1, w_bf, bias, qg_t, kg_t, ones_bd):
    const = lambda i: (0, 0)
    row = lambda i: (i, 0)
    return pl.pallas_call(
        _in_proj_kernel,
        grid=(N_TOK // TM_IN,),
        in_specs=[
            pl.BlockSpec((TM_IN, D_MODEL), row),
            pl.BlockSpec((1, D_MODEL), const),
            pl.BlockSpec((D_MODEL, IN_WIDTH), const, pipeline_mode=pl.Buffered(1)),
            pl.BlockSpec((1, 2 * D_MODEL), const),
            pl.BlockSpec((1, TN_IN), const),
            pl.BlockSpec((1, TN_IN), const),
            pl.BlockSpec((TN_IN, TN_IN), const),
        ],
        out_specs=[
            pl.BlockSpec((TM_IN, FOURIER_WIDTH), row),
            pl.BlockSpec((TM_IN, Q_WIDTH), row),
            pl.BlockSpec((TM_IN, 2 * KV_WIDTH), row),
            pl.BlockSpec((TM_IN, 2 * D_MODEL), row),
        ],
        out_shape=[
            jax.ShapeDtypeStruct((N_TOK, FOURIER_WIDTH), BF16),
            jax.ShapeDtypeStruct((N_TOK, Q_WIDTH), BF16),
            jax.ShapeDtypeStruct((N_TOK, 2 * KV_WIDTH), BF16),
            jax.ShapeDtypeStruct((N_TOK, 2 * D_MODEL), BF16),
        ],
        scratch_shapes=[pltpu.VMEM((TM_IN, D_MODEL), BF16)],
        compiler_params=_cparams(("arbitrary",), 60),
        name="in_proj",
    )(x2, g1, w_bf, bias, qg_t, kg_t, ones_bd)


def _fourier_kernel(u_ref, f1r_ref, f1i_ref, gr_ref, gi_ref, cc_ref, sc_ref, rev_ref, y_ref, cs_scr, ss_scr, pq_scr):
    t = pl.program_id(0)
    b = pl.program_id(1)
    nt = SEQ // TT_F

    @pl.when(t < F_DIRECT)
    def _():
        @pl.when(b == 0)
        def _():
            gr = gr_ref[...]
            gi = gi_ref[...]
            for a in range(TT_F // DFT_RADIX):
                f1r = f1r_ref[a:a + 1, :]
                f1i = f1i_ref[a:a + 1, :]
                rows = slice(a * DFT_RADIX, (a + 1) * DFT_RADIX)
                cs_scr[rows, :] = (f1r * gr - f1i * gi).astype(BF16)
                ss_scr[rows, :] = (f1r * gi + f1i * gr).astype(BF16)

        u = u_ref[...]
        a_seq = _dot(cs_scr[...], u)
        b_seq = _dot(ss_scr[...], u)
        for g in range(N_FOURIER_GROUPS):
            cols = slice(g * FOURIER_GROUP_DIM, (g + 1) * FOURIER_GROUP_DIM)
            p = _dot(a_seq[:, cols].astype(BF16), cc_ref[...])
            q = _dot(b_seq[:, cols].astype(BF16), sc_ref[...])
            y_ref[:, cols] = (p - q).astype(BF16)

            @pl.when(t < nt - F_DIRECT + 1)
            def _():
                pq_scr[b, t, :, cols] = (p + q).astype(BF16)

    @pl.when(t >= F_DIRECT)
    def _():
        src = jnp.concatenate([pq_scr[b, nt - 1 - t], pq_scr[b, nt - t]], axis=0)
        y_ref[...] = _dot(rev_ref[...], src).astype(BF16)


def _fourier(u, f1r, f1i, gr, gi, cc, sc, rev):
    nt = SEQ // TT_F
    f1_rows = TT_F // DFT_RADIX
    const = lambda t, b: (0, 0)
    direct = lambda t: jnp.minimum(t, F_DIRECT - 1)
    return pl.pallas_call(
        _fourier_kernel,
        grid=(nt, BATCH),
        in_specs=[
            pl.BlockSpec((SEQ, FOURIER_WIDTH), lambda t, b: (jnp.where(t < F_DIRECT, b, BATCH - 1), 0)),
            pl.BlockSpec((f1_rows, SEQ), lambda t, b: (direct(t), 0)),
            pl.BlockSpec((f1_rows, SEQ), lambda t, b: (direct(t), 0)),
            pl.BlockSpec((DFT_RADIX, SEQ), const),
            pl.BlockSpec((DFT_RADIX, SEQ), const),
            pl.BlockSpec((FOURIER_GROUP_DIM, FOURIER_GROUP_DIM), const),
            pl.BlockSpec((FOURIER_GROUP_DIM, FOURIER_GROUP_DIM), const),
            pl.BlockSpec((TT_F, 2 * TT_F), const),
        ],
        out_specs=pl.BlockSpec((TT_F, FOURIER_WIDTH), lambda t, b: (b * nt + t, 0)),
        out_shape=jax.ShapeDtypeStruct((N_TOK, FOURIER_WIDTH), BF16),
        scratch_shapes=[pltpu.VMEM((TT_F, SEQ), BF16), pltpu.VMEM((TT_F, SEQ), BF16),
                        pltpu.VMEM((BATCH, nt - F_DIRECT + 1, TT_F, FOURIER_WIDTH), BF16)],
        compiler_params=_cparams(("arbitrary", "arbitrary"), 60),
        name="fourier",
    )(u, f1r, f1i, gr, gi, cc, sc, rev)


def _mirror_rows():
    rev = np.zeros((TT_F, 2 * TT_F), np.float32)
    rev[0, TT_F] = 1.0
    rev[np.arange(1, TT_F), TT_F - np.arange(1, TT_F)] = 1.0
    return jnp.asarray(rev).astype(BF16)


def _alibi_slope(h):
    return float(2.0 ** (-8.0 * (h + 1) / N_Q_HEADS))


def _attention_kernel(sink_ref, q_ref, prev_ref, cur_ref, next_ref, sel_ref, o_ref, kz_scr, vz_scr):
    t = pl.program_id(1)
    band = jnp.concatenate([prev_ref[...], cur_ref[...], next_ref[...]], axis=0)
    for g in range(N_KV_HEADS):
        c, half = divmod(g, 2)
        kcol = band[:, c * LANES:(c + 1) * LANES]
        vcol = band[:, KV_WIDTH + c * LANES:KV_WIDTH + (c + 1) * LANES]
        for d in range(2):
            sel = sel_ref[half * 2 + d]
            kz_scr[2 * g + d] = jnp.transpose(_dot(kcol, sel)).astype(BF16)
            vz_scr[2 * g + d] = _dot(vcol, sel).astype(BF16)

    row = lax.broadcasted_iota(jnp.int32, (2 * QB, 3 * QB), 0)
    col = lax.broadcasted_iota(jnp.int32, (2 * QB, 3 * QB), 1)
    absrel = jnp.abs((row & (QB - 1)) - col + QB)
    absrel_f = absrel.astype(F32)
    top = lax.broadcasted_iota(jnp.int32, (2 * QB, 1), 0) < QB
    for i in range(TQ // QB):
        q_rows = slice(i * QB, (i + 1) * QB)
        k_rows = slice(i * QB, (i + 3) * QB)
        kpos = t * TQ + (i - 1) * QB + col
        mask = (absrel <= WINDOW) & (kpos >= 0) & (kpos < SEQ)
        for g in range(N_KV_HEADS):
            cols = [slice((2 * g + p) * LANES, (2 * g + p + 1) * LANES) for p in range(2)]
            qq = jnp.concatenate([q_ref[q_rows, cols[0]], q_ref[q_rows, cols[1]]], axis=0)
            out = jnp.zeros((2 * QB, LANES), F32)
            for d in range(2):
                h_top, h_bot = 4 * g + d, 4 * g + 2 + d
                slope = jnp.where(top, _alibi_slope(h_top), _alibi_slope(h_bot))
                sink = jnp.where(top, sink_ref[h_top], sink_ref[h_bot])
                s = _dot(qq, kz_scr[2 * g + d, :, k_rows])
                s = jnp.where(mask, s - slope * absrel_f, NEG_INF)
                m = jnp.maximum(jnp.max(s, axis=-1, keepdims=True), sink)
                pr = jnp.exp(s - m)
                den = jnp.sum(pr, axis=-1, keepdims=True) + jnp.exp(sink - m)
                out = out + _dot(pr.astype(BF16), vz_scr[2 * g + d, k_rows, :]) / den
            o_ref[q_rows, cols[0]] = out[:QB].astype(BF16)
            o_ref[q_rows, cols[1]] = out[QB:].astype(BF16)


def _attention(sink, q, kv, sel):
    nt = SEQ // TQ
    nb = SEQ // QB
    r = TQ // QB
    grid_spec = pltpu.PrefetchScalarGridSpec(
        num_scalar_prefetch=1,
        grid=(BATCH, nt),
        in_specs=[
            pl.BlockSpec((TQ, Q_WIDTH), lambda b, t, s: (b * nt + t, 0)),
            pl.BlockSpec((QB, 2 * KV_WIDTH), lambda b, t, s: (b * nb + jnp.maximum(t * r - 1, 0), 0)),
            pl.BlockSpec((TQ, 2 * KV_WIDTH), lambda b, t, s: (b * nt + t, 0)),
            pl.BlockSpec((QB, 2 * KV_WIDTH), lambda b, t, s: (b * nb + jnp.minimum(t * r + r, nb - 1), 0)),
            pl.BlockSpec((4, LANES, LANES), lambda b, t, s: (0, 0, 0)),
        ],
        out_specs=pl.BlockSpec((TQ, Q_WIDTH), lambda b, t, s: (b * nt + t, 0)),
        scratch_shapes=[pltpu.VMEM((2 * N_KV_HEADS, LANES, TQ + 2 * QB), BF16),
                        pltpu.VMEM((2 * N_KV_HEADS, TQ + 2 * QB, LANES), BF16)],
    )
    return pl.pallas_call(
        _attention_kernel,
        grid_spec=grid_spec,
        out_shape=jax.ShapeDtypeStruct((N_TOK, Q_WIDTH), BF16),
        compiler_params=_cparams(("arbitrary", "arbitrary"), 32),
        name="attention",
    )(sink, q, kv, kv, kv, sel)


def _merge_kernel(yf_ref, at_ref, gf_ref, ga_ref, x_ref, wfo_ref, wao_ref, wo_ref, g2_ref, wr_ref, br_ref,
                  ltri_ref, x1_ref, h2p_ref, meta_ref, cnt_ref, base_scr):
    i = pl.program_id(0)

    @pl.when(i == 0)
    def _():
        base_scr[...] = jnp.zeros_like(base_scr)

    base = base_scr[...]
    for s in range(TM_MG // MG_SLAB):
        rows = slice(s * MG_SLAB, (s + 1) * MG_SLAB)
        yf = _dot(yf_ref[rows, :], wfo_ref[...])
        ya = _dot(at_ref[rows, :], wao_ref[...])
        merged = gf_ref[rows, :].astype(F32) * yf + ga_ref[rows, :].astype(F32) * ya
        x1 = x_ref[rows, :] + _dot(merged.astype(BF16), wo_ref[...])
        x1_ref[rows, :] = x1
        ms = jnp.mean(x1 * x1, axis=-1, keepdims=True)
        h2 = x1 * lax.rsqrt(ms + RMS_EPS) * g2_ref[...]
        _store_row_tiles(h2p_ref.at[pl.ds(s * MG_SLAB * SUBLANES, MG_SLAB * SUBLANES), :],
                         _pack_halves(h2[:, :HALF], h2[:, HALF:]))
        logits = _dot(h2.astype(BF16), wr_ref[...]) + br_ref[...]

        lane = lax.broadcasted_iota(jnp.int32, logits.shape, 1)
        lane_f = lane.astype(F32)
        vals = logits
        top_v, top_i, onehots = [], [], []
        for _ in range(TOP_K):
            m = jnp.max(vals, axis=-1, keepdims=True)
            idx = jnp.min(jnp.where(vals == m, lane_f, float(LANES)), axis=-1, keepdims=True)
            oh = lane_f == idx
            top_v.append(m)
            top_i.append(idx)
            onehots.append(oh)
            vals = jnp.where(oh, -jnp.inf, vals)
        exps = [jnp.exp(v - top_v[0]) for v in top_v]
        den = exps[0] + exps[1] + exps[2] + exps[3]
        gates = [e / den for e in exps]

        cnt = sum(oh.astype(F32) for oh in onehots)
        prefix = _dot(ltri_ref[...], cnt.astype(BF16))
        tot = base + prefix
        ranks = [jnp.sum(jnp.where(oh, tot, 0.0), axis=-1, keepdims=True) for oh in onehots]
        base = base + jnp.sum(cnt, axis=0, keepdims=True)

        meta = jnp.zeros(logits.shape, F32)
        for k in range(TOP_K):
            meta = jnp.where(lane == k, top_i[k], meta)
            meta = jnp.where(lane == TOP_K + k, ranks[k], meta)
            meta = jnp.where(lane == 2 * TOP_K + k, gates[k], meta)
        meta_ref[rows, :] = meta
    base_scr[...] = base
    cnt_ref[...] = jnp.broadcast_to(base, cnt_ref.shape)


def _merge(yf, attn, gates, x2, wfo, wao, wo, g2, wr, br, ltri):
    tm = TM_MG
    const = lambda i: (0, 0)
    resident = functools.partial(pl.BlockSpec, index_map=const, pipeline_mode=pl.Buffered(1))
    return pl.pallas_call(
        _merge_kernel,
        grid=(N_TOK // tm,),
        in_specs=[
            pl.BlockSpec((tm, FOURIER_WIDTH), lambda i: (i, 0)),
            pl.BlockSpec((tm, Q_WIDTH), lambda i: (i, 0)),
            pl.BlockSpec((tm, D_MODEL), lambda i: (i, 0)),
            pl.BlockSpec((tm, D_MODEL), lambda i: (i, 1)),
            pl.BlockSpec((tm, D_MODEL), lambda i: (i, 0)),
            resident((FOURIER_WIDTH, D_MODEL)),
            resident((Q_WIDTH, D_MODEL)),
            resident((D_MODEL, D_MODEL)),
            pl.BlockSpec((1, D_MODEL), const),
            pl.BlockSpec((D_MODEL, LANES), const),
            pl.BlockSpec((1, LANES), const),
            pl.BlockSpec((MG_SLAB, MG_SLAB), const),
        ],
        out_specs=[
            pl.BlockSpec((tm, D_MODEL), lambda i: (i, 0)),
            pl.BlockSpec((tm * SUBLANES, LANES), lambda i: (i, 0)),
            pl.BlockSpec((tm, LANES), lambda i: (i, 0)),
            pl.BlockSpec((8, LANES), const),
        ],
        out_shape=[
            jax.ShapeDtypeStruct((N_TOK, D_MODEL), F32),
            jax.ShapeDtypeStruct((N_TOK * SUBLANES, LANES), U32),
            jax.ShapeDtypeStruct((N_TOK, LANES), F32),
            jax.ShapeDtypeStruct((8, LANES), F32),
        ],
        scratch_shapes=[pltpu.VMEM((1, LANES), F32)],
        compiler_params=_cparams(("arbitrary",), 56),
        name="merge",
    )(yf, attn, gates, gates, x2, wfo, wao, wo, g2, wr, br, ltri)


def _expert_kernel(e_ref, live_ref, tok_cur_ref, tok_nxt_ref, dst_cur_ref,
                   h2p_hbm, wg_ref, bg_ref, wu_ref, bu_ref, wd_ref, bd_ref,
                   yslots_hbm, xu_scr, acc_scr, xs_scr, wgb, wub, wdb, gsem, ssem):
    del e_ref
    w = pl.program_id(0)
    f = pl.program_id(1)
    live = live_ref[w] > 0
    prev_live = (w > 0) & (live_ref[jnp.maximum(w - 1, 0)] > 0)
    cur = w & 1
    nxt = 1 - cur
    tile_rows = RT * SUBLANES

    def sub_rows(s):
        return pl.ds(pl.multiple_of(s * RT, RT), RT)

    def row_tile(row):
        return pl.ds(pl.multiple_of(row * SUBLANES, SUBLANES), SUBLANES)

    def gather_copy(tok_ref, slot, row):
        tok = tok_ref[0, 0, row]
        return pltpu.make_async_copy(h2p_hbm.at[row_tile(tok), :], xu_scr.at[slot, row_tile(row), :], gsem.at[slot])

    def scatter_copy(row):
        dst = dst_cur_ref[0, 0, row]
        return pltpu.make_async_copy(xu_scr.at[cur, row_tile(row), :], yslots_hbm.at[row_tile(dst), :], ssem)

    def for_all_rows(fn):
        def body(row, carry):
            fn(row)
            return carry
        lax.fori_loop(0, R_MAX, body, 0, unroll=8)

    def prefetch_next(s):
        base = s * RT + f * GATHER_PER_STEP
        for g in range(GATHER_PER_STEP):
            gather_copy(tok_nxt_ref, nxt, base + g).start(priority=g % 2)

    def cast_weights():
        wgb[...] = wg_ref[...].astype(BF16)
        wub[...] = wu_ref[...].astype(BF16)
        wdb[...] = wd_ref[...].astype(BF16)

    def stage_inputs(first_sub, n_sub, stage_row0):
        n_rows = n_sub * RT
        tiles = xu_scr.at[cur, pl.ds(pl.multiple_of(first_sub * tile_rows, tile_rows), n_sub * tile_rows), :]
        stage = xs_scr.at[pl.ds(stage_row0, n_rows), :]
        for j in range(SUBLANES):
            lo, hi = _unpack_halves(_load_row_tiles(tiles, j, n_rows))
            stage[:, j * LANES:(j + 1) * LANES] = lo.astype(BF16)
            stage[:, HALF + j * LANES:HALF + (j + 1) * LANES] = hi.astype(BF16)

    def activations(n_sub, stage_row0):
        x = xs_scr[pl.ds(stage_row0, n_sub * RT), :]
        g = _dot(x, wgb[...]) + bg_ref[...]
        u = _dot(x, wub[...]) + bu_ref[...]
        g = jnp.minimum(g, SWIGLU_LIMIT)
        u = jnp.clip(u, -SWIGLU_LIMIT, SWIGLU_LIMIT)
        act = (u + 1.0) * (g * (1.0 / (1.0 + jnp.exp(-SWIGLU_ALPHA * g))))
        return act.astype(BF16)

    def accumulate(j):
        n_rows = GROUP * RT
        rows = pl.ds(j * n_rows, n_rows)
        stage_inputs(j * GROUP, GROUP, 0)
        c = _dot(activations(GROUP, 0), wdb[...])
        acc_scr[rows, :] = jnp.where(f == 0, c, acc_scr[rows, :] + c)

    def finish(s):
        act = activations(1, (s % GROUP) * RT)
        rows = sub_rows(s)
        tiles = xu_scr.at[cur, pl.ds(s * tile_rows, tile_rows), :]
        for c in range(HALF // OUT_CHUNK):
            lo_cols = slice(c * OUT_CHUNK, (c + 1) * OUT_CHUNK)
            hi_cols = slice(HALF + c * OUT_CHUNK, HALF + (c + 1) * OUT_CHUNK)
            y_lo = acc_scr[rows, lo_cols] + _dot(act, wdb[:, lo_cols]) + bd_ref[:, lo_cols]
            y_hi = acc_scr[rows, hi_cols] + _dot(act, wdb[:, hi_cols]) + bd_ref[:, hi_cols]
            words = _pack_halves(y_lo, y_hi)
            for jj in range(OUT_CHUNK // LANES):
                j_tile = c * (OUT_CHUNK // LANES) + jj
                tiles[pl.ds(j_tile, RT, stride=SUBLANES), :] = words[:, jj * LANES:(jj + 1) * LANES]

    first_tile = pl.ds(0, SUBLANES)

    @pl.when(f == 0)
    def _():
        @pl.when(w == 0)
        def _():
            def zero(s, carry):
                acc_scr[sub_rows(s), :] = jnp.zeros((RT, D_MODEL), F32)
                xu_scr[1, pl.ds(pl.multiple_of(s * tile_rows, tile_rows), tile_rows), :] = (
                    jnp.zeros((tile_rows, LANES), U32))
                return carry
            lax.fori_loop(0, N_SUB, zero, 0)
            pad_rows = pl.ds(TOP_K * N_TOK * SUBLANES, R_MAX * SUBLANES)
            fill = pltpu.make_async_copy(xu_scr.at[1], yslots_hbm.at[pad_rows, :], ssem)
            fill.start()
            fill.wait()
            for_all_rows(lambda row: gather_copy(tok_cur_ref, cur, row).start())

        @pl.when((w == 0) | prev_live)
        def _():
            row_in = pltpu.make_async_copy(h2p_hbm.at[first_tile, :], xu_scr.at[cur, first_tile, :], gsem.at[cur])
            for_all_rows(lambda row: row_in.wait())

        @pl.when(prev_live)
        def _():
            row_out = pltpu.make_async_copy(xu_scr.at[cur, first_tile, :], yslots_hbm.at[first_tile, :], ssem)
            for_all_rows(lambda row: row_out.wait())

    @pl.when(live & (f < NF - 1))
    def _():
        cast_weights()
        for j in range(N_SUB // GROUP):
            accumulate(j)
            for i in range(GROUP):
                prefetch_next(j * GROUP + i)

    @pl.when(live & (f == NF - 1))
    def _():
        cast_weights()
        stage_inputs(0, 1, 0)
        for s in range(N_SUB):
            finish(s)
            if s + 1 < N_SUB:
                stage_inputs(s + 1, 1, ((s + 1) % GROUP) * RT)
            for r in range(RT):
                scatter_copy(s * RT + r).start(priority=r % 2)
            prefetch_next(s)


def _experts(item_e, item_live, item_tok, item_dst, h2p, wg, bg, wu, bu, wd, bd):
    def w_in_map(w, f, e, lv):
        return (e[w], 0, jnp.where(lv[w] > 0, f, NF - 1))

    def w_down_map(w, f, e, lv):
        return (e[w], jnp.where(lv[w] > 0, f, NF - 1), 0)

    def b_down_map(w, f, e, lv):
        return (e[w], 0, 0)

    grid_spec = pltpu.PrefetchScalarGridSpec(
        num_scalar_prefetch=2,
        grid=(jnp.sum(item_live) + 1, NF),
        in_specs=[
            pl.BlockSpec((1, 1, R_MAX), lambda w, f, e, lv: (w, 0, 0), memory_space=pltpu.SMEM),
            pl.BlockSpec((1, 1, R_MAX), lambda w, f, e, lv: (jnp.minimum(w + 1, N_ITEMS - 1), 0, 0),
                         memory_space=pltpu.SMEM),
            pl.BlockSpec((1, 1, R_MAX), lambda w, f, e, lv: (w, 0, 0), memory_space=pltpu.SMEM),
            pl.BlockSpec(memory_space=pl.ANY),
            pl.BlockSpec((None, D_MODEL, FC), w_in_map),
            pl.BlockSpec((None, 1, FC), w_in_map),
            pl.BlockSpec((None, D_MODEL, FC), w_in_map),
            pl.BlockSpec((None, 1, FC), w_in_map),
            pl.BlockSpec((None, FC, D_MODEL), w_down_map),
            pl.BlockSpec((None, 1, D_MODEL), b_down_map),
        ],
        out_specs=pl.BlockSpec(memory_space=pl.ANY),
        scratch_shapes=[
            pltpu.VMEM((2, R_MAX * SUBLANES, LANES), U32),
            pltpu.VMEM((R_MAX, D_MODEL), F32),
            pltpu.VMEM((GROUP * RT, D_MODEL), BF16),
            pltpu.VMEM((D_MODEL, FC), BF16),
            pltpu.VMEM((D_MODEL, FC), BF16),
            pltpu.VMEM((FC, D_MODEL), BF16),
            pltpu.SemaphoreType.DMA((2,)),
            pltpu.SemaphoreType.DMA(()),
        ],
    )
    return pl.pallas_call(
        _expert_kernel,
        grid_spec=grid_spec,
        out_shape=jax.ShapeDtypeStruct((SLOT_ROWS * SUBLANES, LANES), U32),
        compiler_params=_cparams(("arbitrary", "arbitrary"), 60),
        name="experts",
    )(item_e, item_live, item_tok, item_tok, item_dst, h2p, wg, bg, wu, bu, wd, bd)


def _combine_kernel(meta_ref, x1_ref, y0_ref, y1_ref, y2_ref, y3_ref, out_ref):
    gates = [meta_ref[:, 2 * TOP_K + k:2 * TOP_K + k + 1] for k in range(TOP_K)]
    for j in range(SUBLANES):
        lo_cols = slice(j * LANES, (j + 1) * LANES)
        hi_cols = slice(HALF + j * LANES, HALF + (j + 1) * LANES)
        lo_acc = x1_ref[:, lo_cols]
        hi_acc = x1_ref[:, hi_cols]
        for k, y_ref in enumerate((y0_ref, y1_ref, y2_ref, y3_ref)):
            lo, hi = _unpack_halves(_load_row_tiles(y_ref, j, TM_CB))
            lo_acc = lo_acc + gates[k] * lo
            hi_acc = hi_acc + gates[k] * hi
        out_ref[:, lo_cols] = lo_acc
        out_ref[:, hi_cols] = hi_acc


def _combine(meta, x1, y_slots):
    tm = TM_CB
    nblk = N_TOK // tm
    slot_specs = [pl.BlockSpec((tm * SUBLANES, LANES), lambda i, k=k: (k * nblk + i, 0)) for k in range(TOP_K)]
    return pl.pallas_call(
        _combine_kernel,
        grid=(nblk,),
        in_specs=[
            pl.BlockSpec((tm, LANES), lambda i: (i, 0)),
            pl.BlockSpec((tm, D_MODEL), lambda i: (i, 0)),
        ] + slot_specs,
        out_specs=pl.BlockSpec((tm, D_MODEL), lambda i: (i, 0)),
        out_shape=jax.ShapeDtypeStruct((N_TOK, D_MODEL), F32),
        compiler_params=_cparams(("arbitrary",), 32),
        name="combine",
    )(meta, x1, y_slots, y_slots, y_slots, y_slots)


def _dft_tables():
    r = DFT_RADIX
    k = np.arange(r)
    e64 = np.exp(2j * np.pi * np.outer(k, k) / r)
    e4096 = np.exp(2j * np.pi * np.outer(k, k) / SEQ)
    s = np.arange(SEQ)
    f1 = e64[:, s % r]
    g = e64[:, s // r] * e4096[:, s % r] / math.sqrt(SEQ)
    c = np.arange(FOURIER_GROUP_DIM)
    ang = 2.0 * np.pi * np.outer(c, c) / FOURIER_GROUP_DIM
    scale = 1.0 / math.sqrt(FOURIER_GROUP_DIM)
    as32 = lambda a: jnp.asarray(a.astype(np.float32))
    return (as32(f1.real), as32(f1.imag), as32(g.real), as32(g.imag),
            jnp.asarray((np.cos(ang) * scale).astype(np.float32)).astype(BF16),
            jnp.asarray((np.sin(ang) * scale).astype(np.float32)).astype(BF16))


def _head_selectors():
    i = np.arange(LANES)[:, None]
    j = np.arange(LANES)[None, :]
    out = np.zeros((4, LANES, LANES), np.float32)
    for src in range(2):
        for dst in range(2):
            out[2 * src + dst] = (i - HEAD_DIM * src == j - HEAD_DIM * dst) & (j // HEAD_DIM == dst)
    return jnp.asarray(out).astype(BF16)


def kernel(x, norm1_g, w_in, b_branch_gate, q_norm_g, k_norm_g, attn_sink, w_fourier_out, w_attn_out, w_o,
           norm2_g, w_router, b_router, w_gate_e, b_gate_e, w_up_e, b_up_e, w_down_e, b_down_e):
    b, s, d = x.shape
    assert (b, s, d) == (BATCH, SEQ, D_MODEL) and norm1_g.shape[0] == 1
    x2 = x.reshape(N_TOK, D_MODEL)

    heads_per_tile = TN_IN // HEAD_DIM
    ones_bd = jnp.asarray((np.arange(TN_IN)[:, None] // HEAD_DIM == np.arange(TN_IN)[None, :] // HEAD_DIM)
                          .astype(np.float32)).astype(BF16)
    u_f, q, kv, gates = _in_proj(
        x2, norm1_g[0].reshape(1, D_MODEL), w_in[0].astype(BF16), b_branch_gate[0].reshape(1, 2 * D_MODEL),
        jnp.tile(q_norm_g[0], heads_per_tile).reshape(1, TN_IN),
        jnp.tile(k_norm_g[0], heads_per_tile).reshape(1, TN_IN), ones_bd)
    y_f = _fourier(u_f, *_dft_tables(), _mirror_rows())
    attn = _attention(attn_sink[0], q, kv, _head_selectors())

    wr = jnp.zeros((D_MODEL, LANES), BF16).at[:, :N_EXPERTS].set(w_router[0].astype(BF16))
    br = jnp.full((1, LANES), NEG_INF, F32).at[0, :N_EXPERTS].set(b_router[0])
    ltri = jnp.asarray(np.tril(np.ones((MG_SLAB, MG_SLAB), np.float32), -1)).astype(BF16)
    x1, h2p, meta, cnt = _merge(y_f, attn, gates, x2, w_fourier_out[0].astype(BF16), w_attn_out[0].astype(BF16),
                                w_o[0].astype(BF16), norm2_g[0].reshape(1, D_MODEL), wr, br, ltri)

    tables = _routing_tables(cnt[0, :N_EXPERTS].astype(jnp.int32), meta[:, 0:TOP_K].astype(jnp.int32),
                             meta[:, TOP_K:2 * TOP_K].astype(jnp.int32))
    y_slots = _experts(*tables, h2p,
                       w_gate_e[0], b_gate_e[0].reshape(N_EXPERTS, 1, D_EXPERT),
                       w_up_e[0], b_up_e[0].reshape(N_EXPERTS, 1, D_EXPERT),
                       w_down_e[0], b_down_e[0].reshape(N_EXPERTS, 1, D_MODEL))
    out = _combine(meta, x1, y_slots)
    return out.reshape(BATCH, SEQ, D_MODEL)


def _routing_tables(counts, top_idx, rank):
    padded = ((counts + RT - 1) // RT) * RT
    pend = jnp.cumsum(padded)
    pstart = pend - padded
    dest = pstart[top_idx] + rank
    n_assign = TOP_K * N_TOK
    row_assign = jnp.full((ROWS,), -1, jnp.int32).at[dest.reshape(n_assign)].set(
        jnp.arange(n_assign, dtype=jnp.int32), unique_indices=True)

    items_per_e = (padded + R_MAX - 1) // R_MAX
    it_end = jnp.cumsum(items_per_e)
    it_start = it_end - items_per_e
    total_items = it_end[-1]
    wi = jnp.arange(N_ITEMS, dtype=jnp.int32)
    e_w = jnp.minimum(jnp.searchsorted(it_end, wi, side="right"), N_EXPERTS - 1).astype(jnp.int32)
    j_w = wi - it_start[e_w]
    live = wi < total_items
    rows_w = jnp.clip(padded[e_w] - j_w * R_MAX, 0, R_MAX)
    e_last = e_w[jnp.maximum(total_items - 1, 0)]
    item_e = jnp.where(live, e_w, e_last).astype(jnp.int32)
    item_row0 = jnp.where(live, pstart[e_w] + j_w * R_MAX, 0).astype(jnp.int32)
    item_nsub = jnp.where(live, rows_w // RT, 0).astype(jnp.int32)
    item_live = live.astype(jnp.int32)
    r = jnp.arange(R_MAX, dtype=jnp.int32)
    src = jnp.minimum(item_row0[:, None] + r[None, :], ROWS - 1)
    item_rows = jnp.where(r[None, :] < (item_nsub * RT)[:, None], row_assign[src], -1)
    item_tok = (jnp.maximum(item_rows, 0) // TOP_K).reshape(N_ITEMS, 1, R_MAX)
    item_dst = jnp.where(item_rows >= 0, (item_rows % TOP_K) * N_TOK + item_rows // TOP_K,
                         TOP_K * N_TOK + r[None, :]).reshape(N_ITEMS, 1, R_MAX)
    return item_e, item_live, item_tok, item_dst
```

```python
import functools
import math

import jax
import jax.numpy as jnp
import numpy as np
from jax import lax
from jax.experimental import pallas as pl
from jax.experimental.pallas import tpu as pltpu

D_MODEL = 2048
BATCH = 4
SEQ = 4096
N_TOK = BATCH * SEQ
N_Q_HEADS = 16
N_KV_HEADS = 4
HEAD_DIM = 64
Q_WIDTH = N_Q_HEADS * HEAD_DIM
KV_WIDTH = N_KV_HEADS * HEAD_DIM
WINDOW = 128
N_FOURIER_GROUPS = 4
FOURIER_GROUP_DIM = 256
FOURIER_WIDTH = N_FOURIER_GROUPS * FOURIER_GROUP_DIM
IN_WIDTH = FOURIER_WIDTH + Q_WIDTH + 2 * KV_WIDTH + 2 * D_MODEL
N_EXPERTS = 32
TOP_K = 4
D_EXPERT = D_MODEL
SWIGLU_LIMIT = 7.0
SWIGLU_ALPHA = 1.702
RMS_EPS = 1e-6
NEG_INF = -1e30

F32 = jnp.float32
BF16 = jnp.bfloat16
U32 = jnp.uint32

V7X_VMEM_BYTES = 64 * 1024 * 1024
LANES = 128
SUBLANES = 8
MIB = 1024 * 1024

TM_IN = 512
TN_IN = 512
NORM_ROWS = 64
TT_F = 512
F_DIRECT = SEQ // TT_F // 2 + 1
DFT_RADIX = 64
TQ = 512
QB = 128
TM_MG = 512
MG_SLAB = 256
RT = 256
R_MAX = 9 * RT
FC = 256
NF = D_EXPERT // FC
ROWS = TOP_K * N_TOK + N_EXPERTS * RT
N_ITEMS = (TOP_K * N_TOK + N_EXPERTS * (RT - 1) + N_EXPERTS * (R_MAX - RT)) // R_MAX + 1
TM_CB = 256
HALF = D_MODEL // 2
N_SUB = R_MAX // RT
GROUP = 3
OUT_CHUNK = 256
GATHER_PER_STEP = RT // NF
SLOT_ROWS = TOP_K * N_TOK + R_MAX


def _cparams(sem, vmem_mib):
    return pltpu.CompilerParams(dimension_semantics=sem, vmem_limit_bytes=vmem_mib * MIB)


def _dot(a, b):
    return jnp.dot(a, b, preferred_element_type=F32)


def _pack_halves(lo, hi):
    lo_bits = pltpu.bitcast(lo.astype(BF16).astype(F32), U32)
    hi_bits = pltpu.bitcast(hi.astype(BF16).astype(F32), U32)
    return (hi_bits & jnp.uint32(0xFFFF0000)) | (lo_bits >> jnp.uint32(16))


def _unpack_halves(words):
    lo = pltpu.bitcast(words << jnp.uint32(16), F32)
    hi = pltpu.bitcast(words & jnp.uint32(0xFFFF0000), F32)
    return lo, hi


def _store_row_tiles(tiles_ref, words):
    n_rows = words.shape[0]
    for j in range(SUBLANES):
        tiles_ref[pl.ds(j, n_rows, stride=SUBLANES), :] = words[:, j * LANES:(j + 1) * LANES]


def _load_row_tiles(tiles_ref, j, n_rows):
    return tiles_ref[pl.ds(j, n_rows, stride=SUBLANES), :]


def _in_proj_kernel(x_ref, g1_ref, w_ref, b_ref, qg_ref, kg_ref, ones_ref,
                    u_ref, q_ref, kv_ref, gate_ref, h_scr):
    def body(c, carry):
        rows = pl.ds(pl.multiple_of(c * NORM_ROWS, NORM_ROWS), NORM_ROWS)
        x = x_ref[rows, :]
        ms = jnp.mean(x * x, axis=-1, keepdims=True)
        h_scr[rows, :] = (x * lax.rsqrt(ms + RMS_EPS) * g1_ref[...]).astype(BF16)
        return carry
    lax.fori_loop(0, TM_IN // NORM_ROWS, body, 0)

    def head_norm(a, gain):
        ssq = _dot((a * a).astype(BF16), ones_ref[...])
        return a * lax.rsqrt(ssq * (1.0 / HEAD_DIM) + RMS_EPS) * gain

    h = h_scr[...]
    for c in range(IN_WIDTH // TN_IN):
        col0 = c * TN_IN
        acc = _dot(h, w_ref[:, col0:col0 + TN_IN])
        if col0 < FOURIER_WIDTH:
            u_ref[:, col0:col0 + TN_IN] = acc.astype(BF16)
        elif col0 < FOURIER_WIDTH + Q_WIDTH:
            o = col0 - FOURIER_WIDTH
            q_ref[:, o:o + TN_IN] = (head_norm(acc, qg_ref[...]) * (HEAD_DIM ** -0.5)).astype(BF16)
        elif col0 < FOURIER_WIDTH + Q_WIDTH + 2 * KV_WIDTH:
            lane = lax.broadcasted_iota(jnp.int32, acc.shape, 1)
            kv_ref[...] = jnp.where(lane < KV_WIDTH, head_norm(acc, kg_ref[...]), acc).astype(BF16)
        else:
            o = col0 - (FOURIER_WIDTH + Q_WIDTH + 2 * KV_WIDTH)
            z = acc + b_ref[:, o:o + TN_IN]
            gate_ref[:, o:o + TN_IN] = (1.0 / (1.0 + jnp.exp(-z))).astype(BF16)


def _in_proj(x2, g1, w_bf, bias, qg_t, kg_t, ones_bd):
    const = lambda i: (0, 0)
    row = lambda i: (i, 0)
    return pl.pallas_call(
        _in_proj_kernel,
        grid=(N_TOK // TM_IN,),
        in_specs=[
            pl.BlockSpec((TM_IN, D_MODEL), row),
            pl.BlockSpec((1, D_MODEL), const),
            pl.BlockSpec((D_MODEL, IN_WIDTH), const, pipeline_mode=pl.Buffered(1)),
            pl.BlockSpec((1, 2 * D_MODEL), const),
            pl.BlockSpec((1, TN_IN), const),
            pl.BlockSpec((1, TN_IN), const),
            pl.BlockSpec((TN_IN, TN_IN), const),
        ],
        out_specs=[
            pl.BlockSpec((TM_IN, FOURIER_WIDTH), row),
            pl.BlockSpec((TM_IN, Q_WIDTH), row),
            pl.BlockSpec((TM_IN, 2 * KV_WIDTH), row),
            pl.BlockSpec((TM_IN, 2 * D_MODEL), row),
        ],
        out_shape=[
            jax.ShapeDtypeStruct((N_TOK, FOURIER_WIDTH), BF16),
            jax.ShapeDtypeStruct((N_TOK, Q_WIDTH), BF16),
            jax.ShapeDtypeStruct((N_TOK, 2 * KV_WIDTH), BF16),
            jax.ShapeDtypeStruct((N_TOK, 2 * D_MODEL), BF16),
        ],
        scratch_shapes=[pltpu.VMEM((TM_IN, D_MODEL), BF16)],
        compiler_params=_cparams(("arbitrary",), 60),
        name="in_proj",
    )(x2, g1, w_bf, bias, qg_t, kg_t, ones_bd)


def _fourier_kernel(u_ref, f1r_ref, f1i_ref, gr_ref, gi_ref, cc_ref, sc_ref, rev_ref, y_ref, cs_scr, ss_scr, pq_scr):
    t = pl.program_id(0)
    b = pl.program_id(1)
    nt = SEQ // TT_F

    @pl.when(t < F_DIRECT)
    def _():
        @pl.when(b == 0)
        def _():
            gr = gr_ref[...]
            gi = gi_ref[...]
            for a in range(TT_F // DFT_RADIX):
                f1r = f1r_ref[a:a + 1, :]
                f1i = f1i_ref[a:a + 1, :]
                rows = slice(a * DFT_RADIX, (a + 1) * DFT_RADIX)
                cs_scr[rows, :] = (f1r * gr - f1i * gi).astype(BF16)
                ss_scr[rows, :] = (f1r * gi + f1i * gr).astype(BF16)

        u = u_ref[...]
        a_seq = _dot(cs_scr[...], u)
        b_seq = _dot(ss_scr[...], u)
        for g in range(N_FOURIER_GROUPS):
            cols = slice(g * FOURIER_GROUP_DIM, (g + 1) * FOURIER_GROUP_DIM)
            p = _dot(a_seq[:, cols].astype(BF16), cc_ref[...])
            q = _dot(b_seq[:, cols].astype(BF16), sc_ref[...])
            y_ref[:, cols] = (p - q).astype(BF16)

            @pl.when(t < nt - F_DIRECT + 1)
            def _():
                pq_scr[b, t, :, cols] = (p + q).astype(BF16)

    @pl.when(t >= F_DIRECT)
    def _():
        src = jnp.concatenate([pq_scr[b, nt - 1 - t], pq_scr[b, nt - t]], axis=0)
        y_ref[...] = _dot(rev_ref[...], src).astype(BF16)


def _fourier(u, f1r, f1i, gr, gi, cc, sc, rev):
    nt = SEQ // TT_F
    f1_rows = TT_F // DFT_RADIX
    const = lambda t, b: (0, 0)
    direct = lambda t: jnp.minimum(t, F_DIRECT - 1)
    return pl.pallas_call(
        _fourier_kernel,
        grid=(nt, BATCH),
        in_specs=[
            pl.BlockSpec((SEQ, FOURIER_WIDTH), lambda t, b: (jnp.where(t < F_DIRECT, b, BATCH - 1), 0)),
            pl.BlockSpec((f1_rows, SEQ), lambda t, b: (direct(t), 0)),
            pl.BlockSpec((f1_rows, SEQ), lambda t, b: (direct(t), 0)),
            pl.BlockSpec((DFT_RADIX, SEQ), const),
            pl.BlockSpec((DFT_RADIX, SEQ), const),
            pl.BlockSpec((FOURIER_GROUP_DIM, FOURIER_GROUP_DIM), const),
            pl.BlockSpec((FOURIER_GROUP_DIM, FOURIER_GROUP_DIM), const),
            pl.BlockSpec((TT_F, 2 * TT_F), const),
        ],
        out_specs=pl.BlockSpec((TT_F, FOURIER_WIDTH), lambda t, b: (b * nt + t, 0)),
        out_shape=jax.ShapeDtypeStruct((N_TOK, FOURIER_WIDTH), BF16),
        scratch_shapes=[pltpu.VMEM((TT_F, SEQ), BF16), pltpu.VMEM((TT_F, SEQ), BF16),
                        pltpu.VMEM((BATCH, nt - F_DIRECT + 1, TT_F, FOURIER_WIDTH), BF16)],
        compiler_params=_cparams(("arbitrary", "arbitrary"), 60),
        name="fourier",
    )(u, f1r, f1i, gr, gi, cc, sc, rev)


def _mirror_rows():
    rev = np.zeros((TT_F, 2 * TT_F), np.float32)
    rev[0, TT_F] = 1.0
    rev[np.arange(1, TT_F), TT_F - np.arange(1, TT_F)] = 1.0
    return jnp.asarray(rev).astype(BF16)


def _alibi_slope(h):
    return float(2.0 ** (-8.0 * (h + 1) / N_Q_HEADS))


def _attention_kernel(sink_ref, q_ref, prev_ref, cur_ref, next_ref, sel_ref, o_ref, kz_scr, vz_scr):
    t = pl.program_id(1)
    band = jnp.concatenate([prev_ref[...], cur_ref[...], next_ref[...]], axis=0)
    for g in range(N_KV_HEADS):
        c, half = divmod(g, 2)
        kcol = band[:, c * LANES:(c + 1) * LANES]
        vcol = band[:, KV_WIDTH + c * LANES:KV_WIDTH + (c + 1) * LANES]
        for d in range(2):
            sel = sel_ref[half * 2 + d]
            kz_scr[2 * g + d] = jnp.transpose(_dot(kcol, sel)).astype(BF16)
            vz_scr[2 * g + d] = _dot(vcol, sel).astype(BF16)

    row = lax.broadcasted_iota(jnp.int32, (2 * QB, 3 * QB), 0)
    col = lax.broadcasted_iota(jnp.int32, (2 * QB, 3 * QB), 1)
    absrel = jnp.abs((row & (QB - 1)) - col + QB)
    absrel_f = absrel.astype(F32)
    top = lax.broadcasted_iota(jnp.int32, (2 * QB, 1), 0) < QB
    for i in range(TQ // QB):
        q_rows = slice(i * QB, (i + 1) * QB)
        k_rows = slice(i * QB, (i + 3) * QB)
        kpos = t * TQ + (i - 1) * QB + col
        mask = (absrel <= WINDOW) & (kpos >= 0) & (kpos < SEQ)
        for g in range(N_KV_HEADS):
            cols = [slice((2 * g + p) * LANES, (2 * g + p + 1) * LANES) for p in range(2)]
            qq = jnp.concatenate([q_ref[q_rows, cols[0]], q_ref[q_rows, cols[1]]], axis=0)
            out = jnp.zeros((2 * QB, LANES), F32)
            for d in range(2):
                h_top, h_bot = 4 * g + d, 4 * g + 2 + d
                slope = jnp.where(top, _alibi_slope(h_top), _alibi_slope(h_bot))
                sink = jnp.where(top, sink_ref[h_top], sink_ref[h_bot])
                s = _dot(qq, kz_scr[2 * g + d, :, k_rows])
                s = jnp.where(mask, s - slope * absrel_f, NEG_INF)
                m = jnp.maximum(jnp.max(s, axis=-1, keepdims=True), sink)
                pr = jnp.exp(s - m)
                den = jnp.sum(pr, axis=-1, keepdims=True) + jnp.exp(sink - m)
                out = out + _dot(pr.astype(BF16), vz_scr[2 * g + d, k_rows, :]) / den
            o_ref[q_rows, cols[0]] = out[:QB].astype(BF16)
            o_ref[q_rows, cols[1]] = out[QB:].astype(BF16)


def _attention(sink, q, kv, sel):
    nt = SEQ // TQ
    nb = SEQ // QB
    r = TQ // QB
    grid_spec = pltpu.PrefetchScalarGridSpec(
        num_scalar_prefetch=1,
        grid=(BATCH, nt),
        in_specs=[
            pl.BlockSpec((TQ, Q_WIDTH), lambda b, t, s: (b * nt + t, 0)),
            pl.BlockSpec((QB, 2 * KV_WIDTH), lambda b, t, s: (b * nb + jnp.maximum(t * r - 1, 0), 0)),
            pl.BlockSpec((TQ, 2 * KV_WIDTH), lambda b, t, s: (b * nt + t, 0)),
            pl.BlockSpec((QB, 2 * KV_WIDTH), lambda b, t, s: (b * nb + jnp.minimum(t * r + r, nb - 1), 0)),
            pl.BlockSpec((4, LANES, LANES), lambda b, t, s: (0, 0, 0)),
        ],
        out_specs=pl.BlockSpec((TQ, Q_WIDTH), lambda b, t, s: (b * nt + t, 0)),
        scratch_shapes=[pltpu.VMEM((2 * N_KV_HEADS, LANES, TQ + 2 * QB), BF16),
                        pltpu.VMEM((2 * N_KV_HEADS, TQ + 2 * QB, LANES), BF16)],
    )
    return pl.pallas_call(
        _attention_kernel,
        grid_spec=grid_spec,
        out_shape=jax.ShapeDtypeStruct((N_TOK, Q_WIDTH), BF16),
        compiler_params=_cparams(("arbitrary", "arbitrary"), 32),
        name="attention",
    )(sink, q, kv, kv, kv, sel)


def _merge_kernel(yf_ref, at_ref, gf_ref, ga_ref, x_ref, wfo_ref, wao_ref, wo_ref, g2_ref, wr_ref, br_ref,
                  ltri_ref, x1_ref, h2p_ref, meta_ref, cnt_ref, base_scr):
    i = pl.program_id(0)

    @pl.when(i == 0)
    def _():
        base_scr[...] = jnp.zeros_like(base_scr)

    base = base_scr[...]
    for s in range(TM_MG // MG_SLAB):
        rows = slice(s * MG_SLAB, (s + 1) * MG_SLAB)
        yf = _dot(yf_ref[rows, :], wfo_ref[...])
        ya = _dot(at_ref[rows, :], wao_ref[...])
        merged = gf_ref[rows, :].astype(F32) * yf + ga_ref[rows, :].astype(F32) * ya
        x1 = x_ref[rows, :] + _dot(merged.astype(BF16), wo_ref[...])
        x1_ref[rows, :] = x1
        ms = jnp.mean(x1 * x1, axis=-1, keepdims=True)
        h2 = x1 * lax.rsqrt(ms + RMS_EPS) * g2_ref[...]
        _store_row_tiles(h2p_ref.at[pl.ds(s * MG_SLAB * SUBLANES, MG_SLAB * SUBLANES), :],
                         _pack_halves(h2[:, :HALF], h2[:, HALF:]))
        logits = _dot(h2.astype(BF16), wr_ref[...]) + br_ref[...]

        lane = lax.broadcasted_iota(jnp.int32, logits.shape, 1)
        lane_f = lane.astype(F32)
        vals = logits
        top_v, top_i, onehots = [], [], []
        for _ in range(TOP_K):
            m = jnp.max(vals, axis=-1, keepdims=True)
            idx = jnp.min(jnp.where(vals == m, lane_f, float(LANES)), axis=-1, keepdims=True)
            oh = lane_f == idx
            top_v.append(m)
            top_i.append(idx)
            onehots.append(oh)
            vals = jnp.where(oh, -jnp.inf, vals)
        exps = [jnp.exp(v - top_v[0]) for v in top_v]
        den = exps[0] + exps[1] + exps[2] + exps[3]
        gates = [e / den for e in exps]

        cnt = sum(oh.astype(F32) for oh in onehots)
        prefix = _dot(ltri_ref[...], cnt.astype(BF16))
        tot = base + prefix
        ranks = [jnp.sum(jnp.where(oh, tot, 0.0), axis=-1, keepdims=True) for oh in onehots]
        base = base + jnp.sum(cnt, axis=0, keepdims=True)

        meta = jnp.zeros(logits.shape, F32)
        for k in range(TOP_K):
            meta = jnp.where(lane == k, top_i[k], meta)
            meta = jnp.where(lane == TOP_K + k, ranks[k], meta)
            meta = jnp.where(lane == 2 * TOP_K + k, gates[k], meta)
        meta_ref[rows, :] = meta
    base_scr[...] = base
    cnt_ref[...] = jnp.broadcast_to(base, cnt_ref.shape)


def _merge(yf, attn, gates, x2, wfo, wao, wo, g2, wr, br, ltri):
    tm = TM_MG
    const = lambda i: (0, 0)
    resident = functools.partial(pl.BlockSpec, index_map=const, pipeline_mode=pl.Buffered(1))
    return pl.pallas_call(
        _merge_kernel,
        grid=(N_TOK // tm,),
        in_specs=[
            pl.BlockSpec((tm, FOURIER_WIDTH), lambda i: (i, 0)),
            pl.BlockSpec((tm, Q_WIDTH), lambda i: (i, 0)),
            pl.BlockSpec((tm, D_MODEL), lambda i: (i, 0)),
            pl.BlockSpec((tm, D_MODEL), lambda i: (i, 1)),
            pl.BlockSpec((tm, D_MODEL), lambda i: (i, 0)),
            resident((FOURIER_WIDTH, D_MODEL)),
            resident((Q_WIDTH, D_MODEL)),
            resident((D_MODEL, D_MODEL)),
            pl.BlockSpec((1, D_MODEL), const),
            pl.BlockSpec((D_MODEL, LANES), const),
            pl.BlockSpec((1, LANES), const),
            pl.BlockSpec((MG_SLAB, MG_SLAB), const),
        ],
        out_specs=[
            pl.BlockSpec((tm, D_MODEL), lambda i: (i, 0)),
            pl.BlockSpec((tm * SUBLANES, LANES), lambda i: (i, 0)),
            pl.BlockSpec((tm, LANES), lambda i: (i, 0)),
            pl.BlockSpec((8, LANES), const),
        ],
        out_shape=[
            jax.ShapeDtypeStruct((N_TOK, D_MODEL), F32),
            jax.ShapeDtypeStruct((N_TOK * SUBLANES, LANES), U32),
            jax.ShapeDtypeStruct((N_TOK, LANES), F32),
            jax.ShapeDtypeStruct((8, LANES), F32),
        ],
        scratch_shapes=[pltpu.VMEM((1, LANES), F32)],
        compiler_params=_cparams(("arbitrary",), 56),
        name="merge",
    )(yf, attn, gates, gates, x2, wfo, wao, wo, g2, wr, br, ltri)


def _expert_kernel(e_ref, live_ref, tok_cur_ref, tok_nxt_ref, dst_cur_ref,
                   h2p_hbm, wg_ref, bg_ref, wu_ref, bu_ref, wd_ref, bd_ref,
                   yslots_hbm, xu_scr, acc_scr, xs_scr, wgb, wub, wdb, gsem, ssem):
    del e_ref
    w = pl.program_id(0)
    f = pl.program_id(1)
    live = live_ref[w] > 0
    prev_live = (w > 0) & (live_ref[jnp.maximum(w - 1, 0)] > 0)
    cur = w & 1
    nxt = 1 - cur
    tile_rows = RT * SUBLANES

    def sub_rows(s):
        return pl.ds(pl.multiple_of(s * RT, RT), RT)

    def row_tile(row):
        return pl.ds(pl.multiple_of(row * SUBLANES, SUBLANES), SUBLANES)

    def gather_copy(tok_ref, slot, row):
        tok = tok_ref[0, 0, row]
        return pltpu.make_async_copy(h2p_hbm.at[row_tile(tok), :], xu_scr.at[slot, row_tile(row), :], gsem.at[slot])

    def scatter_copy(row):
        dst = dst_cur_ref[0, 0, row]
        return pltpu.make_async_copy(xu_scr.at[cur, row_tile(row), :], yslots_hbm.at[row_tile(dst), :], ssem)

    def for_all_rows(fn):
        def body(row, carry):
            fn(row)
            return carry
        lax.fori_loop(0, R_MAX, body, 0, unroll=8)

    def prefetch_next(s):
        base = s * RT + f * GATHER_PER_STEP
        for g in range(GATHER_PER_STEP):
            gather_copy(tok_nxt_ref, nxt, base + g).start(priority=g % 2)

    def cast_weights():
        wgb[...] = wg_ref[...].astype(BF16)
        wub[...] = wu_ref[...].astype(BF16)
        wdb[...] = wd_ref[...].astype(BF16)

    def stage_inputs(first_sub, n_sub, stage_row0):
        n_rows = n_sub * RT
        tiles = xu_scr.at[cur, pl.ds(pl.multiple_of(first_sub * tile_rows, tile_rows), n_sub * tile_rows), :]
        stage = xs_scr.at[pl.ds(stage_row0, n_rows), :]
        for j in range(SUBLANES):
            lo, hi = _unpack_halves(_load_row_tiles(tiles, j, n_rows))
            stage[:, j * LANES:(j + 1) * LANES] = lo.astype(BF16)
            stage[:, HALF + j * LANES:HALF + (j + 1) * LANES] = hi.astype(BF16)

    def activations(n_sub, stage_row0):
        x = xs_scr[pl.ds(stage_row0, n_sub * RT), :]
        g = _dot(x, wgb[...]) + bg_ref[...]
        u = _dot(x, wub[...]) + bu_ref[...]
        g = jnp.minimum(g, SWIGLU_LIMIT)
        u = jnp.clip(u, -SWIGLU_LIMIT, SWIGLU_LIMIT)
        act = (u + 1.0) * (g * (1.0 / (1.0 + jnp.exp(-SWIGLU_ALPHA * g))))
        return act.astype(BF16)

    def accumulate(j):
        n_rows = GROUP * RT
        rows = pl.ds(j * n_rows, n_rows)
        stage_inputs(j * GROUP, GROUP, 0)
        c = _dot(activations(GROUP, 0), wdb[...])
        acc_scr[rows, :] = jnp.where(f == 0, c, acc_scr[rows, :] + c)

    def finish(s):
        act = activations(1, (s % GROUP) * RT)
        rows = sub_rows(s)
        tiles = xu_scr.at[cur, pl.ds(s * tile_rows, tile_rows), :]
        for c in range(HALF // OUT_CHUNK):
            lo_cols = slice(c * OUT_CHUNK, (c + 1) * OUT_CHUNK)
            hi_cols = slice(HALF + c * OUT_CHUNK, HALF + (c + 1) * OUT_CHUNK)
            y_lo = acc_scr[rows, lo_cols] + _dot(act, wdb[:, lo_cols]) + bd_ref[:, lo_cols]
            y_hi = acc_scr[rows, hi_cols] + _dot(act, wdb[:, hi_cols]) + bd_ref[:, hi_cols]
            words = _pack_halves(y_lo, y_hi)
            for jj in range(OUT_CHUNK // LANES):
                j_tile = c * (OUT_CHUNK // LANES) + jj
                tiles[pl.ds(j_tile, RT, stride=SUBLANES), :] = words[:, jj * LANES:(jj + 1) * LANES]

    first_tile = pl.ds(0, SUBLANES)

    @pl.when(f == 0)
    def _():
        @pl.when(w == 0)
        def _():
            def zero(s, carry):
                acc_scr[sub_rows(s), :] = jnp.zeros((RT, D_MODEL), F32)
                xu_scr[1, pl.ds(pl.multiple_of(s * tile_rows, tile_rows), tile_rows), :] = (
                    jnp.zeros((tile_rows, LANES), U32))
                return carry
            lax.fori_loop(0, N_SUB, zero, 0)
            pad_rows = pl.ds(TOP_K * N_TOK * SUBLANES, R_MAX * SUBLANES)
            fill = pltpu.make_async_copy(xu_scr.at[1], yslots_hbm.at[pad_rows, :], ssem)
            fill.start()
            fill.wait()
            for_all_rows(lambda row: gather_copy(tok_cur_ref, cur, row).start())

        @pl.when((w == 0) | prev_live)
        def _():
            row_in = pltpu.make_async_copy(h2p_hbm.at[first_tile, :], xu_scr.at[cur, first_tile, :], gsem.at[cur])
            for_all_rows(lambda row: row_in.wait())

        @pl.when(prev_live)
        def _():
            row_out = pltpu.make_async_copy(xu_scr.at[cur, first_tile, :], yslots_hbm.at[first_tile, :], ssem)
            for_all_rows(lambda row: row_out.wait())

    @pl.when(live & (f < NF - 1))
    def _():
        cast_weights()
        for j in range(N_SUB // GROUP):
            accumulate(j)
            for i in range(GROUP):
                prefetch_next(j * GROUP + i)

    @pl.when(live & (f == NF - 1))
    def _():
        cast_weights()
        stage_inputs(0, 1, 0)
        for s in range(N_SUB):
            finish(s)
            if s + 1 < N_SUB:
                stage_inputs(s + 1, 1, ((s + 1) % GROUP) * RT)
            for r in range(RT):
                scatter_copy(s * RT + r).start(priority=r % 2)
            prefetch_next(s)


def _experts(item_e, item_live, item_tok, item_dst, h2p, wg, bg, wu, bu, wd, bd):
    def w_in_map(w, f, e, lv):
        return (e[w], 0, jnp.where(lv[w] > 0, f, NF - 1))

    def w_down_map(w, f, e, lv):
        return (e[w], jnp.where(lv[w] > 0, f, NF - 1), 0)

    def b_down_map(w, f, e, lv):
        return (e[w], 0, 0)

    grid_spec = pltpu.PrefetchScalarGridSpec(
        num_scalar_prefetch=2,
        grid=(jnp.sum(item_live) + 1, NF),
        in_specs=[
            pl.BlockSpec((1, 1, R_MAX), lambda w, f, e, lv: (w, 0, 0), memory_space=pltpu.SMEM),
            pl.BlockSpec((1, 1, R_MAX), lambda w, f, e, lv: (jnp.minimum(w + 1, N_ITEMS - 1), 0, 0),
                         memory_space=pltpu.SMEM),
            pl.BlockSpec((1, 1, R_MAX), lambda w, f, e, lv: (w, 0, 0), memory_space=pltpu.SMEM),
            pl.BlockSpec(memory_space=pl.ANY),
            pl.BlockSpec((None, D_MODEL, FC), w_in_map),
            pl.BlockSpec((None, 1, FC), w_in_map),
            pl.BlockSpec((None, D_MODEL, FC), w_in_map),
            pl.BlockSpec((None, 1, FC), w_in_map),
            pl.BlockSpec((None, FC, D_MODEL), w_down_map),
            pl.BlockSpec((None, 1, D_MODEL), b_down_map),
        ],
        out_specs=pl.BlockSpec(memory_space=pl.ANY),
        scratch_shapes=[
            pltpu.VMEM((2, R_MAX * SUBLANES, LANES), U32),
            pltpu.VMEM((R_MAX, D_MODEL), F32),
            pltpu.VMEM((GROUP * RT, D_MODEL), BF16),
            pltpu.VMEM((D_MODEL, FC), BF16),
            pltpu.VMEM((D_MODEL, FC), BF16),
            pltpu.VMEM((FC, D_MODEL), BF16),
            pltpu.SemaphoreType.DMA((2,)),
            pltpu.SemaphoreType.DMA(()),
        ],
    )
    return pl.pallas_call(
        _expert_kernel,
        grid_spec=grid_spec,
        out_shape=jax.ShapeDtypeStruct((SLOT_ROWS * SUBLANES, LANES), U32),
        compiler_params=_cparams(("arbitrary", "arbitrary"), 60),
        name="experts",
    )(item_e, item_live, item_tok, item_tok, item_dst, h2p, wg, bg, wu, bu, wd, bd)


def _combine_kernel(meta_ref, x1_ref, y0_ref, y1_ref, y2_ref, y3_ref, out_ref):
    gates = [meta_ref[:, 2 * TOP_K + k:2 * TOP_K + k + 1] for k in range(TOP_K)]
    for j in range(SUBLANES):
        lo_cols = slice(j * LANES, (j + 1) * LANES)
        hi_cols = slice(HALF + j * LANES, HALF + (j + 1) * LANES)
        lo_acc = x1_ref[:, lo_cols]
        hi_acc = x1_ref[:, hi_cols]
        for k, y_ref in enumerate((y0_ref, y1_ref, y2_ref, y3_ref)):
            lo, hi = _unpack_halves(_load_row_tiles(y_ref, j, TM_CB))
            lo_acc = lo_acc + gates[k] * lo
            hi_acc = hi_acc + gates[k] * hi
        out_ref[:, lo_cols] = lo_acc
        out_ref[:, hi_cols] = hi_acc


def _combine(meta, x1, y_slots):
    tm = TM_CB
    nblk = N_TOK // tm
    slot_specs = [pl.BlockSpec((tm * SUBLANES, LANES), lambda i, k=k: (k * nblk + i, 0)) for k in range(TOP_K)]
    return pl.pallas_call(
        _combine_kernel,
        grid=(nblk,),
        in_specs=[
            pl.BlockSpec((tm, LANES), lambda i: (i, 0)),
            pl.BlockSpec((tm, D_MODEL), lambda i: (i, 0)),
        ] + slot_specs,
        out_specs=pl.BlockSpec((tm, D_MODEL), lambda i: (i, 0)),
        out_shape=jax.ShapeDtypeStruct((N_TOK, D_MODEL), F32),
        compiler_params=_cparams(("arbitrary",), 32),
        name="combine",
    )(meta, x1, y_slots, y_slots, y_slots, y_slots)


def _dft_tables():
    r = DFT_RADIX
    k = np.arange(r)
    e64 = np.exp(2j * np.pi * np.outer(k, k) / r)
    e4096 = np.exp(2j * np.pi * np.outer(k, k) / SEQ)
    s = np.arange(SEQ)
    f1 = e64[:, s % r]
    g = e64[:, s // r] * e4096[:, s % r] / math.sqrt(SEQ)
    c = np.arange(FOURIER_GROUP_DIM)
    ang = 2.0 * np.pi * np.outer(c, c) / FOURIER_GROUP_DIM
    scale = 1.0 / math.sqrt(FOURIER_GROUP_DIM)
    as32 = lambda a: jnp.asarray(a.astype(np.float32))
    return (as32(f1.real), as32(f1.imag), as32(g.real), as32(g.imag),
            jnp.asarray((np.cos(ang) * scale).astype(np.float32)).astype(BF16),
            jnp.asarray((np.sin(ang) * scale).astype(np.float32)).astype(BF16))


def _head_selectors():
    i = np.arange(LANES)[:, None]
    j = np.arange(LANES)[None, :]
    out = np.zeros((4, LANES, LANES), np.float32)
    for src in range(2):
        for dst in range(2):
            out[2 * src + dst] = (i - HEAD_DIM * src == j - HEAD_DIM * dst) & (j // HEAD_DIM == dst)
    return jnp.asarray(out).astype(BF16)


def kernel(x, norm1_g, w_in, b_branch_gate, q_norm_g, k_norm_g, attn_sink, w_fourier_out, w_attn_out, w_o,
           norm2_g, w_router, b_router, w_gate_e, b_gate_e, w_up_e, b_up_e, w_down_e, b_down_e):
    b, s, d = x.shape
    assert (b, s, d) == (BATCH, SEQ, D_MODEL) and norm1_g.shape[0] == 1
    x2 = x.reshape(N_TOK, D_MODEL)

    heads_per_tile = TN_IN // HEAD_DIM
    ones_bd = jnp.asarray((np.arange(TN_IN)[:, None] // HEAD_DIM == np.arange(TN_IN)[None, :] // HEAD_DIM)
                          .astype(np.float32)).astype(BF16)
    u_f, q, kv, gates = _in_proj(
        x2, norm1_g[0].reshape(1, D_MODEL), w_in[0].astype(BF16), b_branch_gate[0].reshape(1, 2 * D_MODEL),
        jnp.tile(q_norm_g[0], heads_per_tile).reshape(1, TN_IN),
        jnp.tile(k_norm_g[0], heads_per_tile).reshape(1, TN_IN), ones_bd)
    y_f = _fourier(u_f, *_dft_tables(), _mirror_rows())
    attn = _attention(attn_sink[0], q, kv, _head_selectors())

    wr = jnp.zeros((D_MODEL, LANES), BF16).at[:, :N_EXPERTS].set(w_router[0].astype(BF16))
    br = jnp.full((1, LANES), NEG_INF, F32).at[0, :N_EXPERTS].set(b_router[0])
    ltri = jnp.asarray(np.tril(np.ones((MG_SLAB, MG_SLAB), np.float32), -1)).astype(BF16)
    x1, h2p, meta, cnt = _merge(y_f, attn, gates, x2, w_fourier_out[0].astype(BF16), w_attn_out[0].astype(BF16),
                                w_o[0].astype(BF16), norm2_g[0].reshape(1, D_MODEL), wr, br, ltri)

    flat_int = lambda cols: cols.reshape(TOP_K * N_TOK).astype(jnp.int32)
    tables = _routing_tables(cnt[0, :N_EXPERTS].astype(jnp.int32), flat_int(meta[:, 0:TOP_K]),
                             flat_int(meta[:, TOP_K:2 * TOP_K]))
    y_slots = _experts(*tables, h2p,
                       w_gate_e[0], b_gate_e[0].reshape(N_EXPERTS, 1, D_EXPERT),
                       w_up_e[0], b_up_e[0].reshape(N_EXPERTS, 1, D_EXPERT),
                       w_down_e[0], b_down_e[0].reshape(N_EXPERTS, 1, D_MODEL))
    out = _combine(meta, x1, y_slots)
    return out.reshape(BATCH, SEQ, D_MODEL)


def _routing_tables(counts, top_idx, rank):
    padded = ((counts + RT - 1) // RT) * RT
    pend = jnp.cumsum(padded)
    pstart = pend - padded
    n_assign = TOP_K * N_TOK
    dest = pstart[top_idx.reshape(n_assign)] + rank.reshape(n_assign)
    row_assign = jnp.full((ROWS,), -1, jnp.int32).at[dest].set(
        jnp.arange(n_assign, dtype=jnp.int32), unique_indices=True)

    items_per_e = (padded + R_MAX - 1) // R_MAX
    it_end = jnp.cumsum(items_per_e)
    it_start = it_end - items_per_e
    total_items = it_end[-1]
    wi = jnp.arange(N_ITEMS, dtype=jnp.int32)
    e_w = jnp.minimum(jnp.searchsorted(it_end, wi, side="right"), N_EXPERTS - 1).astype(jnp.int32)
    j_w = wi - it_start[e_w]
    live = wi < total_items
    rows_w = jnp.clip(padded[e_w] - j_w * R_MAX, 0, R_MAX)
    e_last = e_w[jnp.maximum(total_items - 1, 0)]
    item_e = jnp.where(live, e_w, e_last).astype(jnp.int32)
    item_row0 = jnp.where(live, pstart[e_w] + j_w * R_MAX, 0).astype(jnp.int32)
    item_nsub = jnp.where(live, rows_w // RT, 0).astype(jnp.int32)
    item_live = live.astype(jnp.int32)
    r = jnp.arange(R_MAX, dtype=jnp.int32)
    src = jnp.minimum(item_row0[:, None] + r[None, :], ROWS - 1)
    item_rows = jnp.where(r[None, :] < (item_nsub * RT)[:, None], row_assign[src], -1)
    item_tok = (jnp.maximum(item_rows, 0) // TOP_K).reshape(N_ITEMS, 1, R_MAX)
    item_dst = jnp.where(item_rows >= 0, (item_rows % TOP_K) * N_TOK + item_rows // TOP_K,
                         TOP_K * N_TOK + r[None, :]).reshape(N_ITEMS, 1, R_MAX)
    return item_e, item_live, item_tok, item_dst
```

```python
import functools
import math

import jax
import jax.numpy as jnp
import numpy as np
from jax import lax
from jax.experimental import pallas as pl
from jax.experimental.pallas import tpu as pltpu

D_MODEL = 2048
BATCH = 4
SEQ = 4096
N_TOK = BATCH * SEQ
N_Q_HEADS = 16
N_KV_HEADS = 4
HEAD_DIM = 64
Q_WIDTH = N_Q_HEADS * HEAD_DIM
KV_WIDTH = N_KV_HEADS * HEAD_DIM
WINDOW = 128
N_FOURIER_GROUPS = 4
FOURIER_GROUP_DIM = 256
FOURIER_WIDTH = N_FOURIER_GROUPS * FOURIER_GROUP_DIM
IN_WIDTH = FOURIER_WIDTH + Q_WIDTH + 2 * KV_WIDTH + 2 * D_MODEL
N_EXPERTS = 32
TOP_K = 4
D_EXPERT = D_MODEL
SWIGLU_LIMIT = 7.0
SWIGLU_ALPHA = 1.702
RMS_EPS = 1e-6
NEG_INF = -1e30

F32 = jnp.float32
BF16 = jnp.bfloat16
U32 = jnp.uint32

V7X_VMEM_BYTES = 64 * 1024 * 1024
LANES = 128
SUBLANES = 8
MIB = 1024 * 1024

TM_IN = 512
TN_IN = 512
NORM_ROWS = 64
TT_F = 512
F_DIRECT = SEQ // TT_F // 2 + 1
DFT_RADIX = 64
TQ = 512
QB = 128
TM_MG = 512
MG_SLAB = 256
RT = 256
R_MAX = 9 * RT
FC = 256
NF = D_EXPERT // FC
ROWS = TOP_K * N_TOK + N_EXPERTS * RT
N_ITEMS = (TOP_K * N_TOK + N_EXPERTS * (RT - 1) + N_EXPERTS * (R_MAX - RT)) // R_MAX + 1
TM_CB = 256
HALF = D_MODEL // 2
N_SUB = R_MAX // RT
GROUP = 3
OUT_CHUNK = 256
GATHER_PER_STEP = RT // NF
SLOT_ROWS = TOP_K * N_TOK + R_MAX


def _cparams(sem, vmem_mib):
    return pltpu.CompilerParams(dimension_semantics=sem, vmem_limit_bytes=vmem_mib * MIB)


def _dot(a, b):
    return jnp.dot(a, b, preferred_element_type=F32)


def _pack_halves(lo, hi):
    lo_bits = pltpu.bitcast(lo.astype(BF16).astype(F32), U32)
    hi_bits = pltpu.bitcast(hi.astype(BF16).astype(F32), U32)
    return (hi_bits & jnp.uint32(0xFFFF0000)) | (lo_bits >> jnp.uint32(16))


def _unpack_halves(words):
    lo = pltpu.bitcast(words << jnp.uint32(16), F32)
    hi = pltpu.bitcast(words & jnp.uint32(0xFFFF0000), F32)
    return lo, hi


def _store_row_tiles(tiles_ref, words):
    n_rows = words.shape[0]
    for j in range(SUBLANES):
        tiles_ref[pl.ds(j, n_rows, stride=SUBLANES), :] = words[:, j * LANES:(j + 1) * LANES]


def _load_row_tiles(tiles_ref, j, n_rows):
    return tiles_ref[pl.ds(j, n_rows, stride=SUBLANES), :]


def _in_proj_kernel(x_ref, g1_ref, w_ref, b_ref, qg_ref, kg_ref, ones_ref,
                    u_ref, q_ref, kv_ref, gate_ref, h_scr):
    def body(c, carry):
        rows = pl.ds(pl.multiple_of(c * NORM_ROWS, NORM_ROWS), NORM_ROWS)
        x = x_ref[rows, :]
        ms = jnp.mean(x * x, axis=-1, keepdims=True)
        h_scr[rows, :] = (x * lax.rsqrt(ms + RMS_EPS) * g1_ref[...]).astype(BF16)
        return carry
    lax.fori_loop(0, TM_IN // NORM_ROWS, body, 0)

    def head_norm(a, gain):
        ssq = _dot((a * a).astype(BF16), ones_ref[...])
        return a * lax.rsqrt(ssq * (1.0 / HEAD_DIM) + RMS_EPS) * gain

    h = h_scr[...]
    for c in range(IN_WIDTH // TN_IN):
        col0 = c * TN_IN
        acc = _dot(h, w_ref[:, col0:col0 + TN_IN])
        if col0 < FOURIER_WIDTH:
            u_ref[:, col0:col0 + TN_IN] = acc.astype(BF16)
        elif col0 < FOURIER_WIDTH + Q_WIDTH:
            o = col0 - FOURIER_WIDTH
            q_ref[:, o:o + TN_IN] = (head_norm(acc, qg_ref[...]) * (HEAD_DIM ** -0.5)).astype(BF16)
        elif col0 < FOURIER_WIDTH + Q_WIDTH + 2 * KV_WIDTH:
            lane = lax.broadcasted_iota(jnp.int32, acc.shape, 1)
            kv_ref[...] = jnp.where(lane < KV_WIDTH, head_norm(acc, kg_ref[...]), acc).astype(BF16)
        else:
            o = col0 - (FOURIER_WIDTH + Q_WIDTH + 2 * KV_WIDTH)
            z = acc + b_ref[:, o:o + TN_IN]
            gate_ref[:, o:o + TN_IN] = (1.0 / (1.0 + jnp.exp(-z))).astype(BF16)


def _in_proj(x2, g1, w_bf, bias, qg_t, kg_t, ones_bd):
    const = lambda i: (0, 0)
    row = lambda i: (i, 0)
    return pl.pallas_call(
        _in_proj_kernel,
        grid=(N_TOK // TM_IN,),
        in_specs=[
            pl.BlockSpec((TM_IN, D_MODEL), row),
            pl.BlockSpec((1, D_MODEL), const),
            pl.BlockSpec((D_MODEL, IN_WIDTH), const, pipeline_mode=pl.Buffered(1)),
            pl.BlockSpec((1, 2 * D_MODEL), const),
            pl.BlockSpec((1, TN_IN), const),
            pl.BlockSpec((1, TN_IN), const),
            pl.BlockSpec((TN_IN, TN_IN), const),
        ],
        out_specs=[
            pl.BlockSpec((TM_IN, FOURIER_WIDTH), row),
            pl.BlockSpec((TM_IN, Q_WIDTH), row),
            pl.BlockSpec((TM_IN, 2 * KV_WIDTH), row),
            pl.BlockSpec((TM_IN, 2 * D_MODEL), row),
        ],
        out_shape=[
            jax.ShapeDtypeStruct((N_TOK, FOURIER_WIDTH), BF16),
            jax.ShapeDtypeStruct((N_TOK, Q_WIDTH), BF16),
            jax.ShapeDtypeStruct((N_TOK, 2 * KV_WIDTH), BF16),
            jax.ShapeDtypeStruct((N_TOK, 2 * D_MODEL), BF16),
        ],
        scratch_shapes=[pltpu.VMEM((TM_IN, D_MODEL), BF16)],
        compiler_params=_cparams(("arbitrary",), 60),
        name="in_proj",
    )(x2, g1, w_bf, bias, qg_t, kg_t, ones_bd)


def _fourier_kernel(u_ref, f1r_ref, f1i_ref, gr_ref, gi_ref, cc_ref, sc_ref, rev_ref, y_ref, cs_scr, ss_scr, pq_scr):
    t = pl.program_id(0)
    b = pl.program_id(1)
    nt = SEQ // TT_F

    @pl.when(t < F_DIRECT)
    def _():
        @pl.when(b == 0)
        def _():
            gr = gr_ref[...]
            gi = gi_ref[...]
            for a in range(TT_F // DFT_RADIX):
                f1r = f1r_ref[a:a + 1, :]
                f1i = f1i_ref[a:a + 1, :]
                rows = slice(a * DFT_RADIX, (a + 1) * DFT_RADIX)
                cs_scr[rows, :] = (f1r * gr - f1i * gi).astype(BF16)
                ss_scr[rows, :] = (f1r * gi + f1i * gr).astype(BF16)

        u = u_ref[...]
        a_seq = _dot(cs_scr[...], u)
        b_seq = _dot(ss_scr[...], u)
        for g in range(N_FOURIER_GROUPS):
            cols = slice(g * FOURIER_GROUP_DIM, (g + 1) * FOURIER_GROUP_DIM)
            p = _dot(a_seq[:, cols].astype(BF16), cc_ref[...])
            q = _dot(b_seq[:, cols].astype(BF16), sc_ref[...])
            y_ref[:, cols] = (p - q).astype(BF16)

            @pl.when(t < nt - F_DIRECT + 1)
            def _():
                pq_scr[b, t, :, cols] = (p + q).astype(BF16)

    @pl.when(t >= F_DIRECT)
    def _():
        src = jnp.concatenate([pq_scr[b, nt - 1 - t], pq_scr[b, nt - t]], axis=0)
        y_ref[...] = _dot(rev_ref[...], src).astype(BF16)


def _fourier(u, f1r, f1i, gr, gi, cc, sc, rev):
    nt = SEQ // TT_F
    f1_rows = TT_F // DFT_RADIX
    const = lambda t, b: (0, 0)
    direct = lambda t: jnp.minimum(t, F_DIRECT - 1)
    return pl.pallas_call(
        _fourier_kernel,
        grid=(nt, BATCH),
        in_specs=[
            pl.BlockSpec((SEQ, FOURIER_WIDTH), lambda t, b: (jnp.where(t < F_DIRECT, b, BATCH - 1), 0)),
            pl.BlockSpec((f1_rows, SEQ), lambda t, b: (direct(t), 0)),
            pl.BlockSpec((f1_rows, SEQ), lambda t, b: (direct(t), 0)),
            pl.BlockSpec((DFT_RADIX, SEQ), const),
            pl.BlockSpec((DFT_RADIX, SEQ), const),
            pl.BlockSpec((FOURIER_GROUP_DIM, FOURIER_GROUP_DIM), const),
            pl.BlockSpec((FOURIER_GROUP_DIM, FOURIER_GROUP_DIM), const),
            pl.BlockSpec((TT_F, 2 * TT_F), const),
        ],
        out_specs=pl.BlockSpec((TT_F, FOURIER_WIDTH), lambda t, b: (b * nt + t, 0)),
        out_shape=jax.ShapeDtypeStruct((N_TOK, FOURIER_WIDTH), BF16),
        scratch_shapes=[pltpu.VMEM((TT_F, SEQ), BF16), pltpu.VMEM((TT_F, SEQ), BF16),
                        pltpu.VMEM((BATCH, nt - F_DIRECT + 1, TT_F, FOURIER_WIDTH), BF16)],
        compiler_params=_cparams(("arbitrary", "arbitrary"), 60),
        name="fourier",
    )(u, f1r, f1i, gr, gi, cc, sc, rev)


def _mirror_rows():
    rev = np.zeros((TT_F, 2 * TT_F), np.float32)
    rev[0, TT_F] = 1.0
    rev[np.arange(1, TT_F), TT_F - np.arange(1, TT_F)] = 1.0
    return jnp.asarray(rev).astype(BF16)


def _alibi_slope(h):
    return float(2.0 ** (-8.0 * (h + 1) / N_Q_HEADS))


def _attention_kernel(sink_ref, q_ref, prev_ref, cur_ref, next_ref, sel_ref, o_ref, kz_scr, vz_scr):
    t = pl.program_id(1)
    band = jnp.concatenate([prev_ref[...], cur_ref[...], next_ref[...]], axis=0)
    for g in range(N_KV_HEADS):
        c, half = divmod(g, 2)
        kcol = band[:, c * LANES:(c + 1) * LANES]
        vcol = band[:, KV_WIDTH + c * LANES:KV_WIDTH + (c + 1) * LANES]
        for d in range(2):
            sel = sel_ref[half * 2 + d]
            kz_scr[2 * g + d] = jnp.transpose(_dot(kcol, sel)).astype(BF16)
            vz_scr[2 * g + d] = _dot(vcol, sel).astype(BF16)

    row = lax.broadcasted_iota(jnp.int32, (2 * QB, 3 * QB), 0)
    col = lax.broadcasted_iota(jnp.int32, (2 * QB, 3 * QB), 1)
    absrel = jnp.abs((row & (QB - 1)) - col + QB)
    absrel_f = absrel.astype(F32)
    top = lax.broadcasted_iota(jnp.int32, (2 * QB, 1), 0) < QB
    for i in range(TQ // QB):
        q_rows = slice(i * QB, (i + 1) * QB)
        k_rows = slice(i * QB, (i + 3) * QB)
        kpos = t * TQ + (i - 1) * QB + col
        mask = (absrel <= WINDOW) & (kpos >= 0) & (kpos < SEQ)
        for g in range(N_KV_HEADS):
            cols = [slice((2 * g + p) * LANES, (2 * g + p + 1) * LANES) for p in range(2)]
            qq = jnp.concatenate([q_ref[q_rows, cols[0]], q_ref[q_rows, cols[1]]], axis=0)
            out = jnp.zeros((2 * QB, LANES), F32)
            for d in range(2):
                h_top, h_bot = 4 * g + d, 4 * g + 2 + d
                slope = jnp.where(top, _alibi_slope(h_top), _alibi_slope(h_bot))
                sink = jnp.where(top, sink_ref[h_top], sink_ref[h_bot])
                s = _dot(qq, kz_scr[2 * g + d, :, k_rows])
                s = jnp.where(mask, s - slope * absrel_f, NEG_INF)
                m = jnp.maximum(jnp.max(s, axis=-1, keepdims=True), sink)
                pr = jnp.exp(s - m)
                den = jnp.sum(pr, axis=-1, keepdims=True) + jnp.exp(sink - m)
                out = out + _dot(pr.astype(BF16), vz_scr[2 * g + d, k_rows, :]) / den
            o_ref[q_rows, cols[0]] = out[:QB].astype(BF16)
            o_ref[q_rows, cols[1]] = out[QB:].astype(BF16)


def _attention(sink, q, kv, sel):
    nt = SEQ // TQ
    nb = SEQ // QB
    r = TQ // QB
    grid_spec = pltpu.PrefetchScalarGridSpec(
        num_scalar_prefetch=1,
        grid=(BATCH, nt),
        in_specs=[
            pl.BlockSpec((TQ, Q_WIDTH), lambda b, t, s: (b * nt + t, 0)),
            pl.BlockSpec((QB, 2 * KV_WIDTH), lambda b, t, s: (b * nb + jnp.maximum(t * r - 1, 0), 0)),
            pl.BlockSpec((TQ, 2 * KV_WIDTH), lambda b, t, s: (b * nt + t, 0)),
            pl.BlockSpec((QB, 2 * KV_WIDTH), lambda b, t, s: (b * nb + jnp.minimum(t * r + r, nb - 1), 0)),
            pl.BlockSpec((4, LANES, LANES), lambda b, t, s: (0, 0, 0)),
        ],
        out_specs=pl.BlockSpec((TQ, Q_WIDTH), lambda b, t, s: (b * nt + t, 0)),
        scratch_shapes=[pltpu.VMEM((2 * N_KV_HEADS, LANES, TQ + 2 * QB), BF16),
                        pltpu.VMEM((2 * N_KV_HEADS, TQ + 2 * QB, LANES), BF16)],
    )
    return pl.pallas_call(
        _attention_kernel,
        grid_spec=grid_spec,
        out_shape=jax.ShapeDtypeStruct((N_TOK, Q_WIDTH), BF16),
        compiler_params=_cparams(("arbitrary", "arbitrary"), 32),
        name="attention",
    )(sink, q, kv, kv, kv, sel)


def _merge_kernel(yf_ref, at_ref, gf_ref, ga_ref, x_ref, wfo_ref, wao_ref, wo_ref, g2_ref, wr_ref, br_ref,
                  ltri_ref, x1_ref, h2p_ref, meta_ref, cnt_ref, base_scr):
    i = pl.program_id(0)

    @pl.when(i == 0)
    def _():
        base_scr[...] = jnp.zeros_like(base_scr)

    base = base_scr[...]
    for s in range(TM_MG // MG_SLAB):
        rows = slice(s * MG_SLAB, (s + 1) * MG_SLAB)
        yf = _dot(yf_ref[rows, :], wfo_ref[...])
        ya = _dot(at_ref[rows, :], wao_ref[...])
        merged = gf_ref[rows, :].astype(F32) * yf + ga_ref[rows, :].astype(F32) * ya
        x1 = x_ref[rows, :] + _dot(merged.astype(BF16), wo_ref[...])
        x1_ref[rows, :] = x1
        ms = jnp.mean(x1 * x1, axis=-1, keepdims=True)
        h2 = x1 * lax.rsqrt(ms + RMS_EPS) * g2_ref[...]
        _store_row_tiles(h2p_ref.at[pl.ds(s * MG_SLAB * SUBLANES, MG_SLAB * SUBLANES), :],
                         _pack_halves(h2[:, :HALF], h2[:, HALF:]))
        logits = _dot(h2.astype(BF16), wr_ref[...]) + br_ref[...]

        lane = lax.broadcasted_iota(jnp.int32, logits.shape, 1)
        lane_f = lane.astype(F32)
        vals = logits
        top_v, top_i, onehots = [], [], []
        for _ in range(TOP_K):
            m = jnp.max(vals, axis=-1, keepdims=True)
            idx = jnp.min(jnp.where(vals == m, lane_f, float(LANES)), axis=-1, keepdims=True)
            oh = lane_f == idx
            top_v.append(m)
            top_i.append(idx)
            onehots.append(oh)
            vals = jnp.where(oh, -jnp.inf, vals)
        exps = [jnp.exp(v - top_v[0]) for v in top_v]
        den = exps[0] + exps[1] + exps[2] + exps[3]
        gates = [e / den for e in exps]

        cnt = sum(oh.astype(F32) for oh in onehots)
        prefix = _dot(ltri_ref[...], cnt.astype(BF16))
        tot = base + prefix
        ranks = [jnp.sum(jnp.where(oh, tot, 0.0), axis=-1, keepdims=True) for oh in onehots]
        base = base + jnp.sum(cnt, axis=0, keepdims=True)

        meta = jnp.zeros(logits.shape, F32)
        for k in range(TOP_K):
            meta = jnp.where(lane == k, top_i[k], meta)
            meta = jnp.where(lane == TOP_K + k, ranks[k], meta)
            meta = jnp.where(lane == 2 * TOP_K + k, gates[k], meta)
        meta_ref[rows, :] = meta
    base_scr[...] = base
    cnt_ref[...] = jnp.broadcast_to(base, cnt_ref.shape)


def _merge(yf, attn, gates, x2, wfo, wao, wo, g2, wr, br, ltri):
    tm = TM_MG
    const = lambda i: (0, 0)
    resident = functools.partial(pl.BlockSpec, index_map=const, pipeline_mode=pl.Buffered(1))
    return pl.pallas_call(
        _merge_kernel,
        grid=(N_TOK // tm,),
        in_specs=[
            pl.BlockSpec((tm, FOURIER_WIDTH), lambda i: (i, 0)),
            pl.BlockSpec((tm, Q_WIDTH), lambda i: (i, 0)),
            pl.BlockSpec((tm, D_MODEL), lambda i: (i, 0)),
            pl.BlockSpec((tm, D_MODEL), lambda i: (i, 1)),
            pl.BlockSpec((tm, D_MODEL), lambda i: (i, 0)),
            resident((FOURIER_WIDTH, D_MODEL)),
            resident((Q_WIDTH, D_MODEL)),
            resident((D_MODEL, D_MODEL)),
            pl.BlockSpec((1, D_MODEL), const),
            pl.BlockSpec((D_MODEL, LANES), const),
            pl.BlockSpec((1, LANES), const),
            pl.BlockSpec((MG_SLAB, MG_SLAB), const),
        ],
        out_specs=[
            pl.BlockSpec((tm, D_MODEL), lambda i: (i, 0)),
            pl.BlockSpec((tm * SUBLANES, LANES), lambda i: (i, 0)),
            pl.BlockSpec((tm, LANES), lambda i: (i, 0)),
            pl.BlockSpec((8, LANES), const),
        ],
        out_shape=[
            jax.ShapeDtypeStruct((N_TOK, D_MODEL), F32),
            jax.ShapeDtypeStruct((N_TOK * SUBLANES, LANES), U32),
            jax.ShapeDtypeStruct((N_TOK, LANES), F32),
            jax.ShapeDtypeStruct((8, LANES), F32),
        ],
        scratch_shapes=[pltpu.VMEM((1, LANES), F32)],
        compiler_params=_cparams(("arbitrary",), 56),
        name="merge",
    )(yf, attn, gates, gates, x2, wfo, wao, wo, g2, wr, br, ltri)


def _expert_kernel(e_ref, live_ref, tok_cur_ref, tok_nxt_ref, dst_cur_ref,
                   h2p_hbm, wg_ref, bg_ref, wu_ref, bu_ref, wd_ref, bd_ref,
                   yslots_hbm, xu_scr, acc_scr, xs_scr, wgb, wub, wdb, gsem, ssem):
    del e_ref
    w = pl.program_id(0)
    f = pl.program_id(1)
    live = live_ref[w] > 0
    prev_live = (w > 0) & (live_ref[jnp.maximum(w - 1, 0)] > 0)
    cur = w & 1
    nxt = 1 - cur
    tile_rows = RT * SUBLANES

    def sub_rows(s):
        return pl.ds(pl.multiple_of(s * RT, RT), RT)

    def row_tile(row):
        return pl.ds(pl.multiple_of(row * SUBLANES, SUBLANES), SUBLANES)

    def gather_copy(tok_ref, slot, row):
        tok = tok_ref[0, 0, row]
        return pltpu.make_async_copy(h2p_hbm.at[row_tile(tok), :], xu_scr.at[slot, row_tile(row), :], gsem.at[slot])

    def scatter_copy(row):
        dst = dst_cur_ref[0, 0, row]
        return pltpu.make_async_copy(xu_scr.at[cur, row_tile(row), :], yslots_hbm.at[row_tile(dst), :], ssem)

    def for_all_rows(fn):
        def body(row, carry):
            fn(row)
            return carry
        lax.fori_loop(0, R_MAX, body, 0, unroll=8)

    def prefetch_next(s):
        base = s * RT + f * GATHER_PER_STEP
        for g in range(GATHER_PER_STEP):
            gather_copy(tok_nxt_ref, nxt, base + g).start()

    def cast_weights():
        wgb[...] = wg_ref[...].astype(BF16)
        wub[...] = wu_ref[...].astype(BF16)
        wdb[...] = wd_ref[...].astype(BF16)

    def stage_inputs(first_sub, n_sub, stage_row0):
        n_rows = n_sub * RT
        tiles = xu_scr.at[cur, pl.ds(pl.multiple_of(first_sub * tile_rows, tile_rows), n_sub * tile_rows), :]
        stage = xs_scr.at[pl.ds(stage_row0, n_rows), :]
        for j in range(SUBLANES):
            lo, hi = _unpack_halves(_load_row_tiles(tiles, j, n_rows))
            stage[:, j * LANES:(j + 1) * LANES] = lo.astype(BF16)
            stage[:, HALF + j * LANES:HALF + (j + 1) * LANES] = hi.astype(BF16)

    def activations(n_sub, stage_row0):
        x = xs_scr[pl.ds(stage_row0, n_sub * RT), :]
        g = _dot(x, wgb[...]) + bg_ref[...]
        u = _dot(x, wub[...]) + bu_ref[...]
        g = jnp.minimum(g, SWIGLU_LIMIT)
        u = jnp.clip(u, -SWIGLU_LIMIT, SWIGLU_LIMIT)
        act = (u + 1.0) * (g * (1.0 / (1.0 + jnp.exp(-SWIGLU_ALPHA * g))))
        return act.astype(BF16)

    def accumulate(j):
        n_rows = GROUP * RT
        rows = pl.ds(j * n_rows, n_rows)
        stage_inputs(j * GROUP, GROUP, 0)
        c = _dot(activations(GROUP, 0), wdb[...])
        acc_scr[rows, :] = jnp.where(f == 0, c, acc_scr[rows, :] + c)

    def finish(s):
        act = activations(1, (s % GROUP) * RT)
        rows = sub_rows(s)
        tiles = xu_scr.at[cur, pl.ds(s * tile_rows, tile_rows), :]
        for c in range(HALF // OUT_CHUNK):
            lo_cols = slice(c * OUT_CHUNK, (c + 1) * OUT_CHUNK)
            hi_cols = slice(HALF + c * OUT_CHUNK, HALF + (c + 1) * OUT_CHUNK)
            y_lo = acc_scr[rows, lo_cols] + _dot(act, wdb[:, lo_cols]) + bd_ref[:, lo_cols]
            y_hi = acc_scr[rows, hi_cols] + _dot(act, wdb[:, hi_cols]) + bd_ref[:, hi_cols]
            words = _pack_halves(y_lo, y_hi)
            for jj in range(OUT_CHUNK // LANES):
                j_tile = c * (OUT_CHUNK // LANES) + jj
                tiles[pl.ds(j_tile, RT, stride=SUBLANES), :] = words[:, jj * LANES:(jj + 1) * LANES]

    first_tile = pl.ds(0, SUBLANES)

    @pl.when(f == 0)
    def _():
        @pl.when(w == 0)
        def _():
            def zero(s, carry):
                acc_scr[sub_rows(s), :] = jnp.zeros((RT, D_MODEL), F32)
                xu_scr[1, pl.ds(pl.multiple_of(s * tile_rows, tile_rows), tile_rows), :] = (
                    jnp.zeros((tile_rows, LANES), U32))
                return carry
            lax.fori_loop(0, N_SUB, zero, 0)
            pad_rows = pl.ds(TOP_K * N_TOK * SUBLANES, R_MAX * SUBLANES)
            fill = pltpu.make_async_copy(xu_scr.at[1], yslots_hbm.at[pad_rows, :], ssem)
            fill.start()
            fill.wait()
            for_all_rows(lambda row: gather_copy(tok_cur_ref, cur, row).start())

        @pl.when((w == 0) | prev_live)
        def _():
            row_in = pltpu.make_async_copy(h2p_hbm.at[first_tile, :], xu_scr.at[cur, first_tile, :], gsem.at[cur])
            for_all_rows(lambda row: row_in.wait())

        @pl.when(prev_live)
        def _():
            row_out = pltpu.make_async_copy(xu_scr.at[cur, first_tile, :], yslots_hbm.at[first_tile, :], ssem)
            for_all_rows(lambda row: row_out.wait())

    @pl.when(live & (f < NF - 1))
    def _():
        cast_weights()
        for j in range(N_SUB // GROUP):
            accumulate(j)
            for i in range(GROUP):
                prefetch_next(j * GROUP + i)

    @pl.when(live & (f == NF - 1))
    def _():
        cast_weights()
        stage_inputs(0, 1, 0)
        for s in range(N_SUB):
            finish(s)
            if s + 1 < N_SUB:
                stage_inputs(s + 1, 1, ((s + 1) % GROUP) * RT)
            for r in range(RT):
                scatter_copy(s * RT + r).start()
            prefetch_next(s)


def _experts(item_e, item_live, item_tok, item_dst, h2p, wg, bg, wu, bu, wd, bd):
    def w_in_map(w, f, e, lv):
        return (e[w], 0, jnp.where(lv[w] > 0, f, NF - 1))

    def w_down_map(w, f, e, lv):
        return (e[w], jnp.where(lv[w] > 0, f, NF - 1), 0)

    def b_down_map(w, f, e, lv):
        return (e[w], 0, 0)

    grid_spec = pltpu.PrefetchScalarGridSpec(
        num_scalar_prefetch=2,
        grid=(jnp.sum(item_live) + 1, NF),
        in_specs=[
            pl.BlockSpec((1, 1, R_MAX), lambda w, f, e, lv: (w, 0, 0), memory_space=pltpu.SMEM),
            pl.BlockSpec((1, 1, R_MAX), lambda w, f, e, lv: (jnp.minimum(w + 1, N_ITEMS - 1), 0, 0),
                         memory_space=pltpu.SMEM),
            pl.BlockSpec((1, 1, R_MAX), lambda w, f, e, lv: (w, 0, 0), memory_space=pltpu.SMEM),
            pl.BlockSpec(memory_space=pl.ANY),
            pl.BlockSpec((None, D_MODEL, FC), w_in_map),
            pl.BlockSpec((None, 1, FC), w_in_map),
            pl.BlockSpec((None, D_MODEL, FC), w_in_map),
            pl.BlockSpec((None, 1, FC), w_in_map),
            pl.BlockSpec((None, FC, D_MODEL), w_down_map),
            pl.BlockSpec((None, 1, D_MODEL), b_down_map),
        ],
        out_specs=pl.BlockSpec(memory_space=pl.ANY),
        scratch_shapes=[
            pltpu.VMEM((2, R_MAX * SUBLANES, LANES), U32),
            pltpu.VMEM((R_MAX, D_MODEL), F32),
            pltpu.VMEM((GROUP * RT, D_MODEL), BF16),
            pltpu.VMEM((D_MODEL, FC), BF16),
            pltpu.VMEM((D_MODEL, FC), BF16),
            pltpu.VMEM((FC, D_MODEL), BF16),
            pltpu.SemaphoreType.DMA((2,)),
            pltpu.SemaphoreType.DMA(()),
        ],
    )
    return pl.pallas_call(
        _expert_kernel,
        grid_spec=grid_spec,
        out_shape=jax.ShapeDtypeStruct((SLOT_ROWS * SUBLANES, LANES), U32),
        compiler_params=_cparams(("arbitrary", "arbitrary"), 60),
        name="experts",
    )(item_e, item_live, item_tok, item_tok, item_dst, h2p, wg, bg, wu, bu, wd, bd)


def _combine_kernel(meta_ref, x1_ref, y0_ref, y1_ref, y2_ref, y3_ref, out_ref):
    gates = [meta_ref[:, 2 * TOP_K + k:2 * TOP_K + k + 1] for k in range(TOP_K)]
    for j in range(SUBLANES):
        lo_cols = slice(j * LANES, (j + 1) * LANES)
        hi_cols = slice(HALF + j * LANES, HALF + (j + 1) * LANES)
        lo_acc = x1_ref[:, lo_cols]
        hi_acc = x1_ref[:, hi_cols]
        for k, y_ref in enumerate((y0_ref, y1_ref, y2_ref, y3_ref)):
            lo, hi = _unpack_halves(_load_row_tiles(y_ref, j, TM_CB))
            lo_acc = lo_acc + gates[k] * lo
            hi_acc = hi_acc + gates[k] * hi
        out_ref[:, lo_cols] = lo_acc
        out_ref[:, hi_cols] = hi_acc


def _combine(meta, x1, y_slots):
    tm = TM_CB
    nblk = N_TOK // tm
    slot_specs = [pl.BlockSpec((tm * SUBLANES, LANES), lambda i, k=k: (k * nblk + i, 0)) for k in range(TOP_K)]
    return pl.pallas_call(
        _combine_kernel,
        grid=(nblk,),
        in_specs=[
            pl.BlockSpec((tm, LANES), lambda i: (i, 0)),
            pl.BlockSpec((tm, D_MODEL), lambda i: (i, 0)),
        ] + slot_specs,
        out_specs=pl.BlockSpec((tm, D_MODEL), lambda i: (i, 0)),
        out_shape=jax.ShapeDtypeStruct((N_TOK, D_MODEL), F32),
        compiler_params=_cparams(("arbitrary",), 32),
        name="combine",
    )(meta, x1, y_slots, y_slots, y_slots, y_slots)


def _dft_tables():
    r = DFT_RADIX
    k = np.arange(r)
    e64 = np.exp(2j * np.pi * np.outer(k, k) / r)
    e4096 = np.exp(2j * np.pi * np.outer(k, k) / SEQ)
    s = np.arange(SEQ)
    f1 = e64[:, s % r]
    g = e64[:, s // r] * e4096[:, s % r] / math.sqrt(SEQ)
    c = np.arange(FOURIER_GROUP_DIM)
    ang = 2.0 * np.pi * np.outer(c, c) / FOURIER_GROUP_DIM
    scale = 1.0 / math.sqrt(FOURIER_GROUP_DIM)
    as32 = lambda a: jnp.asarray(a.astype(np.float32))
    return (as32(f1.real), as32(f1.imag), as32(g.real), as32(g.imag),
            jnp.asarray((np.cos(ang) * scale).astype(np.float32)).astype(BF16),
            jnp.asarray((np.sin(ang) * scale).astype(np.float32)).astype(BF16))


def _head_selectors():
    i = np.arange(LANES)[:, None]
    j = np.arange(LANES)[None, :]
    out = np.zeros((4, LANES, LANES), np.float32)
    for src in range(2):
        for dst in range(2):
            out[2 * src + dst] = (i - HEAD_DIM * src == j - HEAD_DIM * dst) & (j // HEAD_DIM == dst)
    return jnp.asarray(out).astype(BF16)


def kernel(x, norm1_g, w_in, b_branch_gate, q_norm_g, k_norm_g, attn_sink, w_fourier_out, w_attn_out, w_o,
           norm2_g, w_router, b_router, w_gate_e, b_gate_e, w_up_e, b_up_e, w_down_e, b_down_e):
    b, s, d = x.shape
    assert (b, s, d) == (BATCH, SEQ, D_MODEL) and norm1_g.shape[0] == 1
    x2 = x.reshape(N_TOK, D_MODEL)

    heads_per_tile = TN_IN // HEAD_DIM
    ones_bd = jnp.asarray((np.arange(TN_IN)[:, None] // HEAD_DIM == np.arange(TN_IN)[None, :] // HEAD_DIM)
                          .astype(np.float32)).astype(BF16)
    u_f, q, kv, gates = _in_proj(
        x2, norm1_g[0].reshape(1, D_MODEL), w_in[0].astype(BF16), b_branch_gate[0].reshape(1, 2 * D_MODEL),
        jnp.tile(q_norm_g[0], heads_per_tile).reshape(1, TN_IN),
        jnp.tile(k_norm_g[0], heads_per_tile).reshape(1, TN_IN), ones_bd)
    y_f = _fourier(u_f, *_dft_tables(), _mirror_rows())
    attn = _attention(attn_sink[0], q, kv, _head_selectors())

    wr = jnp.zeros((D_MODEL, LANES), BF16).at[:, :N_EXPERTS].set(w_router[0].astype(BF16))
    br = jnp.full((1, LANES), NEG_INF, F32).at[0, :N_EXPERTS].set(b_router[0])
    ltri = jnp.asarray(np.tril(np.ones((MG_SLAB, MG_SLAB), np.float32), -1)).astype(BF16)
    x1, h2p, meta, cnt = _merge(y_f, attn, gates, x2, w_fourier_out[0].astype(BF16), w_attn_out[0].astype(BF16),
                                w_o[0].astype(BF16), norm2_g[0].reshape(1, D_MODEL), wr, br, ltri)

    tables = _routing_tables(cnt[0, :N_EXPERTS].astype(jnp.int32), meta[:, 0:TOP_K].astype(jnp.int32),
                             meta[:, TOP_K:2 * TOP_K].astype(jnp.int32))
    y_slots = _experts(*tables, h2p,
                       w_gate_e[0], b_gate_e[0].reshape(N_EXPERTS, 1, D_EXPERT),
                       w_up_e[0], b_up_e[0].reshape(N_EXPERTS, 1, D_EXPERT),
                       w_down_e[0], b_down_e[0].reshape(N_EXPERTS, 1, D_MODEL))
    out = _combine(meta, x1, y_slots)
    return out.reshape(BATCH, SEQ, D_MODEL)


def _routing_tables(counts, top_idx, rank):
    padded = ((counts + RT - 1) // RT) * RT
    pend = jnp.cumsum(padded)
    pstart = pend - padded
    dest = pstart[top_idx] + rank
    n_assign = TOP_K * N_TOK
    row_assign = jnp.full((ROWS,), -1, jnp.int32).at[dest.reshape(n_assign)].set(
        jnp.arange(n_assign, dtype=jnp.int32), unique_indices=True)

    items_per_e = (padded + R_MAX - 1) // R_MAX
    it_end = jnp.cumsum(items_per_e)
    it_start = it_end - items_per_e
    total_items = it_end[-1]
    wi = jnp.arange(N_ITEMS, dtype=jnp.int32)
    e_w = jnp.minimum(jnp.searchsorted(it_end, wi, side="right"), N_EXPERTS - 1).astype(jnp.int32)
    j_w = wi - it_start[e_w]
    live = wi < total_items
    rows_w = jnp.clip(padded[e_w] - j_w * R_MAX, 0, R_MAX)
    e_last = e_w[jnp.maximum(total_items - 1, 0)]
    item_e = jnp.where(live, e_w, e_last).astype(jnp.int32)
    item_row0 = jnp.where(live, pstart[e_w] + j_w * R_MAX, 0).astype(jnp.int32)
    item_nsub = jnp.where(live, rows_w // RT, 0).astype(jnp.int32)
    item_live = live.astype(jnp.int32)
    r = jnp.arange(R_MAX, dtype=jnp.int32)
    src = jnp.minimum(item_row0[:, None] + r[None, :], ROWS - 1)
    item_rows = jnp.where(r[None, :] < (item_nsub * RT)[:, None], row_assign[src], -1)
    item_tok = (jnp.maximum(item_rows, 0) // TOP_K).reshape(N_ITEMS, 1, R_MAX)
    item_dst = jnp.where(item_rows >= 0, (item_rows % TOP_K) * N_TOK + item_rows // TOP_K,
                         TOP_K * N_TOK + r[None, :]).reshape(N_ITEMS, 1, R_MAX)
    return item_e, item_live, item_tok, item_dst
```

```python
import functools
import math

import jax
import jax.numpy as jnp
import numpy as np
from jax import lax
from jax.experimental import pallas as pl
from jax.experimental.pallas import tpu as pltpu

D_MODEL = 2048
BATCH = 4
SEQ = 4096
N_TOK = BATCH * SEQ
N_Q_HEADS = 16
N_KV_HEADS = 4
HEAD_DIM = 64
Q_WIDTH = N_Q_HEADS * HEAD_DIM
KV_WIDTH = N_KV_HEADS * HEAD_DIM
WINDOW = 128
N_FOURIER_GROUPS = 4
FOURIER_GROUP_DIM = 256
FOURIER_WIDTH = N_FOURIER_GROUPS * FOURIER_GROUP_DIM
IN_WIDTH = FOURIER_WIDTH + Q_WIDTH + 2 * KV_WIDTH + 2 * D_MODEL
N_EXPERTS = 32
TOP_K = 4
D_EXPERT = D_MODEL
SWIGLU_LIMIT = 7.0
SWIGLU_ALPHA = 1.702
RMS_EPS = 1e-6
NEG_INF = -1e30

F32 = jnp.float32
BF16 = jnp.bfloat16
U32 = jnp.uint32

V7X_VMEM_BYTES = 64 * 1024 * 1024
LANES = 128
SUBLANES = 8
MIB = 1024 * 1024

TM_IN = 512
TN_IN = 512
NORM_ROWS = 64
TT_F = 512
F_DIRECT = SEQ // TT_F // 2 + 1
DFT_RADIX = 64
TQ = 512
QB = 128
TM_MG = 512
MG_SLAB = 256
RT = 256
R_MAX = 9 * RT
FC = 256
NF = D_EXPERT // FC
ROWS = TOP_K * N_TOK + N_EXPERTS * RT
N_ITEMS = (TOP_K * N_TOK + N_EXPERTS * (RT - 1) + N_EXPERTS * (R_MAX - RT)) // R_MAX + 1
TM_CB = 256
HALF = D_MODEL // 2
N_SUB = R_MAX // RT
GROUP = 3
OUT_CHUNK = 256
GATHER_PER_STEP = RT // NF
SLOT_ROWS = TOP_K * N_TOK + R_MAX


def _cparams(sem, vmem_mib):
    return pltpu.CompilerParams(dimension_semantics=sem, vmem_limit_bytes=vmem_mib * MIB)


def _dot(a, b):
    return jnp.dot(a, b, preferred_element_type=F32)


def _pack_halves(lo, hi):
    lo_bits = pltpu.bitcast(lo.astype(BF16).astype(F32), U32)
    hi_bits = pltpu.bitcast(hi.astype(BF16).astype(F32), U32)
    return (hi_bits & jnp.uint32(0xFFFF0000)) | (lo_bits >> jnp.uint32(16))


def _unpack_halves(words):
    lo = pltpu.bitcast(words << jnp.uint32(16), F32)
    hi = pltpu.bitcast(words & jnp.uint32(0xFFFF0000), F32)
    return lo, hi


def _store_row_tiles(tiles_ref, words):
    n_rows = words.shape[0]
    for j in range(SUBLANES):
        tiles_ref[pl.ds(j, n_rows, stride=SUBLANES), :] = words[:, j * LANES:(j + 1) * LANES]


def _load_row_tiles(tiles_ref, j, n_rows):
    return tiles_ref[pl.ds(j, n_rows, stride=SUBLANES), :]


def _in_proj_kernel(x_ref, g1_ref, w_ref, b_ref, qg_ref, kg_ref, ones_ref,
                    u_ref, q_ref, kv_ref, gate_ref, h_scr):
    def body(c, carry):
        rows = pl.ds(pl.multiple_of(c * NORM_ROWS, NORM_ROWS), NORM_ROWS)
        x = x_ref[rows, :]
        ms = jnp.mean(x * x, axis=-1, keepdims=True)
        h_scr[rows, :] = (x * lax.rsqrt(ms + RMS_EPS) * g1_ref[...]).astype(BF16)
        return carry
    lax.fori_loop(0, TM_IN // NORM_ROWS, body, 0)

    def head_norm(a, gain):
        ssq = _dot((a * a).astype(BF16), ones_ref[...])
        return a * lax.rsqrt(ssq * (1.0 / HEAD_DIM) + RMS_EPS) * gain

    h = h_scr[...]
    for c in range(IN_WIDTH // TN_IN):
        col0 = c * TN_IN
        acc = _dot(h, w_ref[:, col0:col0 + TN_IN])
        if col0 < FOURIER_WIDTH:
            u_ref[:, col0:col0 + TN_IN] = acc.astype(BF16)
        elif col0 < FOURIER_WIDTH + Q_WIDTH:
            o = col0 - FOURIER_WIDTH
            q_ref[:, o:o + TN_IN] = (head_norm(acc, qg_ref[...]) * (HEAD_DIM ** -0.5)).astype(BF16)
        elif col0 < FOURIER_WIDTH + Q_WIDTH + 2 * KV_WIDTH:
            lane = lax.broadcasted_iota(jnp.int32, acc.shape, 1)
            kv_ref[...] = jnp.where(lane < KV_WIDTH, head_norm(acc, kg_ref[...]), acc).astype(BF16)
        else:
            o = col0 - (FOURIER_WIDTH + Q_WIDTH + 2 * KV_WIDTH)
            z = acc + b_ref[:, o:o + TN_IN]
            gate_ref[:, o:o + TN_IN] = (1.0 / (1.0 + jnp.exp(-z))).astype(BF16)


def _in_proj(x2, g1, w_bf, bias, qg_t, kg_t, ones_bd):
    const = lambda i: (0, 0)
    row = lambda i: (i, 0)
    return pl.pallas_call(
        _in_proj_kernel,
        grid=(N_TOK // TM_IN,),
        in_specs=[
            pl.BlockSpec((TM_IN, D_MODEL), row),
            pl.BlockSpec((1, D_MODEL), const),
            pl.BlockSpec((D_MODEL, IN_WIDTH), const, pipeline_mode=pl.Buffered(1)),
            pl.BlockSpec((1, 2 * D_MODEL), const),
            pl.BlockSpec((1, TN_IN), const),
            pl.BlockSpec((1, TN_IN), const),
            pl.BlockSpec((TN_IN, TN_IN), const),
        ],
        out_specs=[
            pl.BlockSpec((TM_IN, FOURIER_WIDTH), row),
            pl.BlockSpec((TM_IN, Q_WIDTH), row),
            pl.BlockSpec((TM_IN, 2 * KV_WIDTH), row),
            pl.BlockSpec((TM_IN, 2 * D_MODEL), row),
        ],
        out_shape=[
            jax.ShapeDtypeStruct((N_TOK, FOURIER_WIDTH), BF16),
            jax.ShapeDtypeStruct((N_TOK, Q_WIDTH), BF16),
            jax.ShapeDtypeStruct((N_TOK, 2 * KV_WIDTH), BF16),
            jax.ShapeDtypeStruct((N_TOK, 2 * D_MODEL), BF16),
        ],
        scratch_shapes=[pltpu.VMEM((TM_IN, D_MODEL), BF16)],
        compiler_params=_cparams(("arbitrary",), 60),
        name="in_proj",
    )(x2, g1, w_bf, bias, qg_t, kg_t, ones_bd)


def _fourier_kernel(u_ref, f1r_ref, f1i_ref, gr_ref, gi_ref, cc_ref, sc_ref, rev_ref, y_ref, cs_scr, ss_scr, pq_scr):
    t = pl.program_id(0)
    b = pl.program_id(1)
    nt = SEQ // TT_F

    @pl.when(t < F_DIRECT)
    def _():
        @pl.when(b == 0)
        def _():
            gr = gr_ref[...]
            gi = gi_ref[...]
            for a in range(TT_F // DFT_RADIX):
                f1r = f1r_ref[a:a + 1, :]
                f1i = f1i_ref[a:a + 1, :]
                rows = slice(a * DFT_RADIX, (a + 1) * DFT_RADIX)
                cs_scr[rows, :] = (f1r * gr - f1i * gi).astype(BF16)
                ss_scr[rows, :] = (f1r * gi + f1i * gr).astype(BF16)

        u = u_ref[...]
        a_seq = _dot(cs_scr[...], u)
        b_seq = _dot(ss_scr[...], u)
        for g in range(N_FOURIER_GROUPS):
            cols = slice(g * FOURIER_GROUP_DIM, (g + 1) * FOURIER_GROUP_DIM)
            p = _dot(a_seq[:, cols].astype(BF16), cc_ref[...])
            q = _dot(b_seq[:, cols].astype(BF16), sc_ref[...])
            y_ref[:, cols] = (p - q).astype(BF16)

            @pl.when(t < nt - F_DIRECT + 1)
            def _():
                pq_scr[b, t, :, cols] = (p + q).astype(BF16)

    @pl.when(t >= F_DIRECT)
    def _():
        src = jnp.concatenate([pq_scr[b, nt - 1 - t], pq_scr[b, nt - t]], axis=0)
        y_ref[...] = _dot(rev_ref[...], src).astype(BF16)


def _fourier(u, f1r, f1i, gr, gi, cc, sc, rev):
    nt = SEQ // TT_F
    f1_rows = TT_F // DFT_RADIX
    const = lambda t, b: (0, 0)
    direct = lambda t: jnp.minimum(t, F_DIRECT - 1)
    return pl.pallas_call(
        _fourier_kernel,
        grid=(nt, BATCH),
        in_specs=[
            pl.BlockSpec((SEQ, FOURIER_WIDTH), lambda t, b: (jnp.where(t < F_DIRECT, b, BATCH - 1), 0)),
            pl.BlockSpec((f1_rows, SEQ), lambda t, b: (direct(t), 0)),
            pl.BlockSpec((f1_rows, SEQ), lambda t, b: (direct(t), 0)),
            pl.BlockSpec((DFT_RADIX, SEQ), const),
            pl.BlockSpec((DFT_RADIX, SEQ), const),
            pl.BlockSpec((FOURIER_GROUP_DIM, FOURIER_GROUP_DIM), const),
            pl.BlockSpec((FOURIER_GROUP_DIM, FOURIER_GROUP_DIM), const),
            pl.BlockSpec((TT_F, 2 * TT_F), const),
        ],
        out_specs=pl.BlockSpec((TT_F, FOURIER_WIDTH), lambda t, b: (b * nt + t, 0)),
        out_shape=jax.ShapeDtypeStruct((N_TOK, FOURIER_WIDTH), BF16),
        scratch_shapes=[pltpu.VMEM((TT_F, SEQ), BF16), pltpu.VMEM((TT_F, SEQ), BF16),
                        pltpu.VMEM((BATCH, nt - F_DIRECT + 1, TT_F, FOURIER_WIDTH), BF16)],
        compiler_params=_cparams(("arbitrary", "arbitrary"), 60),
        name="fourier",
    )(u, f1r, f1i, gr, gi, cc, sc, rev)


def _mirror_rows():
    rev = np.zeros((TT_F, 2 * TT_F), np.float32)
    rev[0, TT_F] = 1.0
    rev[np.arange(1, TT_F), TT_F - np.arange(1, TT_F)] = 1.0
    return jnp.asarray(rev).astype(BF16)


def _alibi_slope(h):
    return float(2.0 ** (-8.0 * (h + 1) / N_Q_HEADS))


def _attention_kernel(sink_ref, q_ref, prev_ref, cur_ref, next_ref, sel_ref, o_ref, kz_scr, vz_scr):
    t = pl.program_id(1)
    band = jnp.concatenate([prev_ref[...], cur_ref[...], next_ref[...]], axis=0)
    for g in range(N_KV_HEADS):
        c, half = divmod(g, 2)
        kcol = band[:, c * LANES:(c + 1) * LANES]
        vcol = band[:, KV_WIDTH + c * LANES:KV_WIDTH + (c + 1) * LANES]
        for d in range(2):
            sel = sel_ref[half * 2 + d]
            kz_scr[2 * g + d] = jnp.transpose(_dot(kcol, sel)).astype(BF16)
            vz_scr[2 * g + d] = _dot(vcol, sel).astype(BF16)

    row = lax.broadcasted_iota(jnp.int32, (2 * QB, 3 * QB), 0)
    col = lax.broadcasted_iota(jnp.int32, (2 * QB, 3 * QB), 1)
    absrel = jnp.abs((row & (QB - 1)) - col + QB)
    absrel_f = absrel.astype(F32)
    top = lax.broadcasted_iota(jnp.int32, (2 * QB, 1), 0) < QB
    for i in range(TQ // QB):
        q_rows = slice(i * QB, (i + 1) * QB)
        k_rows = slice(i * QB, (i + 3) * QB)
        kpos = t * TQ + (i - 1) * QB + col
        mask = (absrel <= WINDOW) & (kpos >= 0) & (kpos < SEQ)
        for g in range(N_KV_HEADS):
            cols = [slice((2 * g + p) * LANES, (2 * g + p + 1) * LANES) for p in range(2)]
            qq = jnp.concatenate([q_ref[q_rows, cols[0]], q_ref[q_rows, cols[1]]], axis=0)
            out = jnp.zeros((2 * QB, LANES), F32)
            for d in range(2):
                h_top, h_bot = 4 * g + d, 4 * g + 2 + d
                slope = jnp.where(top, _alibi_slope(h_top), _alibi_slope(h_bot))
                sink = jnp.where(top, sink_ref[h_top], sink_ref[h_bot])
                s = _dot(qq, kz_scr[2 * g + d, :, k_rows])
                s = jnp.where(mask, s - slope * absrel_f, NEG_INF)
                m = jnp.maximum(jnp.max(s, axis=-1, keepdims=True), sink)
                pr = jnp.exp(s - m)
                den = jnp.sum(pr, axis=-1, keepdims=True) + jnp.exp(sink - m)
                out = out + _dot(pr.astype(BF16), vz_scr[2 * g + d, k_rows, :]) / den
            o_ref[q_rows, cols[0]] = out[:QB].astype(BF16)
            o_ref[q_rows, cols[1]] = out[QB:].astype(BF16)


def _attention(sink, q, kv, sel):
    nt = SEQ // TQ
    nb = SEQ // QB
    r = TQ // QB
    grid_spec = pltpu.PrefetchScalarGridSpec(
        num_scalar_prefetch=1,
        grid=(BATCH, nt),
        in_specs=[
            pl.BlockSpec((TQ, Q_WIDTH), lambda b, t, s: (b * nt + t, 0)),
            pl.BlockSpec((QB, 2 * KV_WIDTH), lambda b, t, s: (b * nb + jnp.maximum(t * r - 1, 0), 0)),
            pl.BlockSpec((TQ, 2 * KV_WIDTH), lambda b, t, s: (b * nt + t, 0)),
            pl.BlockSpec((QB, 2 * KV_WIDTH), lambda b, t, s: (b * nb + jnp.minimum(t * r + r, nb - 1), 0)),
            pl.BlockSpec((4, LANES, LANES), lambda b, t, s: (0, 0, 0)),
        ],
        out_specs=pl.BlockSpec((TQ, Q_WIDTH), lambda b, t, s: (b * nt + t, 0)),
        scratch_shapes=[pltpu.VMEM((2 * N_KV_HEADS, LANES, TQ + 2 * QB), BF16),
                        pltpu.VMEM((2 * N_KV_HEADS, TQ + 2 * QB, LANES), BF16)],
    )
    return pl.pallas_call(
        _attention_kernel,
        grid_spec=grid_spec,
        out_shape=jax.ShapeDtypeStruct((N_TOK, Q_WIDTH), BF16),
        compiler_params=_cparams(("arbitrary", "arbitrary"), 32),
        name="attention",
    )(sink, q, kv, kv, kv, sel)


def _merge_kernel(yf_ref, at_ref, gf_ref, ga_ref, x_ref, wfo_ref, wao_ref, wo_ref, g2_ref, wr_ref, br_ref,
                  ltri_ref, x1_ref, h2p_ref, meta_ref, cnt_ref, base_scr):
    i = pl.program_id(0)

    @pl.when(i == 0)
    def _():
        base_scr[...] = jnp.zeros_like(base_scr)

    base = base_scr[...]
    for s in range(TM_MG // MG_SLAB):
        rows = slice(s * MG_SLAB, (s + 1) * MG_SLAB)
        yf = _dot(yf_ref[rows, :], wfo_ref[...])
        ya = _dot(at_ref[rows, :], wao_ref[...])
        merged = gf_ref[rows, :].astype(F32) * yf + ga_ref[rows, :].astype(F32) * ya
        x1 = x_ref[rows, :] + _dot(merged.astype(BF16), wo_ref[...])
        x1_ref[rows, :] = x1
        ms = jnp.mean(x1 * x1, axis=-1, keepdims=True)
        h2 = x1 * lax.rsqrt(ms + RMS_EPS) * g2_ref[...]
        _store_row_tiles(h2p_ref.at[pl.ds(s * MG_SLAB * SUBLANES, MG_SLAB * SUBLANES), :],
                         _pack_halves(h2[:, :HALF], h2[:, HALF:]))
        logits = _dot(h2.astype(BF16), wr_ref[...]) + br_ref[...]

        lane = lax.broadcasted_iota(jnp.int32, logits.shape, 1)
        lane_f = lane.astype(F32)
        vals = logits
        top_v, top_i, onehots = [], [], []
        for _ in range(TOP_K):
            m = jnp.max(vals, axis=-1, keepdims=True)
            idx = jnp.min(jnp.where(vals == m, lane_f, float(LANES)), axis=-1, keepdims=True)
            oh = lane_f == idx
            top_v.append(m)
            top_i.append(idx)
            onehots.append(oh)
            vals = jnp.where(oh, -jnp.inf, vals)
        exps = [jnp.exp(v - top_v[0]) for v in top_v]
        den = exps[0] + exps[1] + exps[2] + exps[3]
        gates = [e / den for e in exps]

        cnt = sum(oh.astype(F32) for oh in onehots)
        prefix = _dot(ltri_ref[...], cnt.astype(BF16))
        tot = base + prefix
        ranks = [jnp.sum(jnp.where(oh, tot, 0.0), axis=-1, keepdims=True) for oh in onehots]
        base = base + jnp.sum(cnt, axis=0, keepdims=True)

        meta = jnp.zeros(logits.shape, F32)
        for k in range(TOP_K):
            meta = jnp.where(lane == k, top_i[k], meta)
            meta = jnp.where(lane == TOP_K + k, ranks[k], meta)
            meta = jnp.where(lane == 2 * TOP_K + k, gates[k], meta)
        meta_ref[rows, :] = meta
    base_scr[...] = base
    cnt_ref[...] = jnp.broadcast_to(base, cnt_ref.shape)


def _merge(yf, attn, gates, x2, wfo, wao, wo, g2, wr, br, ltri):
    tm = TM_MG
    const = lambda i: (0, 0)
    resident = functools.partial(pl.BlockSpec, index_map=const, pipeline_mode=pl.Buffered(1))
    return pl.pallas_call(
        _merge_kernel,
        grid=(N_TOK // tm,),
        in_specs=[
            pl.BlockSpec((tm, FOURIER_WIDTH), lambda i: (i, 0)),
            pl.BlockSpec((tm, Q_WIDTH), lambda i: (i, 0)),
            pl.BlockSpec((tm, D_MODEL), lambda i: (i, 0)),
            pl.BlockSpec((tm, D_MODEL), lambda i: (i, 1)),
            pl.BlockSpec((tm, D_MODEL), lambda i: (i, 0)),
            resident((FOURIER_WIDTH, D_MODEL)),
            resident((Q_WIDTH, D_MODEL)),
            resident((D_MODEL, D_MODEL)),
            pl.BlockSpec((1, D_MODEL), const),
            pl.BlockSpec((D_MODEL, LANES), const),
            pl.BlockSpec((1, LANES), const),
            pl.BlockSpec((MG_SLAB, MG_SLAB), const),
        ],
        out_specs=[
            pl.BlockSpec((tm, D_MODEL), lambda i: (i, 0)),
            pl.BlockSpec((tm * SUBLANES, LANES), lambda i: (i, 0)),
            pl.BlockSpec((tm, LANES), lambda i: (i, 0)),
            pl.BlockSpec((8, LANES), const),
        ],
        out_shape=[
            jax.ShapeDtypeStruct((N_TOK, D_MODEL), F32),
            jax.ShapeDtypeStruct((N_TOK * SUBLANES, LANES), U32),
            jax.ShapeDtypeStruct((N_TOK, LANES), F32),
            jax.ShapeDtypeStruct((8, LANES), F32),
        ],
        scratch_shapes=[pltpu.VMEM((1, LANES), F32)],
        compiler_params=_cparams(("arbitrary",), 56),
        name="merge",
    )(yf, attn, gates, gates, x2, wfo, wao, wo, g2, wr, br, ltri)


def _expert_kernel(e_ref, live_ref, tok_cur_ref, tok_nxt_ref, dst_cur_ref,
                   h2p_hbm, wg_ref, bg_ref, wu_ref, bu_ref, wd_ref, bd_ref,
                   yslots_hbm, xu_scr, acc_scr, xs_scr, wgb, wub, wdb, gsem, ssem):
    del e_ref
    w = pl.program_id(0)
    f = pl.program_id(1)
    live = live_ref[w] > 0
    prev_live = (w > 0) & (live_ref[jnp.maximum(w - 1, 0)] > 0)
    cur = w & 1
    nxt = 1 - cur
    tile_rows = RT * SUBLANES

    def sub_rows(s):
        return pl.ds(pl.multiple_of(s * RT, RT), RT)

    def row_tile(row):
        return pl.ds(pl.multiple_of(row * SUBLANES, SUBLANES), SUBLANES)

    def gather_copy(tok_ref, slot, row):
        tok = tok_ref[0, 0, row]
        return pltpu.make_async_copy(h2p_hbm.at[row_tile(tok), :], xu_scr.at[slot, row_tile(row), :], gsem.at[slot])

    def scatter_copy(row):
        dst = dst_cur_ref[0, 0, row]
        return pltpu.make_async_copy(xu_scr.at[cur, row_tile(row), :], yslots_hbm.at[row_tile(dst), :], ssem)

    def for_all_rows(fn):
        def body(row, carry):
            fn(row)
            return carry
        lax.fori_loop(0, R_MAX, body, 0, unroll=8)

    def prefetch_next(s):
        base = s * RT + f * GATHER_PER_STEP
        for g in range(GATHER_PER_STEP):
            gather_copy(tok_nxt_ref, nxt, base + g).start(priority=g % 2)

    def cast_weights():
        wgb[...] = wg_ref[...].astype(BF16)
        wub[...] = wu_ref[...].astype(BF16)
        wdb[...] = wd_ref[...].astype(BF16)

    def stage_inputs(first_sub, n_sub, stage_row0):
        n_rows = n_sub * RT
        tiles = xu_scr.at[cur, pl.ds(pl.multiple_of(first_sub * tile_rows, tile_rows), n_sub * tile_rows), :]
        stage = xs_scr.at[pl.ds(stage_row0, n_rows), :]
        for j in range(SUBLANES):
            lo, hi = _unpack_halves(_load_row_tiles(tiles, j, n_rows))
            stage[:, j * LANES:(j + 1) * LANES] = lo.astype(BF16)
            stage[:, HALF + j * LANES:HALF + (j + 1) * LANES] = hi.astype(BF16)

    def activations(n_sub, stage_row0):
        x = xs_scr[pl.ds(stage_row0, n_sub * RT), :]
        g = _dot(x, wgb[...]) + bg_ref[...]
        u = _dot(x, wub[...]) + bu_ref[...]
        g = jnp.minimum(g, SWIGLU_LIMIT)
        u = jnp.clip(u, -SWIGLU_LIMIT, SWIGLU_LIMIT)
        act = (u + 1.0) * (g * (1.0 / (1.0 + jnp.exp(-SWIGLU_ALPHA * g))))
        return act.astype(BF16)

    def accumulate(j):
        n_rows = GROUP * RT
        rows = pl.ds(j * n_rows, n_rows)
        stage_inputs(j * GROUP, GROUP, 0)
        c = _dot(activations(GROUP, 0), wdb[...])
        acc_scr[rows, :] = jnp.where(f == 0, c, acc_scr[rows, :] + c)

    def finish(s):
        act = activations(1, (s % GROUP) * RT)
        rows = sub_rows(s)
        tiles = xu_scr.at[cur, pl.ds(s * tile_rows, tile_rows), :]
        for c in range(HALF // OUT_CHUNK):
            lo_cols = slice(c * OUT_CHUNK, (c + 1) * OUT_CHUNK)
            hi_cols = slice(HALF + c * OUT_CHUNK, HALF + (c + 1) * OUT_CHUNK)
            y_lo = acc_scr[rows, lo_cols] + _dot(act, wdb[:, lo_cols]) + bd_ref[:, lo_cols]
            y_hi = acc_scr[rows, hi_cols] + _dot(act, wdb[:, hi_cols]) + bd_ref[:, hi_cols]
            words = _pack_halves(y_lo, y_hi)
            for jj in range(OUT_CHUNK // LANES):
                j_tile = c * (OUT_CHUNK // LANES) + jj
                tiles[pl.ds(j_tile, RT, stride=SUBLANES), :] = words[:, jj * LANES:(jj + 1) * LANES]

    first_tile = pl.ds(0, SUBLANES)

    @pl.when(f == 0)
    def _():
        @pl.when(w == 0)
        def _():
            def zero(s, carry):
                acc_scr[sub_rows(s), :] = jnp.zeros((RT, D_MODEL), F32)
                xu_scr[1, pl.ds(pl.multiple_of(s * tile_rows, tile_rows), tile_rows), :] = (
                    jnp.zeros((tile_rows, LANES), U32))
                return carry
            lax.fori_loop(0, N_SUB, zero, 0)
            pad_rows = pl.ds(TOP_K * N_TOK * SUBLANES, R_MAX * SUBLANES)
            fill = pltpu.make_async_copy(xu_scr.at[1], yslots_hbm.at[pad_rows, :], ssem)
            fill.start()
            fill.wait()
            for_all_rows(lambda row: gather_copy(tok_cur_ref, cur, row).start())

        @pl.when((w == 0) | prev_live)
        def _():
            row_in = pltpu.make_async_copy(h2p_hbm.at[first_tile, :], xu_scr.at[cur, first_tile, :], gsem.at[cur])
            for_all_rows(lambda row: row_in.wait())

        @pl.when(prev_live)
        def _():
            row_out = pltpu.make_async_copy(xu_scr.at[cur, first_tile, :], yslots_hbm.at[first_tile, :], ssem)
            for_all_rows(lambda row: row_out.wait())

    @pl.when(live & (f < NF - 1))
    def _():
        cast_weights()
        for j in range(N_SUB // GROUP):
            accumulate(j)
            for i in range(GROUP):
                prefetch_next(j * GROUP + i)

    @pl.when(live & (f == NF - 1))
    def _():
        cast_weights()
        stage_inputs(0, 1, 0)
        for s in range(N_SUB):
            finish(s)
            if s + 1 < N_SUB:
                stage_inputs(s + 1, 1, ((s + 1) % GROUP) * RT)
            for r in range(RT):
                scatter_copy(s * RT + r).start(priority=r % 2)
            prefetch_next(s)


def _experts(item_e, item_live, item_tok, item_dst, h2p, wg, bg, wu, bu, wd, bd):
    def w_in_map(w, f, e, lv):
        return (e[w], 0, jnp.where(lv[w] > 0, f, NF - 1))

    def w_down_map(w, f, e, lv):
        return (e[w], jnp.where(lv[w] > 0, f, NF - 1), 0)

    def b_down_map(w, f, e, lv):
        return (e[w], 0, 0)

    grid_spec = pltpu.PrefetchScalarGridSpec(
        num_scalar_prefetch=2,
        grid=(jnp.sum(item_live) + 1, NF),
        in_specs=[
            pl.BlockSpec((1, 1, R_MAX), lambda w, f, e, lv: (w, 0, 0), memory_space=pltpu.SMEM),
            pl.BlockSpec((1, 1, R_MAX), lambda w, f, e, lv: (jnp.minimum(w + 1, N_ITEMS - 1), 0, 0),
                         memory_space=pltpu.SMEM),
            pl.BlockSpec((1, 1, R_MAX), lambda w, f, e, lv: (w, 0, 0), memory_space=pltpu.SMEM),
            pl.BlockSpec(memory_space=pl.ANY),
            pl.BlockSpec((None, D_MODEL, FC), w_in_map),
            pl.BlockSpec((None, 1, FC), w_in_map),
            pl.BlockSpec((None, D_MODEL, FC), w_in_map),
            pl.BlockSpec((None, 1, FC), w_in_map),
            pl.BlockSpec((None, FC, D_MODEL), w_down_map),
            pl.BlockSpec((None, 1, D_MODEL), b_down_map),
        ],
        out_specs=pl.BlockSpec(memory_space=pl.ANY),
        scratch_shapes=[
            pltpu.VMEM((2, R_MAX * SUBLANES, LANES), U32),
            pltpu.VMEM((R_MAX, D_MODEL), F32),
            pltpu.VMEM((GROUP * RT, D_MODEL), BF16),
            pltpu.VMEM((D_MODEL, FC), BF16),
            pltpu.VMEM((D_MODEL, FC), BF16),
            pltpu.VMEM((FC, D_MODEL), BF16),
            pltpu.SemaphoreType.DMA((2,)),
            pltpu.SemaphoreType.DMA(()),
        ],
    )
    return pl.pallas_call(
        _expert_kernel,
        grid_spec=grid_spec,
        out_shape=jax.ShapeDtypeStruct((SLOT_ROWS * SUBLANES, LANES), U32),
        compiler_params=_cparams(("arbitrary", "arbitrary"), 60),
        name="experts",
    )(item_e, item_live, item_tok, item_tok, item_dst, h2p, wg, bg, wu, bu, wd, bd)


def _combine_kernel(meta_ref, x1_ref, y0_ref, y1_ref, y2_ref, y3_ref, out_ref):
    gates = [meta_ref[:, 2 * TOP_K + k:2 * TOP_K + k + 1] for k in range(TOP_K)]
    for j in range(SUBLANES):
        lo_cols = slice(j * LANES, (j + 1) * LANES)
        hi_cols = slice(HALF + j * LANES, HALF + (j + 1) * LANES)
        lo_acc = x1_ref[:, lo_cols]
        hi_acc = x1_ref[:, hi_cols]
        for k, y_ref in enumerate((y0_ref, y1_ref, y2_ref, y3_ref)):
            lo, hi = _unpack_halves(_load_row_tiles(y_ref, j, TM_CB))
            lo_acc = lo_acc + gates[k] * lo
            hi_acc = hi_acc + gates[k] * hi
        out_ref[:, lo_cols] = lo_acc
        out_ref[:, hi_cols] = hi_acc


def _combine(meta, x1, y_slots):
    tm = TM_CB
    nblk = N_TOK // tm
    slot_specs = [pl.BlockSpec((tm * SUBLANES, LANES), lambda i, k=k: (k * nblk + i, 0)) for k in range(TOP_K)]
    return pl.pallas_call(
        _combine_kernel,
        grid=(nblk,),
        in_specs=[
            pl.BlockSpec((tm, LANES), lambda i: (i, 0)),
            pl.BlockSpec((tm, D_MODEL), lambda i: (i, 0)),
        ] + slot_specs,
        out_specs=pl.BlockSpec((tm, D_MODEL), lambda i: (i, 0)),
        out_shape=jax.ShapeDtypeStruct((N_TOK, D_MODEL), F32),
        compiler_params=_cparams(("arbitrary",), 32),
        name="combine",
    )(meta, x1, y_slots, y_slots, y_slots, y_slots)


def _dft_tables():
    r = DFT_RADIX
    k = np.arange(r)
    e64 = np.exp(2j * np.pi * np.outer(k, k) / r)
    e4096 = np.exp(2j * np.pi * np.outer(k, k) / SEQ)
    s = np.arange(SEQ)
    f1 = e64[:, s % r]
    g = e64[:, s // r] * e4096[:, s % r] / math.sqrt(SEQ)
    c = np.arange(FOURIER_GROUP_DIM)
    ang = 2.0 * np.pi * np.outer(c, c) / FOURIER_GROUP_DIM
    scale = 1.0 / math.sqrt(FOURIER_GROUP_DIM)
    as32 = lambda a: jnp.asarray(a.astype(np.float32))
    return (as32(f1.real), as32(f1.imag), as32(g.real), as32(g.imag),
            jnp.asarray((np.cos(ang) * scale).astype(np.float32)).astype(BF16),
            jnp.asarray((np.sin(ang) * scale).astype(np.float32)).astype(BF16))


def _head_selectors():
    i = np.arange(LANES)[:, None]
    j = np.arange(LANES)[None, :]
    out = np.zeros((4, LANES, LANES), np.float32)
    for src in range(2):
        for dst in range(2):
            out[2 * src + dst] = (i - HEAD_DIM * src == j - HEAD_DIM * dst) & (j // HEAD_DIM == dst)
    return jnp.asarray(out).astype(BF16)


def kernel(x, norm1_g, w_in, b_branch_gate, q_norm_g, k_norm_g, attn_sink, w_fourier_out, w_attn_out, w_o,
           norm2_g, w_router, b_router, w_gate_e, b_gate_e, w_up_e, b_up_e, w_down_e, b_down_e):
    b, s, d = x.shape
    assert (b, s, d) == (BATCH, SEQ, D_MODEL) and norm1_g.shape[0] == 1
    x2 = x.reshape(N_TOK, D_MODEL)

    heads_per_tile = TN_IN // HEAD_DIM
    ones_bd = jnp.asarray((np.arange(TN_IN)[:, None] // HEAD_DIM == np.arange(TN_IN)[None, :] // HEAD_DIM)
                          .astype(np.float32)).astype(BF16)
    u_f, q, kv, gates = _in_proj(
        x2, norm1_g[0].reshape(1, D_MODEL), w_in[0].astype(BF16), b_branch_gate[0].reshape(1, 2 * D_MODEL),
        jnp.tile(q_norm_g[0], heads_per_tile).reshape(1, TN_IN),
        jnp.tile(k_norm_g[0], heads_per_tile).reshape(1, TN_IN), ones_bd)
    y_f = _fourier(u_f, *_dft_tables(), _mirror_rows())
    attn = _attention(attn_sink[0], q, kv, _head_selectors())

    wr = jnp.zeros((D_MODEL, LANES), BF16).at[:, :N_EXPERTS].set(w_router[0].astype(BF16))
    br = jnp.full((1, LANES), NEG_INF, F32).at[0, :N_EXPERTS].set(b_router[0])
    ltri = jnp.asarray(np.tril(np.ones((MG_SLAB, MG_SLAB), np.float32), -1)).astype(BF16)
    x1, h2p, meta, cnt = _merge(y_f, attn, gates, x2, w_fourier_out[0].astype(BF16), w_attn_out[0].astype(BF16),
                                w_o[0].astype(BF16), norm2_g[0].reshape(1, D_MODEL), wr, br, ltri)

    tables = _routing_tables(cnt[0, :N_EXPERTS].astype(jnp.int32), meta[:, 0:TOP_K].astype(jnp.int32),
                             meta[:, TOP_K:2 * TOP_K].astype(jnp.int32))
    y_slots = _experts(*tables, h2p,
                       w_gate_e[0], b_gate_e[0].reshape(N_EXPERTS, 1, D_EXPERT),
                       w_up_e[0], b_up_e[0].reshape(N_EXPERTS, 1, D_EXPERT),
                       w_down_e[0], b_down_e[0].reshape(N_EXPERTS, 1, D_MODEL))
    out = _combine(meta, x1, y_slots)
    return out.reshape(BATCH, SEQ, D_MODEL)


def _routing_tables(counts, top_idx, rank):
    padded = ((counts + RT - 1) // RT) * RT
    pend = jnp.cumsum(padded)
    pstart = pend - padded
    dest = pstart[top_idx] + rank
    n_assign = TOP_K * N_TOK
    row_assign = jnp.full((ROWS,), -1, jnp.int32).at[dest.reshape(n_assign)].set(
        jnp.arange(n_assign, dtype=jnp.int32), unique_indices=True,
        mode="promise_in_bounds")

    items_per_e = (padded + R_MAX - 1) // R_MAX
    it_end = jnp.cumsum(items_per_e)
    it_start = it_end - items_per_e
    total_items = it_end[-1]
    wi = jnp.arange(N_ITEMS, dtype=jnp.int32)
    e_w = jnp.minimum(jnp.searchsorted(it_end, wi, side="right"), N_EXPERTS - 1).astype(jnp.int32)
    j_w = wi - it_start[e_w]
    live = wi < total_items
    rows_w = jnp.clip(padded[e_w] - j_w * R_MAX, 0, R_MAX)
    e_last = e_w[jnp.maximum(total_items - 1, 0)]
    item_e = jnp.where(live, e_w, e_last).astype(jnp.int32)
    item_row0 = jnp.where(live, pstart[e_w] + j_w * R_MAX, 0).astype(jnp.int32)
    item_nsub = jnp.where(live, rows_w // RT, 0).astype(jnp.int32)
    item_live = live.astype(jnp.int32)
    r = jnp.arange(R_MAX, dtype=jnp.int32)
    src = jnp.minimum(item_row0[:, None] + r[None, :], ROWS - 1)
    item_rows = jnp.where(r[None, :] < (item_nsub * RT)[:, None], row_assign[src], -1)
    item_tok = (jnp.maximum(item_rows, 0) // TOP_K).reshape(N_ITEMS, 1, R_MAX)
    item_dst = jnp.where(item_rows >= 0, (item_rows % TOP_K) * N_TOK + item_rows // TOP_K,
                         TOP_K * N_TOK + r[None, :]).reshape(N_ITEMS, 1, R_MAX)
    return item_e, item_live, item_tok, item_dst
```
